```python
import math
import jax, jax.numpy as jnp
from jax import lax
import numpy as np

D_MODEL = 1024
BATCH = 16
SEQ = 256
DEPTH = 2
DEC_BATCH = 8
DEC_SEQ = 1024
PAST_LEN = 512

GRID_W = 64
MIX_W = D_MODEL
ATT_W = MIX_W // 2
SSM_W = MIX_W - ATT_W
N_HEADS = 4
D_V = ATT_W // N_HEADS
D_SUB = D_V // 2
ROPE_AXIS = D_SUB // 2
ROPE_THETA = 10000.0
Q_BLOCK = 128
SSM_CH = 16
SSM_GROUPS = SSM_W // SSM_CH
SSM_P = 64
D_FF = 2 * D_MODEL
CONV_W = 3
N_MOD = 6
EPS = 1e-6

kernel_name = 'hybrid_diffattn_s5_dit_step'


def rms_norm(x, g):
    xf = x.astype(jnp.float32)
    y = xf * lax.rsqrt(jnp.mean(xf * xf, axis=-1, keepdims=True) + EPS)
    return (y * g.astype(jnp.float32)).astype(x.dtype)


def adaln_mod(cond, w_mod_l, b_mod_l):
    m = jax.nn.silu(cond) @ w_mod_l + b_mod_l
    sh1, sc1, g1, sh2, sc2, g2 = jnp.split(m[:, None, :], N_MOD, axis=-1)
    return sh1, sc1, g1, sh2, sc2, g2


def modulate(x, g, shift, scale):
    return rms_norm(x, g) * (1.0 + scale) + shift


def axial_rope_tables(n_tokens):
    rows = n_tokens // GRID_W
    f32 = jnp.float32
    row = jnp.repeat(jnp.arange(rows, dtype=f32), GRID_W, total_repeat_length=n_tokens)
    col = jnp.tile(jnp.arange(GRID_W, dtype=f32), rows)
    nf = ROPE_AXIS // 2
    inv_freq = 1.0 / (ROPE_THETA ** (jnp.arange(nf, dtype=f32) / nf))
    ang_r = row[:, None] * inv_freq
    ang_c = col[:, None] * inv_freq
    return jnp.cos(ang_r), jnp.sin(ang_r), jnp.cos(ang_c), jnp.sin(ang_c)


def rotate(x, cos, sin):
    cos = cos[:, None, None, :]
    sin = sin[:, None, None, :]
    x1, x2 = jnp.split(x, 2, axis=-1)
    return jnp.concatenate([x1 * cos - x2 * sin, x1 * sin + x2 * cos], axis=-1)


def axial_rope(x, tables):
    cr, sr, cc, sc = tables
    xr, xc = jnp.split(x, 2, axis=-1)
    return jnp.concatenate([rotate(xr, cr, sr), rotate(xc, cc, sc)], axis=-1).astype(x.dtype)


def diff_attention(q, k, v, lam):
    B, Lq = q.shape[0], q.shape[1]
    nb = Lq // Q_BLOCK
    qb = q.reshape(B, nb, Q_BLOCK, N_HEADS, 2, D_SUB).transpose(1, 0, 2, 3, 4, 5)
    vf = v.astype(jnp.float32)
    scale = D_SUB ** -0.5

    def one_block(qblk):
        s = jnp.einsum('bqhsd,bkhsd->bhsqk', qblk, k).astype(jnp.float32) * scale
        p = jax.nn.softmax(s, axis=-1)
        a = p[:, :, 0] - lam * p[:, :, 1]
        return jnp.einsum('bhqk,bkhd->bqhd', a, vf)

    o = lax.map(one_block, qb)
    return o.transpose(1, 0, 2, 3, 4).reshape(B, Lq, N_HEADS, D_V)


def _lin_rec(e1, e2):
    a1, b1 = e1
    a2, b2 = e2
    return a1 * a2, a2 * b1 + b2


def to_complex(s):
    return lax.complex(s[..., 0].astype(jnp.float32), s[..., 1].astype(jnp.float32))


def to_real(h):
    return jnp.stack([jnp.real(h), jnp.imag(h)], axis=-1)


def s5_scan(ug, h0, lam_re, lam_im, log_step, b_re, b_im, c_re, c_im, reverse):
    f32 = jnp.float32
    lam = lax.complex(lam_re.astype(f32), lam_im.astype(f32))
    delta = jnp.exp(log_step.astype(f32))[:, None]
    lam_bar = jnp.exp(lam * delta)
    b_bar = ((lam_bar - 1.0) / lam)[..., None] * lax.complex(b_re.astype(f32), b_im.astype(f32))
    c = lax.complex(c_re.astype(f32), c_im.astype(f32))
    bu = jnp.einsum('blgc,gpc->blgp', ug, b_bar)
    edge = -1 if reverse else 0
    bu = bu.at[:, edge].add(lam_bar * h0)
    a = jnp.broadcast_to(lam_bar, bu.shape)
    _, h = lax.associative_scan(_lin_rec, (a, bu), reverse=reverse, axis=1)
    y = jnp.real(jnp.einsum('blgp,gcp->blgc', h, c))
    return y, h[:, edge]


def s5_bidirectional(u, h0, lam_re, lam_im, log_step, b_re, b_im, c_re, c_im, d):
    B, L, _ = u.shape
    uf = u.astype(jnp.float32)
    ug = uf.reshape(B, L, SSM_GROUPS, SSM_CH).astype(jnp.complex64)
    y_f, h_f = s5_scan(ug, h0[:, 0], lam_re[0], lam_im[0], log_step[0], b_re[0], b_im[0], c_re[0], c_im[0], False)
    y_b, h_b = s5_scan(ug, h0[:, 1], lam_re[1], lam_im[1], log_step[1], b_re[1], b_im[1], c_re[1], c_im[1], True)
    y = (y_f + y_b).reshape(B, L, SSM_W) + d.astype(jnp.float32) * uf
    return y, jnp.stack([h_f, h_b], axis=1)


def half_glu(y, w, b):
    z = jax.nn.gelu(y)
    return z * jax.nn.sigmoid(z @ w.astype(jnp.float32) + b.astype(jnp.float32))


def conv_ffn(h, w_up_l, conv_w_l, conv_b_l, w_down_l):
    up = h @ w_up_l
    L = up.shape[1]
    p = jnp.pad(up, ((0, 0), (1, 1), (0, 0)))
    z = p[:, :L] * conv_w_l[0] + p[:, 1:L + 1] * conv_w_l[1] + p[:, 2:] * conv_w_l[2] + conv_b_l
    val, gate = jnp.split(z, 2, axis=-1)
    return (jax.nn.silu(gate) * val) @ w_down_l


def setup_inputs(seed: int = 0) -> dict:
    key = jax.random.key(seed)
    ks = jax.random.split(key, 40)
    f32 = jnp.float32

    def nrm(i, shape, s):
        return jax.random.normal(ks[i], shape, f32) * s

    F2 = 2 * D_FF
    G, P, CH = SSM_GROUPS, SSM_P, SSM_CH
    return {
        'x_prompt': nrm(0, (BATCH, SEQ, D_MODEL), 1.0),
        'x_sample': nrm(1, (DEC_BATCH, DEC_SEQ, D_MODEL), 1.0),
        'cache_k': nrm(2, (DEC_BATCH, DEPTH, PAST_LEN, N_HEADS, 2, D_SUB), 1.0),
        'cache_v': nrm(3, (DEC_BATCH, DEPTH, PAST_LEN, N_HEADS, D_V), 0.5),
        'state_ssm': nrm(4, (DEC_BATCH, DEPTH, 2, G, P, 2), 0.3),
        'c': nrm(5, (DEC_BATCH, D_MODEL), 1.0),
        'c_ctx': nrm(6, (D_MODEL,), 1.0),
        'w_mod': nrm(7, (DEPTH, D_MODEL, N_MOD * D_MODEL), 0.5 * D_MODEL ** -0.5),
        'b_mod': nrm(8, (DEPTH, N_MOD * D_MODEL), 0.01),
        'g_norm1': 1.0 + nrm(9, (DEPTH, D_MODEL), 0.02),
        'w_in': nrm(10, (DEPTH, D_MODEL, 3 * ATT_W + SSM_W), D_MODEL ** -0.5),
        'q_norm': 1.0 + nrm(11, (DEPTH, D_SUB), 0.02),
        'k_norm': 1.0 + nrm(12, (DEPTH, D_SUB), 0.02),
        'lambda_q1': nrm(13, (DEPTH, D_SUB), 0.1),
        'lambda_k1': nrm(14, (DEPTH, D_SUB), 0.1),
        'lambda_q2': nrm(15, (DEPTH, D_SUB), 0.1),
        'lambda_k2': nrm(16, (DEPTH, D_SUB), 0.1),
        'subln_g': 1.0 + nrm(17, (DEPTH, D_V), 0.02),
        'ssm_lambda_re': -0.5 + nrm(18, (DEPTH, 2, G, P), 0.01),
        'ssm_lambda_im': jnp.pi * jnp.arange(P, dtype=f32) + nrm(19, (DEPTH, 2, G, P), 0.01),
        'ssm_log_step': jax.random.uniform(ks[20], (DEPTH, 2, G), f32, math.log(1e-3), math.log(1e-1)),
        'ssm_b_re': nrm(21, (DEPTH, 2, G, P, CH), (2 * CH) ** -0.5),
        'ssm_b_im': nrm(22, (DEPTH, 2, G, P, CH), (2 * CH) ** -0.5),
        'ssm_c_re': nrm(23, (DEPTH, 2, G, CH, P), (2 * P) ** -0.5),
        'ssm_c_im': nrm(24, (DEPTH, 2, G, CH, P), (2 * P) ** -0.5),
        'ssm_d': nrm(25, (DEPTH, SSM_W), 1.0),
        'w_glu': nrm(26, (DEPTH, SSM_W, SSM_W), SSM_W ** -0.5),
        'b_glu': nrm(27, (DEPTH, SSM_W), 0.01),
        'w_out': nrm(28, (DEPTH, MIX_W, D_MODEL), MIX_W ** -0.5),
        'g_norm2': 1.0 + nrm(29, (DEPTH, D_MODEL), 0.02),
        'w_up': nrm(30, (DEPTH, D_MODEL, F2), D_MODEL ** -0.5),
        'conv_w': nrm(31, (DEPTH, CONV_W, F2), 0.3) + jnp.array([0.0, 1.0, 0.0], f32)[None, :, None],
        'conv_b': nrm(32, (DEPTH, F2), 0.01),
        'w_down': nrm(33, (DEPTH, D_FF, D_MODEL), D_FF ** -0.5),
    }


def reference(x_prompt, x_sample, cache_k, cache_v, state_ssm, c, c_ctx, w_mod, b_mod, g_norm1, w_in,
              q_norm, k_norm, lambda_q1, lambda_k1, lambda_q2, lambda_k2, subln_g,
              ssm_lambda_re, ssm_lambda_im, ssm_log_step, ssm_b_re, ssm_b_im, ssm_c_re, ssm_c_im, ssm_d,
              w_glu, b_glu, w_out, g_norm2, w_up, conv_w, conv_b, w_down):
    f32 = jnp.float32

    def layer(x, cond, l, tables, ctx_k, ctx_v, h0):
        sh1, sc1, gt1, sh2, sc2, gt2 = adaln_mod(cond, w_mod[l], b_mod[l])
        B, L, _ = x.shape
        h = modulate(x, g_norm1[l], sh1, sc1)
        q, k, v, u = jnp.split(h @ w_in[l], [ATT_W, 2 * ATT_W, 3 * ATT_W], axis=-1)
        q = rms_norm(q.reshape(B, L, N_HEADS, 2, D_SUB), q_norm[l])
        k = rms_norm(k.reshape(B, L, N_HEADS, 2, D_SUB), k_norm[l])
        v = v.reshape(B, L, N_HEADS, D_V)
        if tables is None:
            keys, vals = k, v
        else:
            q = axial_rope(q, tables)
            keys = jnp.concatenate([ctx_k.astype(k.dtype), axial_rope(k, tables)], axis=1)
            vals = jnp.concatenate([ctx_v.astype(v.dtype), v], axis=1)
        lam_init = 0.8 - 0.6 * math.exp(-0.3 * l)
        lam = (jnp.exp(jnp.sum(lambda_q1[l].astype(f32) * lambda_k1[l].astype(f32)))
               - jnp.exp(jnp.sum(lambda_q2[l].astype(f32) * lambda_k2[l].astype(f32))) + lam_init)
        o_att = diff_attention(q, keys, vals, lam)
        o_att = (rms_norm(o_att, subln_g[l]) * (1.0 - lam_init)).reshape(B, L, ATT_W).astype(x.dtype)
        y_ssm, h_last = s5_bidirectional(u, h0, ssm_lambda_re[l], ssm_lambda_im[l], ssm_log_step[l],
                                         ssm_b_re[l], ssm_b_im[l], ssm_c_re[l], ssm_c_im[l], ssm_d[l])
        o_ssm = half_glu(y_ssm, w_glu[l], b_glu[l]).astype(x.dtype)
        x = x + gt1 * (jnp.concatenate([o_att, o_ssm], axis=-1) @ w_out[l])
        h2 = modulate(x, g_norm2[l], sh2, sc2)
        x = x + gt2 * conv_ffn(h2, w_up[l], conv_w[l], conv_b[l], w_down[l])
        return x, k, v, h_last

    cond_ctx = c_ctx[None, :]
    h0_zero = jnp.zeros((x_prompt.shape[0], 2, SSM_GROUPS, SSM_P), jnp.complex64)
    xp = x_prompt
    ks_, vs_, ss_ = [], [], []
    for l in range(DEPTH):
        xp, k_l, v_l, h_l = layer(xp, cond_ctx, l, None, None, None, h0_zero)
        ks_.append(k_l)
        vs_.append(v_l)
        ss_.append(to_real(h_l))
    new_k = jnp.stack(ks_, axis=1)
    new_v = jnp.stack(vs_, axis=1)
    new_ssm = jnp.stack(ss_, axis=1)

    tables = axial_rope_tables(x_sample.shape[1])
    xs = x_sample
    for l in range(DEPTH):
        xs, _, _, _ = layer(xs, c, l, tables, cache_k[:, l], cache_v[:, l], to_complex(state_ssm[:, l]))

    return (xp, xs, new_k, new_v, new_ssm)
```

```python
import functools
import math

import jax
import jax.numpy as jnp
from jax import lax
from jax.experimental import pallas as pl
from jax.experimental.pallas import tpu as pltpu

F32 = jnp.float32
BF16 = jnp.bfloat16

D_MODEL = 1024
DEPTH = 2
GRID_W = 64
ATT_W = 512
SSM_W = 512
N_HEADS = 4
D_V = 128
D_SUB = 64
ROPE_AXIS = 32
ROPE_THETA = 10000.0
SSM_CH = 16
SSM_GROUPS = 32
SSM_P = 64
D_FF = 2048
N_MOD = 6
EPS = 1e-6

LANES = 128
CHUNK = 32
ROW_TILE = 256
FFN_ROWS = 1024
FFN_SLAB = 512
MOD_COLS = 1536
VMEM_LIMIT = 56 * 1024 * 1024


def _nt(a, b):
    return lax.dot_general(a, b, (((1,), (1,)), ((), ())), preferred_element_type=F32)


def _nn(a, b):
    return jnp.dot(a, b, preferred_element_type=F32)


def _split_bf16(a):
    hi = a.astype(BF16)
    lo = (a - hi.astype(F32)).astype(BF16)
    return hi, lo


def _nt3(a, b):
    ah, al = _split_bf16(a)
    bh, bl = _split_bf16(b)
    return _nt(ah, bh) + _nt(ah, bl) + _nt(al, bh)


def _sigmoid(x):
    return 1.0 / (1.0 + jnp.exp(-x))


def _mod_kernel(c_ref, w_ref, b_ref, o_ref):
    c = c_ref[...]
    a = (c * _sigmoid(c)).astype(BF16)
    o_ref[0] = _nn(a, w_ref[0].astype(BF16)) + b_ref[0]


def _modulation(cond, w_mod, b_mod):
    nb = cond.shape[0]
    ncol = N_MOD * D_MODEL
    return pl.pallas_call(
        _mod_kernel,
        grid=(DEPTH, ncol // MOD_COLS),
        in_specs=[
            pl.BlockSpec((nb, D_MODEL), lambda l, j: (0, 0)),
            pl.BlockSpec((1, D_MODEL, MOD_COLS), lambda l, j: (l, 0, j)),
            pl.BlockSpec((1, 1, MOD_COLS), lambda l, j: (l, 0, j)),
        ],
        out_specs=pl.BlockSpec((1, nb, MOD_COLS), lambda l, j: (l, 0, j)),
        out_shape=jax.ShapeDtypeStruct((DEPTH, nb, ncol), F32),
        compiler_params=pltpu.CompilerParams(vmem_limit_bytes=VMEM_LIMIT),
        name="adaln_mod",
    )(cond, w_mod, b_mod.reshape(DEPTH, 1, ncol))


def _premix_kernel(x_ref, mod_ref, g1_ref, win_ref, gmat_ref, qg_ref, kg_ref, *rest, rope):
    if rope:
        cos_ref, sina_ref, sinb_ref, q_ref, k_ref, v_ref, u_ref = rest
    else:
        q_ref, k_ref, v_ref, u_ref = rest
    x = x_ref[...]
    mod = mod_ref[0]
    sh1 = mod[0:1]
    sc1 = mod[1:2]
    ms = jnp.mean(x * x, axis=-1, keepdims=True)
    h = x * lax.rsqrt(ms + EPS) * g1_ref[...]
    h = h * (1.0 + sc1) + sh1
    qkvu = _nn(h.astype(BF16), win_ref[...])

    def head_norm(t, g):
        msq = _nn((t * t).astype(BF16), gmat_ref[...])
        return t * lax.rsqrt(msq + EPS) * g

    def rotate(t):
        outs = []
        for j in range(ATT_W // LANES):
            s = t[:, j * LANES:(j + 1) * LANES]
            outs.append(s * cos_ref[...] + pltpu.roll(s, LANES - 16, 1) * sina_ref[...]
                        + pltpu.roll(s, 16, 1) * sinb_ref[...])
        return jnp.concatenate(outs, axis=1)

    q = head_norm(qkvu[:, 0:ATT_W], qg_ref[...])
    k = head_norm(qkvu[:, ATT_W:2 * ATT_W], kg_ref[...])
    if rope:
        q = rotate(q)
        k = rotate(k)
    q_ref[...] = (q * (D_SUB ** -0.5)).astype(q_ref.dtype)
    k_ref[...] = k.astype(k_ref.dtype)
    v_ref[...] = qkvu[:, 2 * ATT_W:3 * ATT_W].astype(v_ref.dtype)
    u_ref[...] = qkvu[:, 3 * ATT_W:].astype(u_ref.dtype)


def _premix(x, mods_l, g1, win, gmat, qg, kg, rope_tabs, seq_len, mod_base, kv_dtype):
    n = x.shape[0]
    tm = ROW_TILE
    per_seq = seq_len // tm
    rope = rope_tabs is not None
    if mod_base == 0:
        mod_map = lambda i: (0, 0, 0)
    else:
        mod_map = lambda i: (mod_base + i // per_seq, 0, 0)
    row = lambda i: (i, 0)
    const = lambda i: (0, 0)
    in_specs = [
        pl.BlockSpec((tm, D_MODEL), row),
        pl.BlockSpec((1, N_MOD, D_MODEL), mod_map),
        pl.BlockSpec((1, D_MODEL), const),
        pl.BlockSpec((D_MODEL, 4 * ATT_W), const),
        pl.BlockSpec((ATT_W, ATT_W), const),
        pl.BlockSpec((1, ATT_W), const),
        pl.BlockSpec((1, ATT_W), const),
    ]
    args = [x, mods_l, g1, win, gmat, qg, kg]
    if rope:
        tab = lambda i: (i % per_seq, 0)
        in_specs += [pl.BlockSpec((tm, LANES), tab)] * 3
        args += list(rope_tabs)
    out = pl.BlockSpec((tm, ATT_W), row)
    return pl.pallas_call(
        functools.partial(_premix_kernel, rope=rope),
        grid=(n // tm,),
        in_specs=in_specs,
        out_specs=[out, out, out, out],
        out_shape=[jax.ShapeDtypeStruct((n, ATT_W), BF16),
                   jax.ShapeDtypeStruct((n, ATT_W), kv_dtype),
                   jax.ShapeDtypeStruct((n, ATT_W), kv_dtype),
                   jax.ShapeDtypeStruct((n, ATT_W), BF16)],
        compiler_params=pltpu.CompilerParams(dimension_semantics=("parallel",),
                                             vmem_limit_bytes=VMEM_LIMIT),
        name="premix_rope" if rope else "premix",
    )(*args)


def _attn_kernel(q_ref, k_ref, v_ref, lamp_ref, sg_ref, *rest, lam_init, has_cache):
    if has_cache:
        kc_ref, vc_ref, o_ref = rest
    else:
        (o_ref,) = rest
    tq = q_ref.shape[0]
    lp = lamp_ref[...]
    l1 = jnp.sum(lp[0:1] * lp[1:2], axis=-1, keepdims=True)
    l2 = jnp.sum(lp[2:3] * lp[3:4], axis=-1, keepdims=True)
    lam = jnp.exp(l1) - jnp.exp(l2) + lam_init
    first = lax.broadcasted_iota(jnp.int32, (1, LANES), 1) < D_SUB
    for h in range(N_HEADS):
        sl = slice(h * D_V, (h + 1) * D_V)
        qh = q_ref[:, sl]
        zero = jnp.zeros_like(qh)
        qq = jnp.concatenate([jnp.where(first, qh, zero), jnp.where(first, zero, qh)], axis=0)
        kn = k_ref[:, sl].astype(BF16)
        vn = v_ref[:, sl].astype(BF16)
        if has_cache:
            ncache = kc_ref.shape[1]
            sc = jnp.concatenate([_nt(qq, kc_ref[0, :, sl]), _nt(qq, kn)], axis=1)
        else:
            sc = _nt(qq, kn)
        m = jnp.max(sc, axis=-1, keepdims=True)
        e = jnp.exp(sc - m)
        ssum = jnp.sum(e, axis=-1, keepdims=True)
        eb = e.astype(BF16)
        if has_cache:
            pv = _nn(eb[:, :ncache], vc_ref[0, :, sl]) + _nn(eb[:, ncache:], vn)
        else:
            pv = _nn(eb, vn)
        on = pv * (1.0 / ssum)
        o = on[:tq] - lam * on[tq:]
        ms = jnp.mean(o * o, axis=-1, keepdims=True)
        o_ref[:, sl] = (o * lax.rsqrt(ms + EPS) * sg_ref[:, sl] * (1.0 - lam_init)).astype(o_ref.dtype)


def _attention(q, k, v, lamp, sg, cache, batch, seq_len, lam_init):
    n = q.shape[0]
    tq = ROW_TILE
    nq = seq_len // tq
    has_cache = cache is not None
    in_specs = [
        pl.BlockSpec((tq, ATT_W), lambda b, i: (b * nq + i, 0)),
        pl.BlockSpec((seq_len, ATT_W), lambda b, i: (b, 0)),
        pl.BlockSpec((seq_len, ATT_W), lambda b, i: (b, 0)),
        pl.BlockSpec((4, D_SUB), lambda b, i: (0, 0)),
        pl.BlockSpec((1, ATT_W), lambda b, i: (0, 0)),
    ]
    args = [q, k, v, lamp, sg]
    if has_cache:
        past = cache[0].shape[1]
        in_specs += [pl.BlockSpec((1, past, ATT_W), lambda b, i: (b, 0, 0))] * 2
        args += list(cache)
    return pl.pallas_call(
        functools.partial(_attn_kernel, lam_init=lam_init, has_cache=has_cache),
        grid=(batch, nq),
        in_specs=in_specs,
        out_specs=pl.BlockSpec((tq, ATT_W), lambda b, i: (b * nq + i, 0)),
        out_shape=jax.ShapeDtypeStruct((n, ATT_W), BF16),
        compiler_params=pltpu.CompilerParams(dimension_semantics=("parallel", "parallel"),
                                             vmem_limit_bytes=VMEM_LIMIT),
        name="attn_cache" if has_cache else "attn",
    )(*args)


def _ssm_kernel(ut_ref, h0_ref, p1_ref, pb_ref, yt_ref, hf_ref, r_scr, hin_scr, *,
                ctx_batch, ctx_chunks, lat_batch, lat_chunks):
    T = CHUNK
    W = SSM_CH * T
    lane = lax.broadcasted_iota(jnp.int32, (1, LANES), 1)
    lo = lane < SSM_P
    sgn_lr = jnp.where(lo, 1.0, -1.0).astype(F32)
    sgn_rl = -sgn_lr
    srow = lax.broadcasted_iota(jnp.int32, (T, LANES), 0).astype(F32)

    def swap(a):
        return pltpu.roll(a, SSM_P, 1)

    def tile_time(a):
        return jnp.concatenate([a] * SSM_CH, axis=0)

    def rep_chan(a):
        return jnp.concatenate([jnp.broadcast_to(a[i:i + 1, :], (T, LANES)) for i in range(SSM_CH)], axis=0)

    def pair_pack(y):
        ys = swap(y)
        return jnp.where(lo, y, ys), jnp.where(lo, ys, y) * sgn_rl

    def cmul_const(x, yr2, yi2s):
        return x * yr2 + swap(x) * yi2s

    p1 = p1_ref[0]
    pb = pb_ref[0]

    def direction(d):
        lre2 = p1[3 * d + 0:3 * d + 1]
        lim2 = p1[3 * d + 1:3 * d + 2]
        ls2 = p1[3 * d + 2:3 * d + 3]
        bre2, bim2, cre2, cim2 = pb[4 * d + 0], pb[4 * d + 1], pb[4 * d + 2], pb[4 * d + 3]
        delta = jnp.exp(ls2)
        xr = lre2 * delta
        th = lim2 * delta

        def cpow_const(k):
            return jnp.exp(k * xr) * jnp.where(lo, jnp.cos(k * th), jnp.sin(k * th))

        mag1 = jnp.exp(xr)
        lbr2 = mag1 * jnp.cos(th)
        lbi2 = mag1 * jnp.sin(th)
        den = lre2 * lre2 + lim2 * lim2
        nr = lbr2 - 1.0
        cr2 = (nr * lre2 + lbi2 * lim2) / den
        ci2 = (lbi2 * lre2 - nr * lim2) / den
        bbr2 = rep_chan(cr2 * bre2 - ci2 * bim2)
        bbi2s = rep_chan((cr2 * bim2 + ci2 * bre2) * sgn_rl)
        ccr2 = rep_chan(cre2)
        cci2s = rep_chan(cim2 * sgn_rl)

        ang = srow * th
        cs = jnp.where(lo, jnp.cos(ang), jnp.sin(ang))
        wp = jnp.exp(srow * xr) * cs
        wn = jnp.exp(-(srow * xr)) * cs * sgn_lr

        def times(x, c1, c2s):
            return tile_time(x) * c1 + tile_time(swap(x)) * c2s

        lam_t = pair_pack(cpow_const(float(T)))
        if d == 0:
            left = times(wn, bbr2, bbi2s)
            bs = times(cmul_const(wn, *pair_pack(cpow_const(float(T - 1)))), bbr2, bbi2s)
            right = times(wp, ccr2, cci2s)
            cc = times(cmul_const(wp, *pair_pack(cpow_const(1.0))), ccr2, cci2s)
        else:
            left = times(wp, bbr2, bbi2s)
            bs = left
            right = times(wn, ccr2, cci2s)
            cc = times(cmul_const(wn, *lam_t), ccr2, cci2s)
        bbar = jnp.where(lo, bbr2, bbi2s)
        return left * sgn_lr, right, bs, cc * sgn_lr, lam_t, bbar

    lf, rf, bsf, ccf, lamt_f, bbar_f = direction(0)
    lb, rb, bsb, ccb, lamt_b, bbar_b = direction(1)

    s_idx = lax.broadcasted_iota(jnp.int32, (W, W), 0) % T
    t_idx = lax.broadcasted_iota(jnp.int32, (W, W), 1) % T
    a_t = jnp.where(t_idx >= s_idx, _nt3(lf, rf), 0.0) + jnp.where(s_idx >= t_idx, _nt3(lb, rb), 0.0)
    w1 = jnp.concatenate([a_t, bsf, swap(bsf), bsb, swap(bsb)], axis=1).astype(BF16)
    r_scr[...] = _nn(ut_ref[0], w1)

    def chain(row0, nb, nchunks, col, lam_t, h, reverse):
        a1, a2s = lam_t
        hs = swap(h)
        order = range(nchunks - 1, -1, -1) if reverse else range(nchunks)
        hcol = slice(LANES, 2 * LANES) if reverse else slice(0, LANES)
        for c in order:
            rows = slice(row0 + c * nb, row0 + (c + 1) * nb)
            hin_scr[rows, hcol] = h
            s = r_scr[rows, col:col + LANES]
            ss = r_scr[rows, col + LANES:col + 2 * LANES]
            h, hs = h * a1 + hs * a2s + s, hs * a1 - h * a2s + ss
        return h

    zero = jnp.zeros((ctx_batch, LANES), F32)
    chain(0, ctx_batch, ctx_chunks, W, lamt_f, zero, False)
    chain(0, ctx_batch, ctx_chunks, W + 2 * LANES, lamt_b, zero, True)
    s_tiled = tile_time(srow)
    first_tok = jnp.where(s_tiled == 0.0, bbar_f, 0.0).astype(BF16)
    last_tok = jnp.where(s_tiled == float(T - 1), bbar_b, 0.0).astype(BF16)
    last_rows = (ctx_chunks - 1) * ctx_batch
    hf_ref[0, :, 0:LANES] = _nn(ut_ref[0, 0:ctx_batch, :], first_tok)
    hf_ref[0, :, LANES:2 * LANES] = _nn(ut_ref[0, last_rows:last_rows + ctx_batch, :], last_tok)
    lat0 = ctx_batch * ctx_chunks
    h0 = h0_ref[0]
    chain(lat0, lat_batch, lat_chunks, W, lamt_f, h0[:, 0:LANES], False)
    chain(lat0, lat_batch, lat_chunks, W + 2 * LANES, lamt_b, h0[:, LANES:2 * LANES], True)

    cc_cat = jnp.concatenate([ccf, ccb], axis=1).astype(BF16)
    yt_ref[0] = r_scr[:, 0:W] + _nt(hin_scr[...].astype(BF16), cc_cat)


def _ssm(ut, h0, p1, pb, ctx_batch, ctx_chunks, lat_batch, lat_chunks):
    g, n, w = ut.shape
    return pl.pallas_call(
        functools.partial(_ssm_kernel, ctx_batch=ctx_batch, ctx_chunks=ctx_chunks,
                          lat_batch=lat_batch, lat_chunks=lat_chunks),
        grid=(g,),
        in_specs=[
            pl.BlockSpec((1, n, w), lambda i: (i, 0, 0)),
            pl.BlockSpec((1, lat_batch, 2 * LANES), lambda i: (i, 0, 0)),
            pl.BlockSpec((1, 8, LANES), lambda i: (i, 0, 0)),
            pl.BlockSpec((1, 8, SSM_CH, LANES), lambda i: (i, 0, 0, 0)),
        ],
        out_specs=[
            pl.BlockSpec((1, n, w), lambda i: (i, 0, 0)),
            pl.BlockSpec((1, ctx_batch, 2 * LANES), lambda i: (i, 0, 0)),
        ],
        out_shape=[jax.ShapeDtypeStruct((g, n, w), F32),
                   jax.ShapeDtypeStruct((g, ctx_batch, 2 * LANES), F32)],
        scratch_shapes=[pltpu.VMEM((n, w + 4 * LANES), F32), pltpu.VMEM((n, 2 * LANES), F32)],
        compiler_params=pltpu.CompilerParams(dimension_semantics=("parallel",),
                                             vmem_limit_bytes=VMEM_LIMIT),
        name="s5_scan",
    )(ut, h0, p1, pb)


def _postmix_kernel(x_ref, oatt_ref, y_ref, u_ref, mod_ref, d_ref, wglu_ref, bglu_ref, wout_ref, g2_ref,
                    x1_ref, h2_ref):
    mod = mod_ref[0]
    gt1 = mod[2:3]
    sh2 = mod[3:4]
    sc2 = mod[4:5]
    y = y_ref[...] + d_ref[...] * u_ref[...].astype(F32)
    z = jax.nn.gelu(y)
    gate = _sigmoid(_nn(z.astype(BF16), wglu_ref[...]) + bglu_ref[...])
    o_ssm = (z * gate).astype(BF16)
    mix = _nn(oatt_ref[...], wout_ref[0:ATT_W, :]) + _nn(o_ssm, wout_ref[ATT_W:, :])
    x1 = x_ref[...] + gt1 * mix
    x1_ref[...] = x1
    ms = jnp.mean(x1 * x1, axis=-1, keepdims=True)
    h2 = x1 * lax.rsqrt(ms + EPS) * g2_ref[...]
    h2_ref[...] = (h2 * (1.0 + sc2) + sh2).astype(BF16)


def _postmix(x, oatt, y, u, mods_l, d, wglu, bglu, wout, g2, seq_len, mod_base):
    n = x.shape[0]
    tm = ROW_TILE
    per_seq = seq_len // tm
    if mod_base == 0:
        mod_map = lambda i: (0, 0, 0)
    else:
        mod_map = lambda i: (mod_base + i // per_seq, 0, 0)
    row = lambda i: (i, 0)
    const = lambda i: (0, 0)
    return pl.pallas_call(
        _postmix_kernel,
        grid=(n // tm,),
        in_specs=[
            pl.BlockSpec((tm, D_MODEL), row),
            pl.BlockSpec((tm, ATT_W), row),
            pl.BlockSpec((tm, SSM_W), row),
            pl.BlockSpec((tm, SSM_W), row),
            pl.BlockSpec((1, N_MOD, D_MODEL), mod_map),
            pl.BlockSpec((1, SSM_W), const),
            pl.BlockSpec((SSM_W, SSM_W), const),
            pl.BlockSpec((1, SSM_W), const),
            pl.BlockSpec((D_MODEL, D_MODEL), const),
            pl.BlockSpec((1, D_MODEL), const),
        ],
        out_specs=[pl.BlockSpec((tm, D_MODEL), row), pl.BlockSpec((tm, D_MODEL), row)],
        out_shape=[jax.ShapeDtypeStruct((n, D_MODEL), F32), jax.ShapeDtypeStruct((n, D_MODEL), BF16)],
        compiler_params=pltpu.CompilerParams(dimension_semantics=("parallel",),
                                             vmem_limit_bytes=VMEM_LIMIT),
        name="postmix",
    )(x, oatt, y, u, mods_l, d, wglu, bglu, wout, g2)


def _ffn_kernel(h2_ref, x1_ref, mod_ref, wv_ref, wg_ref, cwv_ref, cwg_ref, cbv_ref, cbg_ref, wd_ref,
                o_ref, acc_ref, *, seq_len):
    j = pl.program_id(1)
    rows = h2_ref.shape[0]
    h2 = h2_ref[...]
    t = lax.broadcasted_iota(jnp.int32, (rows, 1), 0) % seq_len
    has_prev = t != 0
    has_next = t != seq_len - 1

    def conv(up, cw, cb):
        prev = jnp.where(has_prev, pltpu.roll(up, 1, 0), 0.0)
        nxt = jnp.where(has_next, pltpu.roll(up, rows - 1, 0), 0.0)
        return prev * cw[0:1] + up * cw[1:2] + nxt * cw[2:3] + cb

    val = conv(_nn(h2, wv_ref[...]), cwv_ref[...], cbv_ref[...])
    gate = conv(_nn(h2, wg_ref[...]), cwg_ref[...], cbg_ref[...])
    act = (gate * _sigmoid(gate) * val).astype(BF16)
    part = _nn(act, wd_ref[...])

    @pl.when(j == 0)
    def _():
        acc_ref[...] = part

    @pl.when(j != 0)
    def _():
        acc_ref[...] += part

    @pl.when(j == pl.num_programs(1) - 1)
    def _():
        o_ref[...] = x1_ref[...] + mod_ref[0][5:6] * acc_ref[...]


def _ffn(h2, x1, mods_l, wup, cw, cb, wdown, seq_len, mod_base):
    n = h2.shape[0]
    rows = FFN_ROWS
    per_seq = seq_len // rows if seq_len >= rows else 1
    tf = FFN_SLAB
    nslab = D_FF // tf
    if mod_base == 0:
        mod_map = lambda i, j: (0, 0, 0)
    else:
        mod_map = lambda i, j: (mod_base + i // per_seq, 0, 0)
    return pl.pallas_call(
        functools.partial(_ffn_kernel, seq_len=seq_len),
        grid=(n // rows, nslab),
        in_specs=[
            pl.BlockSpec((rows, D_MODEL), lambda i, j: (i, 0)),
            pl.BlockSpec((rows, D_MODEL), lambda i, j: (i, 0)),
            pl.BlockSpec((1, N_MOD, D_MODEL), mod_map),
            pl.BlockSpec((D_MODEL, tf), lambda i, j: (0, j)),
            pl.BlockSpec((D_MODEL, tf), lambda i, j: (0, nslab + j)),
            pl.BlockSpec((3, tf), lambda i, j: (0, j)),
            pl.BlockSpec((3, tf), lambda i, j: (0, nslab + j)),
            pl.BlockSpec((1, tf), lambda i, j: (0, j)),
            pl.BlockSpec((1, tf), lambda i, j: (0, nslab + j)),
            pl.BlockSpec((tf, D_MODEL), lambda i, j: (j, 0)),
        ],
        out_specs=pl.BlockSpec((rows, D_MODEL), lambda i, j: (i, 0)),
        out_shape=jax.ShapeDtypeStruct((n, D_MODEL), F32),
        scratch_shapes=[pltpu.VMEM((rows, D_MODEL), F32)],
        compiler_params=pltpu.CompilerParams(dimension_semantics=("parallel", "arbitrary"),
                                             vmem_limit_bytes=VMEM_LIMIT),
        name="conv_ffn",
    )(h2, x1, mods_l, wup, wup, cw, cw, cb, cb, wdown)


def _rope_tables(n_tokens):
    pos = jnp.arange(n_tokens, dtype=F32)
    row = jnp.floor(pos / GRID_W)
    col = pos - row * GRID_W
    nf = ROPE_AXIS // 2
    inv_freq = 1.0 / (ROPE_THETA ** (jnp.arange(nf, dtype=F32) / nf))
    lane = jnp.arange(LANES)
    freq = inv_freq[lane % nf]
    is_col = (lane % D_SUB) >= ROPE_AXIS
    ang = jnp.where(is_col[None, :], col[:, None], row[:, None]) * freq[None, :]
    second = (lane % ROPE_AXIS) >= nf
    cos = jnp.cos(ang)
    sin = jnp.sin(ang)
    sina = jnp.where(second[None, :], 0.0, -sin)
    sinb = jnp.where(second[None, :], sin, 0.0)
    return cos, sina, sinb


def _to_chunks(u, batch, seq_len):
    nc = seq_len // CHUNK
    u = u.reshape(batch, nc, CHUNK, SSM_GROUPS, SSM_CH)
    u = u.transpose(3, 1, 0, 4, 2)
    return u.reshape(SSM_GROUPS, nc * batch, SSM_CH * CHUNK)


def _from_chunks(y, batch, seq_len):
    nc = seq_len // CHUNK
    y = y.reshape(SSM_GROUPS, nc, batch, SSM_CH, CHUNK)
    y = y.transpose(2, 1, 4, 0, 3)
    return y.reshape(batch * seq_len, SSM_W)


def _dup_lanes(a):
    return jnp.concatenate([a, a], axis=-1)


def kernel(x_prompt, x_sample, cache_k, cache_v, state_ssm, c, c_ctx, w_mod, b_mod, g_norm1, w_in, q_norm, k_norm, lambda_q1, lambda_k1, lambda_q2, lambda_k2, subln_g, ssm_lambda_re, ssm_lambda_im, ssm_log_step, ssm_b_re, ssm_b_im, ssm_c_re, ssm_c_im, ssm_d, w_glu, b_glu, w_out, g_norm2, w_up, conv_w, conv_b, w_down):
    nb_ctx, len_ctx, _ = x_prompt.shape
    nb_lat, len_lat, _ = x_sample.shape
    past = cache_k.shape[2]

    n_cond = 16
    cond = jnp.zeros((n_cond, D_MODEL), F32).at[0].set(c_ctx).at[1:1 + nb_lat].set(c)
    mods = _modulation(cond, w_mod, b_mod).reshape(DEPTH, n_cond, N_MOD, D_MODEL)

    gidx = jnp.arange(ATT_W) // D_SUB
    gmat = jnp.where(gidx[:, None] == gidx[None, :], 1.0 / D_SUB, 0.0).astype(BF16)
    rope_tabs = _rope_tables(len_lat)

    xp = x_prompt.reshape(nb_ctx * len_ctx, D_MODEL)
    xs = x_sample.reshape(nb_lat * len_lat, D_MODEL)
    ks, vs, ss = [], [], []
    for l in range(DEPTH):
        lam_init = 0.8 - 0.6 * math.exp(-0.3 * l)
        mods_l = mods[l]
        g1 = g_norm1[l].reshape(1, D_MODEL)
        g2 = g_norm2[l].reshape(1, D_MODEL)
        win = w_in[l].astype(BF16)
        wout = w_out[l].astype(BF16)
        wglu = w_glu[l].astype(BF16)
        wup = w_up[l].astype(BF16)
        wdown = w_down[l].astype(BF16)
        qg = jnp.tile(q_norm[l], ATT_W // D_SUB).reshape(1, ATT_W)
        kg = jnp.tile(k_norm[l], ATT_W // D_SUB).reshape(1, ATT_W)
        sg = jnp.tile(subln_g[l], N_HEADS).reshape(1, ATT_W)
        lamp = jnp.stack([lambda_q1[l], lambda_k1[l], lambda_q2[l], lambda_k2[l]])
        d = ssm_d[l].reshape(1, SSM_W)
        bglu = b_glu[l].reshape(1, SSM_W)
        cb = conv_b[l].reshape(1, 2 * D_FF)
        cw = conv_w[l]
        kc = cache_k[:, l].reshape(nb_lat, past, ATT_W).astype(BF16)
        vc = cache_v[:, l].reshape(nb_lat, past, ATT_W).astype(BF16)

        zeros = jnp.zeros((SSM_GROUPS, SSM_P), F32)
        step = jnp.broadcast_to(ssm_log_step[l][:, :, None], (2, SSM_GROUPS, SSM_P))
        p1 = _dup_lanes(jnp.stack([ssm_lambda_re[l, 0], ssm_lambda_im[l, 0], step[0],
                                   ssm_lambda_re[l, 1], ssm_lambda_im[l, 1], step[1], zeros, zeros], axis=1))
        bt = lambda a: a.transpose(0, 2, 1)
        pb = _dup_lanes(jnp.stack([bt(ssm_b_re[l, 0]), bt(ssm_b_im[l, 0]), ssm_c_re[l, 0], ssm_c_im[l, 0],
                                   bt(ssm_b_re[l, 1]), bt(ssm_b_im[l, 1]), ssm_c_re[l, 1], ssm_c_im[l, 1]], axis=1))
        h0 = state_ssm[:, l].transpose(2, 0, 1, 4, 3).reshape(SSM_GROUPS, nb_lat, 4 * SSM_P)

        qc, kc_new, vc_new, uc = _premix(xp, mods_l, g1, win, gmat, qg, kg, None, len_ctx, 0, F32)
        ql, kl, vl, ul = _premix(xs, mods_l, g1, win, gmat, qg, kg, rope_tabs, len_lat, 1, BF16)

        oc = _attention(qc, kc_new, vc_new, lamp, sg, None, nb_ctx, len_ctx, lam_init)
        ol = _attention(ql, kl, vl, lamp, sg, (kc, vc), nb_lat, len_lat, lam_init)

        ut = jnp.concatenate([_to_chunks(uc, nb_ctx, len_ctx), _to_chunks(ul, nb_lat, len_lat)], axis=1)
        n_ctx_rows = nb_ctx * (len_ctx // CHUNK)
        yt, hfin = _ssm(ut, h0, p1, pb, nb_ctx, len_ctx // CHUNK, nb_lat, len_lat // CHUNK)
        yc = _from_chunks(yt[:, :n_ctx_rows], nb_ctx, len_ctx)
        yl = _from_chunks(yt[:, n_ctx_rows:], nb_lat, len_lat)

        x1c, h2c = _postmix(xp, oc, yc, uc, mods_l, d, wglu, bglu, wout, g2, len_ctx, 0)
        x1l, h2l = _postmix(xs, ol, yl, ul, mods_l, d, wglu, bglu, wout, g2, len_lat, 1)

        xp = _ffn(h2c, x1c, mods_l, wup, cw, cb, wdown, len_ctx, 0)
        xs = _ffn(h2l, x1l, mods_l, wup, cw, cb, wdown, len_lat, 1)

        ks.append(kc_new.reshape(nb_ctx, len_ctx, N_HEADS, 2, D_SUB))
        vs.append(vc_new.reshape(nb_ctx, len_ctx, N_HEADS, D_V))
        ss.append(hfin.reshape(SSM_GROUPS, nb_ctx, 2, 2, SSM_P).transpose(1, 2, 0, 4, 3))

    return (xp.reshape(nb_ctx, len_ctx, D_MODEL), xs.reshape(nb_lat, len_lat, D_MODEL),
            jnp.stack(ks, axis=1), jnp.stack(vs, axis=1), jnp.stack(ss, axis=1))
```

```python
import functools
import math

import jax
import jax.numpy as jnp
from jax import lax
from jax.experimental import pallas as pl
from jax.experimental.pallas import tpu as pltpu

F32 = jnp.float32
BF16 = jnp.bfloat16

D_MODEL = 1024
DEPTH = 2
GRID_W = 64
ATT_W = 512
SSM_W = 512
N_HEADS = 4
D_V = 128
D_SUB = 64
ROPE_AXIS = 32
ROPE_THETA = 10000.0
SSM_CH = 16
SSM_GROUPS = 32
SSM_P = 64
D_FF = 2048
N_MOD = 6
EPS = 1e-6

LANES = 128
BF16_ROWS = 16
GROUPS_PER_BLOCK = LANES // SSM_CH
CHUNK = 32
ROW_TILE = 256
FFN_ROWS = 1024
FFN_SLAB = 512
MOD_COLS = 1536
VMEM_LIMIT = 56 * 1024 * 1024


def _nt(a, b):
    return lax.dot_general(a, b, (((1,), (1,)), ((), ())), preferred_element_type=F32)


def _nn(a, b):
    return jnp.dot(a, b, preferred_element_type=F32)


def _split_bf16(a):
    hi = a.astype(BF16)
    lo = (a - hi.astype(F32)).astype(BF16)
    return hi, lo


def _nt3(a, b):
    ah, al = _split_bf16(a)
    bh, bl = _split_bf16(b)
    return _nt(ah, bh) + _nt(ah, bl) + _nt(al, bh)


def _sigmoid(x):
    return 1.0 / (1.0 + jnp.exp(-x))


def _mod_kernel(c_ref, w_ref, b_ref, o_ref):
    c = c_ref[...]
    a = (c * _sigmoid(c)).astype(BF16)
    o_ref[0] = _nn(a, w_ref[0].astype(BF16)) + b_ref[0]


def _modulation(cond, w_mod, b_mod):
    nb = cond.shape[0]
    ncol = N_MOD * D_MODEL
    return pl.pallas_call(
        _mod_kernel,
        grid=(DEPTH, ncol // MOD_COLS),
        in_specs=[
            pl.BlockSpec((nb, D_MODEL), lambda l, j: (0, 0)),
            pl.BlockSpec((1, D_MODEL, MOD_COLS), lambda l, j: (l, 0, j)),
            pl.BlockSpec((1, 1, MOD_COLS), lambda l, j: (l, 0, j)),
        ],
        out_specs=pl.BlockSpec((1, nb, MOD_COLS), lambda l, j: (l, 0, j)),
        out_shape=jax.ShapeDtypeStruct((DEPTH, nb, ncol), F32),
        compiler_params=pltpu.CompilerParams(vmem_limit_bytes=VMEM_LIMIT),
        name="adaln_mod",
    )(cond, w_mod, b_mod.reshape(DEPTH, 1, ncol))


def _premix_kernel(x_ref, mod_ref, g1_ref, win_ref, gmat_ref, qg_ref, kg_ref, *rest, rope):
    if rope:
        cos_ref, sina_ref, sinb_ref, q_ref, k_ref, v_ref, u_ref = rest
    else:
        q_ref, k_ref, v_ref, u_ref = rest
    x = x_ref[...]
    mod = mod_ref[0]
    sh1 = mod[0:1]
    sc1 = mod[1:2]
    ms = jnp.mean(x * x, axis=-1, keepdims=True)
    h = x * lax.rsqrt(ms + EPS) * g1_ref[...]
    h = h * (1.0 + sc1) + sh1
    qkvu = _nn(h.astype(BF16), win_ref[...])

    def head_norm(t, g):
        msq = _nn((t * t).astype(BF16), gmat_ref[...])
        return t * lax.rsqrt(msq + EPS) * g

    def rotate(t):
        outs = []
        for j in range(ATT_W // LANES):
            s = t[:, j * LANES:(j + 1) * LANES]
            outs.append(s * cos_ref[...] + pltpu.roll(s, LANES - 16, 1) * sina_ref[...]
                        + pltpu.roll(s, 16, 1) * sinb_ref[...])
        return jnp.concatenate(outs, axis=1)

    q = head_norm(qkvu[:, 0:ATT_W], qg_ref[...])
    k = head_norm(qkvu[:, ATT_W:2 * ATT_W], kg_ref[...])
    if rope:
        q = rotate(q)
        k = rotate(k)
    q_ref[...] = (q * (D_SUB ** -0.5)).astype(q_ref.dtype)
    k_ref[...] = k.astype(k_ref.dtype)
    v_ref[...] = qkvu[:, 2 * ATT_W:3 * ATT_W].astype(v_ref.dtype)
    u_ref[...] = qkvu[:, 3 * ATT_W:].astype(u_ref.dtype)


def _premix(x, mods_l, g1, win, gmat, qg, kg, rope_tabs, seq_len, mod_base, kv_dtype):
    n = x.shape[0]
    tm = ROW_TILE
    per_seq = seq_len // tm
    rope = rope_tabs is not None
    if mod_base == 0:
        mod_map = lambda i: (0, 0, 0)
    else:
        mod_map = lambda i: (mod_base + i // per_seq, 0, 0)
    row = lambda i: (i, 0)
    const = lambda i: (0, 0)
    in_specs = [
        pl.BlockSpec((tm, D_MODEL), row),
        pl.BlockSpec((1, N_MOD, D_MODEL), mod_map),
        pl.BlockSpec((1, D_MODEL), const),
        pl.BlockSpec((D_MODEL, 4 * ATT_W), const),
        pl.BlockSpec((ATT_W, ATT_W), const),
        pl.BlockSpec((1, ATT_W), const),
        pl.BlockSpec((1, ATT_W), const),
    ]
    args = [x, mods_l, g1, win, gmat, qg, kg]
    if rope:
        tab = lambda i: (i % per_seq, 0)
        in_specs += [pl.BlockSpec((tm, LANES), tab)] * 3
        args += list(rope_tabs)
    out = pl.BlockSpec((tm, ATT_W), row)
    return pl.pallas_call(
        functools.partial(_premix_kernel, rope=rope),
        grid=(n // tm,),
        in_specs=in_specs,
        out_specs=[out, out, out, out],
        out_shape=[jax.ShapeDtypeStruct((n, ATT_W), BF16),
                   jax.ShapeDtypeStruct((n, ATT_W), kv_dtype),
                   jax.ShapeDtypeStruct((n, ATT_W), kv_dtype),
                   jax.ShapeDtypeStruct((n, SSM_W), F32)],
        compiler_params=pltpu.CompilerParams(dimension_semantics=("parallel",),
                                             vmem_limit_bytes=VMEM_LIMIT),
        name="premix_rope" if rope else "premix",
    )(*args)


def _attn_kernel(q_ref, k_ref, v_ref, lamp_ref, sg_ref, *rest, lam_init, has_cache):
    if has_cache:
        kc_ref, vc_ref, o_ref = rest
    else:
        (o_ref,) = rest
    tq = q_ref.shape[0]
    lp = lamp_ref[...]
    l1 = jnp.sum(lp[0:1] * lp[1:2], axis=-1, keepdims=True)
    l2 = jnp.sum(lp[2:3] * lp[3:4], axis=-1, keepdims=True)
    lam = jnp.exp(l1) - jnp.exp(l2) + lam_init
    first = lax.broadcasted_iota(jnp.int32, (1, LANES), 1) < D_SUB
    for h in range(N_HEADS):
        sl = slice(h * D_V, (h + 1) * D_V)
        qh = q_ref[:, sl]
        zero = jnp.zeros_like(qh)
        qq = jnp.concatenate([jnp.where(first, qh, zero), jnp.where(first, zero, qh)], axis=0)
        kn = k_ref[:, sl].astype(BF16)
        vn = v_ref[:, sl].astype(BF16)
        if has_cache:
            ncache = kc_ref.shape[1]
            sc = jnp.concatenate([_nt(qq, kc_ref[0, :, sl]), _nt(qq, kn)], axis=1)
        else:
            sc = _nt(qq, kn)
        m = jnp.max(sc, axis=-1, keepdims=True)
        e = jnp.exp(sc - m)
        ssum = jnp.sum(e, axis=-1, keepdims=True)
        eb = e.astype(BF16)
        if has_cache:
            pv = _nn(eb[:, :ncache], vc_ref[0, :, sl]) + _nn(eb[:, ncache:], vn)
        else:
            pv = _nn(eb, vn)
        on = pv * (1.0 / ssum)
        o = on[:tq] - lam * on[tq:]
        ms = jnp.mean(o * o, axis=-1, keepdims=True)
        o_ref[:, sl] = (o * lax.rsqrt(ms + EPS) * sg_ref[:, sl] * (1.0 - lam_init)).astype(o_ref.dtype)


def _attention(q, k, v, lamp, sg, cache, batch, seq_len, lam_init):
    n = q.shape[0]
    tq = ROW_TILE
    nq = seq_len // tq
    has_cache = cache is not None
    in_specs = [
        pl.BlockSpec((tq, ATT_W), lambda b, i: (b * nq + i, 0)),
        pl.BlockSpec((seq_len, ATT_W), lambda b, i: (b, 0)),
        pl.BlockSpec((seq_len, ATT_W), lambda b, i: (b, 0)),
        pl.BlockSpec((4, D_SUB), lambda b, i: (0, 0)),
        pl.BlockSpec((1, ATT_W), lambda b, i: (0, 0)),
    ]
    args = [q, k, v, lamp, sg]
    if has_cache:
        past = cache[0].shape[1]
        in_specs += [pl.BlockSpec((1, past, ATT_W), lambda b, i: (b, 0, 0))] * 2
        args += list(cache)
    return pl.pallas_call(
        functools.partial(_attn_kernel, lam_init=lam_init, has_cache=has_cache),
        grid=(batch, nq),
        in_specs=in_specs,
        out_specs=pl.BlockSpec((tq, ATT_W), lambda b, i: (b * nq + i, 0)),
        out_shape=jax.ShapeDtypeStruct((n, ATT_W), BF16),
        compiler_params=pltpu.CompilerParams(dimension_semantics=("parallel", "parallel"),
                                             vmem_limit_bytes=VMEM_LIMIT),
        name="attn_cache" if has_cache else "attn",
    )(*args)


def _ssm_kernel(ut_ref, h0_ref, p1_ref, pb_ref, yt_ref, hf_ref, r_scr, hin_scr, *,
                ctx_batch, ctx_chunks, lat_batch, lat_chunks):
    T = CHUNK
    W = SSM_CH * T
    lane = lax.broadcasted_iota(jnp.int32, (1, LANES), 1)
    lo = lane < SSM_P
    sgn_lr = jnp.where(lo, 1.0, -1.0).astype(F32)
    sgn_rl = -sgn_lr
    srow = lax.broadcasted_iota(jnp.int32, (T, LANES), 0).astype(F32)

    def swap(a):
        return pltpu.roll(a, SSM_P, 1)

    def tile_time(a):
        return jnp.concatenate([jnp.broadcast_to(a[i:i + 1, :], (SSM_CH, LANES)) for i in range(T)], axis=0)

    def rep_chan(a):
        return jnp.concatenate([a] * T, axis=0)

    def pair_pack(y):
        ys = swap(y)
        return jnp.where(lo, y, ys), jnp.where(lo, ys, y) * sgn_rl

    def cmul_const(x, yr2, yi2s):
        return x * yr2 + swap(x) * yi2s

    p1 = p1_ref[0]
    pb = pb_ref[0]

    def direction(d):
        lre2 = p1[3 * d + 0:3 * d + 1]
        lim2 = p1[3 * d + 1:3 * d + 2]
        ls2 = p1[3 * d + 2:3 * d + 3]
        bre2, bim2, cre2, cim2 = pb[4 * d + 0], pb[4 * d + 1], pb[4 * d + 2], pb[4 * d + 3]
        delta = jnp.exp(ls2)
        xr = lre2 * delta
        th = lim2 * delta

        def cpow_const(k):
            return jnp.exp(k * xr) * jnp.where(lo, jnp.cos(k * th), jnp.sin(k * th))

        mag1 = jnp.exp(xr)
        lbr2 = mag1 * jnp.cos(th)
        lbi2 = mag1 * jnp.sin(th)
        den = lre2 * lre2 + lim2 * lim2
        nr = lbr2 - 1.0
        cr2 = (nr * lre2 + lbi2 * lim2) / den
        ci2 = (lbi2 * lre2 - nr * lim2) / den
        bbr2 = rep_chan(cr2 * bre2 - ci2 * bim2)
        bbi2s = rep_chan((cr2 * bim2 + ci2 * bre2) * sgn_rl)
        ccr2 = rep_chan(cre2)
        cci2s = rep_chan(cim2 * sgn_rl)

        ang = srow * th
        cs = jnp.where(lo, jnp.cos(ang), jnp.sin(ang))
        wp = jnp.exp(srow * xr) * cs
        wn = jnp.exp(-(srow * xr)) * cs * sgn_lr

        def times(x, c1, c2s):
            return tile_time(x) * c1 + tile_time(swap(x)) * c2s

        lam_t = pair_pack(cpow_const(float(T)))
        if d == 0:
            left = times(wn, bbr2, bbi2s)
            bs = times(cmul_const(wn, *pair_pack(cpow_const(float(T - 1)))), bbr2, bbi2s)
            right = times(wp, ccr2, cci2s)
            cc = times(cmul_const(wp, *pair_pack(cpow_const(1.0))), ccr2, cci2s)
        else:
            left = times(wp, bbr2, bbi2s)
            bs = left
            right = times(wn, ccr2, cci2s)
            cc = times(cmul_const(wn, *lam_t), ccr2, cci2s)
        bbar = jnp.where(lo, bbr2, bbi2s)
        return left * sgn_lr, right, bs, cc * sgn_lr, lam_t, bbar

    lf, rf, bsf, ccf, lamt_f, bbar_f = direction(0)
    lb, rb, bsb, ccb, lamt_b, bbar_b = direction(1)

    s_idx = lax.broadcasted_iota(jnp.int32, (W, W), 0) // SSM_CH
    t_idx = lax.broadcasted_iota(jnp.int32, (W, W), 1) // SSM_CH
    a_t = jnp.where(t_idx >= s_idx, _nt3(lf, rf), 0.0) + jnp.where(s_idx >= t_idx, _nt3(lb, rb), 0.0)
    w1 = jnp.concatenate([a_t, bsf, swap(bsf), bsb, swap(bsb)], axis=1).astype(BF16)
    r_scr[...] = _nn(ut_ref[0], w1)

    def chain(row0, nb, nchunks, col, lam_t, h, reverse):
        a1, a2s = lam_t
        hs = swap(h)
        order = range(nchunks - 1, -1, -1) if reverse else range(nchunks)
        hcol = slice(LANES, 2 * LANES) if reverse else slice(0, LANES)
        for c in order:
            rows = slice(row0 + c * nb, row0 + (c + 1) * nb)
            hin_scr[rows, hcol] = h
            s = r_scr[rows, col:col + LANES]
            ss = r_scr[rows, col + LANES:col + 2 * LANES]
            h, hs = h * a1 + hs * a2s + s, hs * a1 - h * a2s + ss
        return h

    zero = jnp.zeros((ctx_batch, LANES), F32)
    chain(0, ctx_batch, ctx_chunks, W, lamt_f, zero, False)
    chain(0, ctx_batch, ctx_chunks, W + 2 * LANES, lamt_b, zero, True)
    s_tiled = tile_time(srow)
    first_tok = jnp.where(s_tiled == 0.0, bbar_f, 0.0).astype(BF16)
    last_tok = jnp.where(s_tiled == float(T - 1), bbar_b, 0.0).astype(BF16)
    last_rows = (ctx_chunks - 1) * ctx_batch
    hf_ref[0, :, 0:LANES] = _nn(ut_ref[0, 0:ctx_batch, :], first_tok)
    hf_ref[0, :, LANES:2 * LANES] = _nn(ut_ref[0, last_rows:last_rows + ctx_batch, :], last_tok)
    lat0 = ctx_batch * ctx_chunks
    h0 = h0_ref[0]
    chain(lat0, lat_batch, lat_chunks, W, lamt_f, h0[:, 0:LANES], False)
    chain(lat0, lat_batch, lat_chunks, W + 2 * LANES, lamt_b, h0[:, LANES:2 * LANES], True)

    cc_cat = jnp.concatenate([ccf, ccb], axis=1).astype(BF16)
    yt_ref[0] = r_scr[:, 0:W] + _nt(hin_scr[...].astype(BF16), cc_cat)


def _ssm(ut, h0, p1, pb, ctx_batch, ctx_chunks, lat_batch, lat_chunks):
    g, n, w = ut.shape
    return pl.pallas_call(
        functools.partial(_ssm_kernel, ctx_batch=ctx_batch, ctx_chunks=ctx_chunks,
                          lat_batch=lat_batch, lat_chunks=lat_chunks),
        grid=(g,),
        in_specs=[
            pl.BlockSpec((1, n, w), lambda i: (i, 0, 0)),
            pl.BlockSpec((1, lat_batch, 2 * LANES), lambda i: (i, 0, 0)),
            pl.BlockSpec((1, 8, LANES), lambda i: (i, 0, 0)),
            pl.BlockSpec((1, 8, SSM_CH, LANES), lambda i: (i, 0, 0, 0)),
        ],
        out_specs=[
            pl.BlockSpec((1, n, w), lambda i: (i, 0, 0)),
            pl.BlockSpec((1, ctx_batch, 2 * LANES), lambda i: (i, 0, 0)),
        ],
        out_shape=[jax.ShapeDtypeStruct((g, n, w), F32),
                   jax.ShapeDtypeStruct((g, ctx_batch, 2 * LANES), F32)],
        scratch_shapes=[pltpu.VMEM((n, w + 4 * LANES), F32), pltpu.VMEM((n, 2 * LANES), F32)],
        compiler_params=pltpu.CompilerParams(dimension_semantics=("parallel",),
                                             vmem_limit_bytes=VMEM_LIMIT),
        name="s5_scan",
    )(ut, h0, p1, pb)


def _postmix_kernel(x_ref, oatt_ref, y_ref, u_ref, mod_ref, d_ref, wglu_ref, bglu_ref, wout_ref, g2_ref,
                    x1_ref, h2_ref):
    mod = mod_ref[0]
    gt1 = mod[2:3]
    sh2 = mod[3:4]
    sc2 = mod[4:5]
    y = y_ref[...] + d_ref[...] * u_ref[...].astype(F32)
    z = jax.nn.gelu(y)
    gate = _sigmoid(_nn(z.astype(BF16), wglu_ref[...]) + bglu_ref[...])
    o_ssm = (z * gate).astype(BF16)
    mix = _nn(oatt_ref[...], wout_ref[0:ATT_W, :]) + _nn(o_ssm, wout_ref[ATT_W:, :])
    x1 = x_ref[...] + gt1 * mix
    x1_ref[...] = x1
    ms = jnp.mean(x1 * x1, axis=-1, keepdims=True)
    h2 = x1 * lax.rsqrt(ms + EPS) * g2_ref[...]
    h2_ref[...] = (h2 * (1.0 + sc2) + sh2).astype(BF16)


def _postmix(x, oatt, y, u, mods_l, d, wglu, bglu, wout, g2, seq_len, mod_base):
    n = x.shape[0]
    tm = ROW_TILE
    per_seq = seq_len // tm
    if mod_base == 0:
        mod_map = lambda i: (0, 0, 0)
    else:
        mod_map = lambda i: (mod_base + i // per_seq, 0, 0)
    row = lambda i: (i, 0)
    const = lambda i: (0, 0)
    return pl.pallas_call(
        _postmix_kernel,
        grid=(n // tm,),
        in_specs=[
            pl.BlockSpec((tm, D_MODEL), row),
            pl.BlockSpec((tm, ATT_W), row),
            pl.BlockSpec((tm, SSM_W), row),
            pl.BlockSpec((tm, SSM_W), row),
            pl.BlockSpec((1, N_MOD, D_MODEL), mod_map),
            pl.BlockSpec((1, SSM_W), const),
            pl.BlockSpec((SSM_W, SSM_W), const),
            pl.BlockSpec((1, SSM_W), const),
            pl.BlockSpec((D_MODEL, D_MODEL), const),
            pl.BlockSpec((1, D_MODEL), const),
        ],
        out_specs=[pl.BlockSpec((tm, D_MODEL), row), pl.BlockSpec((tm, D_MODEL), row)],
        out_shape=[jax.ShapeDtypeStruct((n, D_MODEL), F32), jax.ShapeDtypeStruct((n, D_MODEL), BF16)],
        compiler_params=pltpu.CompilerParams(dimension_semantics=("parallel",),
                                             vmem_limit_bytes=VMEM_LIMIT),
        name="postmix",
    )(x, oatt, y, u, mods_l, d, wglu, bglu, wout, g2)


def _ffn_kernel(h2_ref, x1_ref, mod_ref, wv_ref, wg_ref, cwv_ref, cwg_ref, cbv_ref, cbg_ref, wd_ref,
                o_ref, acc_ref, *, seq_len):
    j = pl.program_id(1)
    rows = h2_ref.shape[0]
    h2 = h2_ref[...]
    t = lax.broadcasted_iota(jnp.int32, (rows, 1), 0) % seq_len
    has_prev = t != 0
    has_next = t != seq_len - 1

    def conv(up, cw, cb):
        prev = jnp.where(has_prev, pltpu.roll(up, 1, 0), 0.0)
        nxt = jnp.where(has_next, pltpu.roll(up, rows - 1, 0), 0.0)
        return prev * cw[0:1] + up * cw[1:2] + nxt * cw[2:3] + cb

    val = conv(_nn(h2, wv_ref[...]), cwv_ref[...], cbv_ref[...])
    gate = conv(_nn(h2, wg_ref[...]), cwg_ref[...], cbg_ref[...])
    act = (gate * _sigmoid(gate) * val).astype(BF16)
    part = _nn(act, wd_ref[...])

    @pl.when(j == 0)
    def _():
        acc_ref[...] = part

    @pl.when(j != 0)
    def _():
        acc_ref[...] += part

    @pl.when(j == pl.num_programs(1) - 1)
    def _():
        o_ref[...] = x1_ref[...] + mod_ref[0][5:6] * acc_ref[...]


def _ffn(h2, x1, mods_l, wup, cw, cb, wdown, seq_len, mod_base):
    n = h2.shape[0]
    rows = FFN_ROWS
    per_seq = seq_len // rows if seq_len >= rows else 1
    tf = FFN_SLAB
    nslab = D_FF // tf
    if mod_base == 0:
        mod_map = lambda i, j: (0, 0, 0)
    else:
        mod_map = lambda i, j: (mod_base + i // per_seq, 0, 0)
    return pl.pallas_call(
        functools.partial(_ffn_kernel, seq_len=seq_len),
        grid=(n // rows, nslab),
        in_specs=[
            pl.BlockSpec((rows, D_MODEL), lambda i, j: (i, 0)),
            pl.BlockSpec((rows, D_MODEL), lambda i, j: (i, 0)),
            pl.BlockSpec((1, N_MOD, D_MODEL), mod_map),
            pl.BlockSpec((D_MODEL, tf), lambda i, j: (0, j)),
            pl.BlockSpec((D_MODEL, tf), lambda i, j: (0, nslab + j)),
            pl.BlockSpec((3, tf), lambda i, j: (0, j)),
            pl.BlockSpec((3, tf), lambda i, j: (0, nslab + j)),
            pl.BlockSpec((1, tf), lambda i, j: (0, j)),
            pl.BlockSpec((1, tf), lambda i, j: (0, nslab + j)),
            pl.BlockSpec((tf, D_MODEL), lambda i, j: (j, 0)),
        ],
        out_specs=pl.BlockSpec((rows, D_MODEL), lambda i, j: (i, 0)),
        out_shape=jax.ShapeDtypeStruct((n, D_MODEL), F32),
        scratch_shapes=[pltpu.VMEM((rows, D_MODEL), F32)],
        compiler_params=pltpu.CompilerParams(dimension_semantics=("parallel", "arbitrary"),
                                             vmem_limit_bytes=VMEM_LIMIT),
        name="conv_ffn",
    )(h2, x1, mods_l, wup, wup, cw, cw, cb, cb, wdown)


def _rope_tables(n_tokens):
    pos = jnp.arange(n_tokens, dtype=F32)
    row = jnp.floor(pos / GRID_W)
    col = pos - row * GRID_W
    nf = ROPE_AXIS // 2
    inv_freq = 1.0 / (ROPE_THETA ** (jnp.arange(nf, dtype=F32) / nf))
    lane = jnp.arange(LANES)
    freq = inv_freq[lane % nf]
    is_col = (lane % D_SUB) >= ROPE_AXIS
    ang = jnp.where(is_col[None, :], col[:, None], row[:, None]) * freq[None, :]
    second = (lane % ROPE_AXIS) >= nf
    cos = jnp.cos(ang)
    sin = jnp.sin(ang)
    sina = jnp.where(second[None, :], 0.0, -sin)
    sinb = jnp.where(second[None, :], sin, 0.0)
    return cos, sina, sinb


def _segment_transpose(vs):
    seg = lax.broadcasted_iota(jnp.int32, (1, LANES), 1) // SSM_CH
    vs = list(vs)
    for d in (4, 2, 1):
        keep = (seg & d) == 0
        shift = d * SSM_CH
        for a in range(GROUPS_PER_BLOCK):
            if a & d:
                continue
            lo, hi = vs[a], vs[a + d]
            vs[a] = jnp.where(keep, lo, pltpu.roll(hi, shift, 1))
            vs[a + d] = jnp.where(keep, pltpu.roll(lo, LANES - shift, 1), hi)
    return vs


def _chunk_layout_passes(ctx_batch, ctx_len, lat_batch, lat_len):
    ctx_rows = ctx_batch * (ctx_len // CHUNK)
    return ((0, ctx_batch, ctx_len, 0, BF16_ROWS // ctx_batch if ctx_batch < BF16_ROWS else 1),
            (1, lat_batch, lat_len, ctx_rows, BF16_ROWS // lat_batch if lat_batch < BF16_ROWS else 1))


def _to_chunks_kernel(uc_ref, ul_ref, o_ref, *, passes):
    for idx, batch, seq_len, row0, cstep in passes:
        u_ref = (uc_ref, ul_ref)[idx]

        def body(k, carry, u_ref=u_ref, batch=batch, seq_len=seq_len, row0=row0, cstep=cstep):
            for j in range(CHUNK // GROUPS_PER_BLOCK):
                pieces = [[] for _ in range(GROUPS_PER_BLOCK)]
                for cc in range(cstep):
                    base = (k * cstep + cc) * CHUNK + j * GROUPS_PER_BLOCK
                    vs = [u_ref[pl.ds(base + s, batch, stride=seq_len), :] for s in range(GROUPS_PER_BLOCK)]
                    for g, w in enumerate(_segment_transpose(vs)):
                        pieces[g].append(w)
                start = pl.multiple_of(row0 + k * cstep * batch, BF16_ROWS)
                for g in range(GROUPS_PER_BLOCK):
                    blk = pieces[g][0] if cstep == 1 else jnp.concatenate(pieces[g], axis=0)
                    o_ref[g, pl.ds(start, cstep * batch), j * LANES:(j + 1) * LANES] = blk.astype(o_ref.dtype)
            return carry

        lax.fori_loop(0, seq_len // CHUNK // cstep, body, 0)


def _from_chunks_kernel(y_ref, oc_ref, ol_ref, *, passes):
    for idx, batch, seq_len, row0, _ in passes:
        o_ref = (oc_ref, ol_ref)[idx]

        def body(c, carry, o_ref=o_ref, batch=batch, seq_len=seq_len, row0=row0):
            start = pl.multiple_of(row0 + c * batch, 8)
            for j in range(CHUNK // GROUPS_PER_BLOCK):
                ws = [y_ref[g, pl.ds(start, batch), j * LANES:(j + 1) * LANES] for g in range(GROUPS_PER_BLOCK)]
                base = c * CHUNK + j * GROUPS_PER_BLOCK
                for s, v in enumerate(_segment_transpose(ws)):
                    o_ref[pl.ds(base + s, batch, stride=seq_len), :] = v
            return carry

        lax.fori_loop(0, seq_len // CHUNK, body, 0)


def _to_chunks(uc, ul, ctx_batch, ctx_len, lat_batch, lat_len):
    passes = _chunk_layout_passes(ctx_batch, ctx_len, lat_batch, lat_len)
    rows = ctx_batch * (ctx_len // CHUNK) + lat_batch * (lat_len // CHUNK)
    return pl.pallas_call(
        functools.partial(_to_chunks_kernel, passes=passes),
        grid=(SSM_GROUPS // GROUPS_PER_BLOCK,),
        in_specs=[pl.BlockSpec((uc.shape[0], LANES), lambda j: (0, j)),
                  pl.BlockSpec((ul.shape[0], LANES), lambda j: (0, j))],
        out_specs=pl.BlockSpec((GROUPS_PER_BLOCK, rows, SSM_CH * CHUNK), lambda j: (j, 0, 0)),
        out_shape=jax.ShapeDtypeStruct((SSM_GROUPS, rows, SSM_CH * CHUNK), BF16),
        compiler_params=pltpu.CompilerParams(dimension_semantics=("parallel",),
                                             vmem_limit_bytes=VMEM_LIMIT),
        name="to_chunks",
    )(uc, ul)


def _from_chunks(yt, ctx_batch, ctx_len, lat_batch, lat_len):
    passes = _chunk_layout_passes(ctx_batch, ctx_len, lat_batch, lat_len)
    rows = yt.shape[1]
    n_ctx = ctx_batch * ctx_len
    n_lat = lat_batch * lat_len
    return pl.pallas_call(
        functools.partial(_from_chunks_kernel, passes=passes),
        grid=(SSM_GROUPS // GROUPS_PER_BLOCK,),
        in_specs=[pl.BlockSpec((GROUPS_PER_BLOCK, rows, SSM_CH * CHUNK), lambda j: (j, 0, 0))],
        out_specs=[pl.BlockSpec((n_ctx, LANES), lambda j: (0, j)),
                   pl.BlockSpec((n_lat, LANES), lambda j: (0, j))],
        out_shape=[jax.ShapeDtypeStruct((n_ctx, SSM_W), F32), jax.ShapeDtypeStruct((n_lat, SSM_W), F32)],
        compiler_params=pltpu.CompilerParams(dimension_semantics=("parallel",),
                                             vmem_limit_bytes=VMEM_LIMIT),
        name="from_chunks",
    )(yt)


def _dup_lanes(a):
    return jnp.concatenate([a, a], axis=-1)


def kernel(x_prompt, x_sample, cache_k, cache_v, state_ssm, c, c_ctx, w_mod, b_mod, g_norm1, w_in, q_norm, k_norm, lambda_q1, lambda_k1, lambda_q2, lambda_k2, subln_g, ssm_lambda_re, ssm_lambda_im, ssm_log_step, ssm_b_re, ssm_b_im, ssm_c_re, ssm_c_im, ssm_d, w_glu, b_glu, w_out, g_norm2, w_up, conv_w, conv_b, w_down):
    nb_ctx, len_ctx, _ = x_prompt.shape
    nb_lat, len_lat, _ = x_sample.shape
    past = cache_k.shape[2]

    n_cond = 16
    cond = jnp.zeros((n_cond, D_MODEL), F32).at[0].set(c_ctx).at[1:1 + nb_lat].set(c)
    mods = _modulation(cond, w_mod, b_mod).reshape(DEPTH, n_cond, N_MOD, D_MODEL)

    gidx = jnp.arange(ATT_W) // D_SUB
    gmat = jnp.where(gidx[:, None] == gidx[None, :], 1.0 / D_SUB, 0.0).astype(BF16)
    rope_tabs = _rope_tables(len_lat)

    xp = x_prompt.reshape(nb_ctx * len_ctx, D_MODEL)
    xs = x_sample.reshape(nb_lat * len_lat, D_MODEL)
    ks, vs, ss = [], [], []
    for l in range(DEPTH):
        lam_init = 0.8 - 0.6 * math.exp(-0.3 * l)
        mods_l = mods[l]
        g1 = g_norm1[l].reshape(1, D_MODEL)
        g2 = g_norm2[l].reshape(1, D_MODEL)
        win = w_in[l].astype(BF16)
        wout = w_out[l].astype(BF16)
        wglu = w_glu[l].astype(BF16)
        wup = w_up[l].astype(BF16)
        wdown = w_down[l].astype(BF16)
        qg = jnp.tile(q_norm[l], ATT_W // D_SUB).reshape(1, ATT_W)
        kg = jnp.tile(k_norm[l], ATT_W // D_SUB).reshape(1, ATT_W)
        sg = jnp.tile(subln_g[l], N_HEADS).reshape(1, ATT_W)
        lamp = jnp.stack([lambda_q1[l], lambda_k1[l], lambda_q2[l], lambda_k2[l]])
        d = ssm_d[l].reshape(1, SSM_W)
        bglu = b_glu[l].reshape(1, SSM_W)
        cb = conv_b[l].reshape(1, 2 * D_FF)
        cw = conv_w[l]
        kc = cache_k[:, l].reshape(nb_lat, past, ATT_W).astype(BF16)
        vc = cache_v[:, l].reshape(nb_lat, past, ATT_W).astype(BF16)

        zeros = jnp.zeros((SSM_GROUPS, SSM_P), F32)
        step = jnp.broadcast_to(ssm_log_step[l][:, :, None], (2, SSM_GROUPS, SSM_P))
        p1 = _dup_lanes(jnp.stack([ssm_lambda_re[l, 0], ssm_lambda_im[l, 0], step[0],
                                   ssm_lambda_re[l, 1], ssm_lambda_im[l, 1], step[1], zeros, zeros], axis=1))
        bt = lambda a: a.transpose(0, 2, 1)
        pb = _dup_lanes(jnp.stack([bt(ssm_b_re[l, 0]), bt(ssm_b_im[l, 0]), ssm_c_re[l, 0], ssm_c_im[l, 0],
                                   bt(ssm_b_re[l, 1]), bt(ssm_b_im[l, 1]), ssm_c_re[l, 1], ssm_c_im[l, 1]], axis=1))
        h0 = state_ssm[:, l].transpose(2, 0, 1, 4, 3).reshape(SSM_GROUPS, nb_lat, 4 * SSM_P)

        qc, kc_new, vc_new, uc = _premix(xp, mods_l, g1, win, gmat, qg, kg, None, len_ctx, 0, F32)
        ql, kl, vl, ul = _premix(xs, mods_l, g1, win, gmat, qg, kg, rope_tabs, len_lat, 1, BF16)

        oc = _attention(qc, kc_new, vc_new, lamp, sg, None, nb_ctx, len_ctx, lam_init)
        ol = _attention(ql, kl, vl, lamp, sg, (kc, vc), nb_lat, len_lat, lam_init)

        ut = _to_chunks(uc, ul, nb_ctx, len_ctx, nb_lat, len_lat)
        yt, hfin = _ssm(ut, h0, p1, pb, nb_ctx, len_ctx // CHUNK, nb_lat, len_lat // CHUNK)
        yc, yl = _from_chunks(yt, nb_ctx, len_ctx, nb_lat, len_lat)

        x1c, h2c = _postmix(xp, oc, yc, uc, mods_l, d, wglu, bglu, wout, g2, len_ctx, 0)
        x1l, h2l = _postmix(xs, ol, yl, ul, mods_l, d, wglu, bglu, wout, g2, len_lat, 1)

        xp = _ffn(h2c, x1c, mods_l, wup, cw, cb, wdown, len_ctx, 0)
        xs = _ffn(h2l, x1l, mods_l, wup, cw, cb, wdown, len_lat, 1)

        ks.append(kc_new.reshape(nb_ctx, len_ctx, N_HEADS, 2, D_SUB))
        vs.append(vc_new.reshape(nb_ctx, len_ctx, N_HEADS, D_V))
        ss.append(hfin.reshape(SSM_GROUPS, nb_ctx, 2, 2, SSM_P).transpose(1, 2, 0, 4, 3))

    return (xp.reshape(nb_ctx, len_ctx, D_MODEL), xs.reshape(nb_lat, len_lat, D_MODEL),
            jnp.stack(ks, axis=1), jnp.stack(vs, axis=1), jnp.stack(ss, axis=1))
```

```python
import functools
import math

import jax
import jax.numpy as jnp
from jax import lax
from jax.experimental import pallas as pl
from jax.experimental.pallas import tpu as pltpu

F32 = jnp.float32
BF16 = jnp.bfloat16

D_MODEL = 1024
DEPTH = 2
GRID_W = 64
ATT_W = 512
SSM_W = 512
N_HEADS = 4
D_V = 128
D_SUB = 64
ROPE_AXIS = 32
ROPE_THETA = 10000.0
SSM_CH = 16
SSM_GROUPS = 32
SSM_P = 64
D_FF = 2048
N_MOD = 6
EPS = 1e-6

LANES = 128
BF16_ROWS = 16
GROUPS_PER_BLOCK = LANES // SSM_CH
CHUNK = 32
ROW_TILE = 256
MIX_TILE = 512
MIX_SUB = 256
FFN_ROWS = 1024
FFN_SUB = 256
FFN_SLAB = 512
MOD_COLS = 1536
VMEM_LIMIT = 56 * 1024 * 1024


def _nt(a, b):
    return lax.dot_general(a, b, (((1,), (1,)), ((), ())), preferred_element_type=F32)


def _nn(a, b):
    return jnp.dot(a, b, preferred_element_type=F32)


def _split_bf16(a):
    hi = a.astype(BF16)
    lo = (a - hi.astype(F32)).astype(BF16)
    return hi, lo


def _nt3(a, b):
    ah, al = _split_bf16(a)
    bh, bl = _split_bf16(b)
    return _nt(ah, bh) + _nt(ah, bl) + _nt(al, bh)


def _sigmoid(x):
    return 1.0 / (1.0 + jnp.exp(-x))


def _mod_kernel(c_ref, w_ref, b_ref, o_ref):
    c = c_ref[...]
    a = (c * _sigmoid(c)).astype(BF16)
    o_ref[0] = _nn(a, w_ref[0].astype(BF16)) + b_ref[0]


def _modulation(cond, w_mod, b_mod):
    nb = cond.shape[0]
    ncol = N_MOD * D_MODEL
    return pl.pallas_call(
        _mod_kernel,
        grid=(DEPTH, ncol // MOD_COLS),
        in_specs=[
            pl.BlockSpec((nb, D_MODEL), lambda l, j: (0, 0)),
            pl.BlockSpec((1, D_MODEL, MOD_COLS), lambda l, j: (l, 0, j)),
            pl.BlockSpec((1, 1, MOD_COLS), lambda l, j: (l, 0, j)),
        ],
        out_specs=pl.BlockSpec((1, nb, MOD_COLS), lambda l, j: (l, 0, j)),
        out_shape=jax.ShapeDtypeStruct((DEPTH, nb, ncol), F32),
        compiler_params=pltpu.CompilerParams(vmem_limit_bytes=VMEM_LIMIT),
        name="adaln_mod",
    )(cond, w_mod, b_mod.reshape(DEPTH, 1, ncol))


def _premix_kernel(x_ref, mod_ref, g1_ref, win_ref, gmat_ref, qg_ref, kg_ref, *rest, rope):
    if rope:
        cos_ref, sina_ref, sinb_ref, q_ref, k_ref, v_ref, u_ref = rest
    else:
        q_ref, k_ref, v_ref, u_ref = rest
    mod = mod_ref[0]
    sh1 = mod[0:1]
    sc1 = mod[1:2]

    def head_norm(t, g):
        msq = _nn((t * t).astype(BF16), gmat_ref[...])
        return t * lax.rsqrt(msq + EPS) * g

    def rotate(t, rs):
        outs = []
        for j in range(ATT_W // LANES):
            s = t[:, j * LANES:(j + 1) * LANES]
            outs.append(s * cos_ref[rs, :] + pltpu.roll(s, LANES - 16, 1) * sina_ref[rs, :]
                        + pltpu.roll(s, 16, 1) * sinb_ref[rs, :])
        return jnp.concatenate(outs, axis=1)

    for sb in range(x_ref.shape[0] // MIX_SUB):
        rs = slice(sb * MIX_SUB, (sb + 1) * MIX_SUB)
        x = x_ref[rs, :]
        ms = jnp.mean(x * x, axis=-1, keepdims=True)
        h = x * lax.rsqrt(ms + EPS) * g1_ref[...]
        h = h * (1.0 + sc1) + sh1
        qkvu = _nn(h.astype(BF16), win_ref[...])
        q = head_norm(qkvu[:, 0:ATT_W], qg_ref[...])
        k = head_norm(qkvu[:, ATT_W:2 * ATT_W], kg_ref[...])
        if rope:
            q = rotate(q, rs)
            k = rotate(k, rs)
        q_ref[rs, :] = (q * (D_SUB ** -0.5)).astype(q_ref.dtype)
        k_ref[rs, :] = k.astype(k_ref.dtype)
        v_ref[rs, :] = qkvu[:, 2 * ATT_W:3 * ATT_W].astype(v_ref.dtype)
        u_ref[rs, :] = qkvu[:, 3 * ATT_W:].astype(u_ref.dtype)


def _premix(x, mods_l, g1, win, gmat, qg, kg, rope_tabs, seq_len, mod_base, kv_dtype):
    n = x.shape[0]
    tm = MIX_TILE
    per_seq = seq_len // tm
    rope = rope_tabs is not None
    if mod_base == 0:
        mod_map = lambda i: (0, 0, 0)
    else:
        mod_map = lambda i: (mod_base + i // per_seq, 0, 0)
    row = lambda i: (i, 0)
    const = lambda i: (0, 0)
    in_specs = [
        pl.BlockSpec((tm, D_MODEL), row),
        pl.BlockSpec((1, N_MOD, D_MODEL), mod_map),
        pl.BlockSpec((1, D_MODEL), const),
        pl.BlockSpec((D_MODEL, 4 * ATT_W), const),
        pl.BlockSpec((ATT_W, ATT_W), const),
        pl.BlockSpec((1, ATT_W), const),
        pl.BlockSpec((1, ATT_W), const),
    ]
    args = [x, mods_l, g1, win, gmat, qg, kg]
    if rope:
        tab = lambda i: (i % per_seq, 0)
        in_specs += [pl.BlockSpec((tm, LANES), tab)] * 3
        args += list(rope_tabs)
    out = pl.BlockSpec((tm, ATT_W), row)
    return pl.pallas_call(
        functools.partial(_premix_kernel, rope=rope),
        grid=(n // tm,),
        in_specs=in_specs,
        out_specs=[out, out, out, out],
        out_shape=[jax.ShapeDtypeStruct((n, ATT_W), BF16),
                   jax.ShapeDtypeStruct((n, ATT_W), kv_dtype),
                   jax.ShapeDtypeStruct((n, ATT_W), kv_dtype),
                   jax.ShapeDtypeStruct((n, SSM_W), F32)],
        compiler_params=pltpu.CompilerParams(dimension_semantics=("parallel",),
                                             vmem_limit_bytes=VMEM_LIMIT),
        name="premix_rope" if rope else "premix",
    )(*args)


def _attn_kernel(q_ref, k_ref, v_ref, lamp_ref, sg_ref, *rest, lam_init, has_cache):
    if has_cache:
        kc_ref, vc_ref, o_ref = rest
    else:
        (o_ref,) = rest
    tq = q_ref.shape[0]
    lp = lamp_ref[...]
    l1 = jnp.sum(lp[0:1] * lp[1:2], axis=-1, keepdims=True)
    l2 = jnp.sum(lp[2:3] * lp[3:4], axis=-1, keepdims=True)
    lam = jnp.exp(l1) - jnp.exp(l2) + lam_init
    first = lax.broadcasted_iota(jnp.int32, (1, LANES), 1) < D_SUB
    for h in range(N_HEADS):
        sl = slice(h * D_V, (h + 1) * D_V)
        qh = q_ref[:, sl]
        zero = jnp.zeros_like(qh)
        qq = jnp.concatenate([jnp.where(first, qh, zero), jnp.where(first, zero, qh)], axis=0)
        kn = k_ref[:, sl].astype(BF16)
        vn = v_ref[:, sl].astype(BF16)
        if has_cache:
            ncache = kc_ref.shape[1]
            sc = jnp.concatenate([_nt(qq, kc_ref[0, :, sl]), _nt(qq, kn)], axis=1)
        else:
            sc = _nt(qq, kn)
        m = jnp.max(sc, axis=-1, keepdims=True)
        e = jnp.exp(sc - m)
        ssum = jnp.sum(e, axis=-1, keepdims=True)
        eb = e.astype(BF16)
        if has_cache:
            pv = _nn(eb[:, :ncache], vc_ref[0, :, sl]) + _nn(eb[:, ncache:], vn)
        else:
            pv = _nn(eb, vn)
        on = pv * (1.0 / ssum)
        o = on[:tq] - lam * on[tq:]
        ms = jnp.mean(o * o, axis=-1, keepdims=True)
        o_ref[:, sl] = (o * lax.rsqrt(ms + EPS) * sg_ref[:, sl] * (1.0 - lam_init)).astype(o_ref.dtype)


def _attention(q, k, v, lamp, sg, cache, batch, seq_len, lam_init):
    n = q.shape[0]
    tq = ROW_TILE
    nq = seq_len // tq
    has_cache = cache is not None
    in_specs = [
        pl.BlockSpec((tq, ATT_W), lambda b, i: (b * nq + i, 0)),
        pl.BlockSpec((seq_len, ATT_W), lambda b, i: (b, 0)),
        pl.BlockSpec((seq_len, ATT_W), lambda b, i: (b, 0)),
        pl.BlockSpec((4, D_SUB), lambda b, i: (0, 0)),
        pl.BlockSpec((1, ATT_W), lambda b, i: (0, 0)),
    ]
    args = [q, k, v, lamp, sg]
    if has_cache:
        past = cache[0].shape[1]
        in_specs += [pl.BlockSpec((1, past, ATT_W), lambda b, i: (b, 0, 0))] * 2
        args += list(cache)
    return pl.pallas_call(
        functools.partial(_attn_kernel, lam_init=lam_init, has_cache=has_cache),
        grid=(batch, nq),
        in_specs=in_specs,
        out_specs=pl.BlockSpec((tq, ATT_W), lambda b, i: (b * nq + i, 0)),
        out_shape=jax.ShapeDtypeStruct((n, ATT_W), BF16),
        compiler_params=pltpu.CompilerParams(dimension_semantics=("parallel", "parallel"),
                                             vmem_limit_bytes=VMEM_LIMIT),
        name="attn_cache" if has_cache else "attn",
    )(*args)


def _ssm_kernel(ut_ref, h0_ref, p1_ref, pb_ref, yt_ref, hf_ref, r_scr, hin_scr, *,
                ctx_batch, ctx_chunks, lat_batch, lat_chunks):
    T = CHUNK
    W = SSM_CH * T
    lane = lax.broadcasted_iota(jnp.int32, (1, LANES), 1)
    lo = lane < SSM_P
    sgn_lr = jnp.where(lo, 1.0, -1.0).astype(F32)
    sgn_rl = -sgn_lr
    srow = lax.broadcasted_iota(jnp.int32, (T, LANES), 0).astype(F32)

    def swap(a):
        return pltpu.roll(a, SSM_P, 1)

    def tile_time(a):
        return jnp.concatenate([jnp.broadcast_to(a[i:i + 1, :], (SSM_CH, LANES)) for i in range(T)], axis=0)

    def rep_chan(a):
        return jnp.concatenate([a] * T, axis=0)

    def pair_pack(y):
        ys = swap(y)
        return jnp.where(lo, y, ys), jnp.where(lo, ys, y) * sgn_rl

    def cmul_const(x, yr2, yi2s):
        return x * yr2 + swap(x) * yi2s

    p1 = p1_ref[0]
    pb = pb_ref[0]

    def direction(d):
        lre2 = p1[3 * d + 0:3 * d + 1]
        lim2 = p1[3 * d + 1:3 * d + 2]
        ls2 = p1[3 * d + 2:3 * d + 3]
        bre2, bim2, cre2, cim2 = pb[4 * d + 0], pb[4 * d + 1], pb[4 * d + 2], pb[4 * d + 3]
        delta = jnp.exp(ls2)
        xr = lre2 * delta
        th = lim2 * delta

        def cpow_const(k):
            return jnp.exp(k * xr) * jnp.where(lo, jnp.cos(k * th), jnp.sin(k * th))

        mag1 = jnp.exp(xr)
        lbr2 = mag1 * jnp.cos(th)
        lbi2 = mag1 * jnp.sin(th)
        den = lre2 * lre2 + lim2 * lim2
        nr = lbr2 - 1.0
        cr2 = (nr * lre2 + lbi2 * lim2) / den
        ci2 = (lbi2 * lre2 - nr * lim2) / den
        bbr2 = rep_chan(cr2 * bre2 - ci2 * bim2)
        bbi2s = rep_chan((cr2 * bim2 + ci2 * bre2) * sgn_rl)
        ccr2 = rep_chan(cre2)
        cci2s = rep_chan(cim2 * sgn_rl)

        ang = srow * th
        cs = jnp.where(lo, jnp.cos(ang), jnp.sin(ang))
        wp = jnp.exp(srow * xr) * cs
        wn = jnp.exp(-(srow * xr)) * cs * sgn_lr

        def times(x, c1, c2s):
            return tile_time(x) * c1 + tile_time(swap(x)) * c2s

        lam_t = pair_pack(cpow_const(float(T)))
        if d == 0:
            left = times(wn, bbr2, bbi2s)
            bs = times(cmul_const(wn, *pair_pack(cpow_const(float(T - 1)))), bbr2, bbi2s)
            right = times(wp, ccr2, cci2s)
            cc = times(cmul_const(wp, *pair_pack(cpow_const(1.0))), ccr2, cci2s)
        else:
            left = times(wp, bbr2, bbi2s)
            bs = left
            right = times(wn, ccr2, cci2s)
            cc = times(cmul_const(wn, *lam_t), ccr2, cci2s)
        bbar = jnp.where(lo, bbr2, bbi2s)
        return left * sgn_lr, right, bs, cc * sgn_lr, lam_t, bbar

    lf, rf, bsf, ccf, lamt_f, bbar_f = direction(0)
    lb, rb, bsb, ccb, lamt_b, bbar_b = direction(1)

    s_idx = lax.broadcasted_iota(jnp.int32, (W, W), 0) // SSM_CH
    t_idx = lax.broadcasted_iota(jnp.int32, (W, W), 1) // SSM_CH
    a_t = jnp.where(t_idx >= s_idx, _nt3(lf, rf), 0.0) + jnp.where(s_idx >= t_idx, _nt3(lb, rb), 0.0)
    w1 = jnp.concatenate([a_t, bsf, swap(bsf), bsb, swap(bsb)], axis=1).astype(BF16)
    r_scr[...] = _nn(ut_ref[0], w1)

    def chain(row0, nb, nchunks, col, lam_t, h, reverse):
        a1, a2s = lam_t
        hs = swap(h)
        order = range(nchunks - 1, -1, -1) if reverse else range(nchunks)
        hcol = slice(LANES, 2 * LANES) if reverse else slice(0, LANES)
        for c in order:
            rows = slice(row0 + c * nb, row0 + (c + 1) * nb)
            hin_scr[rows, hcol] = h
            s = r_scr[rows, col:col + LANES]
            ss = r_scr[rows, col + LANES:col + 2 * LANES]
            h, hs = h * a1 + hs * a2s + s, hs * a1 - h * a2s + ss
        return h

    zero = jnp.zeros((ctx_batch, LANES), F32)
    chain(0, ctx_batch, ctx_chunks, W, lamt_f, zero, False)
    chain(0, ctx_batch, ctx_chunks, W + 2 * LANES, lamt_b, zero, True)
    s_tiled = tile_time(srow)
    first_tok = jnp.where(s_tiled == 0.0, bbar_f, 0.0).astype(BF16)
    last_tok = jnp.where(s_tiled == float(T - 1), bbar_b, 0.0).astype(BF16)
    last_rows = (ctx_chunks - 1) * ctx_batch
    hf_ref[0, :, 0:LANES] = _nn(ut_ref[0, 0:ctx_batch, :], first_tok)
    hf_ref[0, :, LANES:2 * LANES] = _nn(ut_ref[0, last_rows:last_rows + ctx_batch, :], last_tok)
    lat0 = ctx_batch * ctx_chunks
    h0 = h0_ref[0]
    chain(lat0, lat_batch, lat_chunks, W, lamt_f, h0[:, 0:LANES], False)
    chain(lat0, lat_batch, lat_chunks, W + 2 * LANES, lamt_b, h0[:, LANES:2 * LANES], True)

    cc_cat = jnp.concatenate([ccf, ccb], axis=1).astype(BF16)
    yt_ref[0] = r_scr[:, 0:W] + _nt(hin_scr[...].astype(BF16), cc_cat)


def _ssm(ut, h0, p1, pb, ctx_batch, ctx_chunks, lat_batch, lat_chunks):
    g, n, w = ut.shape
    return pl.pallas_call(
        functools.partial(_ssm_kernel, ctx_batch=ctx_batch, ctx_chunks=ctx_chunks,
                          lat_batch=lat_batch, lat_chunks=lat_chunks),
        grid=(g,),
        in_specs=[
            pl.BlockSpec((1, n, w), lambda i: (i, 0, 0)),
            pl.BlockSpec((1, lat_batch, 2 * LANES), lambda i: (i, 0, 0)),
            pl.BlockSpec((1, 8, LANES), lambda i: (i, 0, 0)),
            pl.BlockSpec((1, 8, SSM_CH, LANES), lambda i: (i, 0, 0, 0)),
        ],
        out_specs=[
            pl.BlockSpec((1, n, w), lambda i: (i, 0, 0)),
            pl.BlockSpec((1, ctx_batch, 2 * LANES), lambda i: (i, 0, 0)),
        ],
        out_shape=[jax.ShapeDtypeStruct((g, n, w), F32),
                   jax.ShapeDtypeStruct((g, ctx_batch, 2 * LANES), F32)],
        scratch_shapes=[pltpu.VMEM((n, w + 4 * LANES), F32), pltpu.VMEM((n, 2 * LANES), F32)],
        compiler_params=pltpu.CompilerParams(dimension_semantics=("parallel",),
                                             vmem_limit_bytes=VMEM_LIMIT),
        name="s5_scan",
    )(ut, h0, p1, pb)


def _postmix_kernel(x_ref, oatt_ref, y_ref, u_ref, mod_ref, d_ref, wglu_ref, bglu_ref, wout_ref, g2_ref,
                    x1_ref, h2_ref):
    mod = mod_ref[0]
    gt1 = mod[2:3]
    sh2 = mod[3:4]
    sc2 = mod[4:5]
    for sb in range(x_ref.shape[0] // MIX_SUB):
        rs = slice(sb * MIX_SUB, (sb + 1) * MIX_SUB)
        y = y_ref[rs, :] + d_ref[...] * u_ref[rs, :]
        z = jax.nn.gelu(y)
        gate = _sigmoid(_nn(z.astype(BF16), wglu_ref[...]) + bglu_ref[...])
        o_ssm = (z * gate).astype(BF16)
        mix = _nn(oatt_ref[rs, :], wout_ref[0:ATT_W, :]) + _nn(o_ssm, wout_ref[ATT_W:, :])
        x1 = x_ref[rs, :] + gt1 * mix
        x1_ref[rs, :] = x1
        ms = jnp.mean(x1 * x1, axis=-1, keepdims=True)
        h2 = x1 * lax.rsqrt(ms + EPS) * g2_ref[...]
        h2_ref[rs, :] = (h2 * (1.0 + sc2) + sh2).astype(BF16)


def _postmix(x, oatt, y, u, mods_l, d, wglu, bglu, wout, g2, seq_len, mod_base):
    n = x.shape[0]
    tm = MIX_TILE
    per_seq = seq_len // tm
    if mod_base == 0:
        mod_map = lambda i: (0, 0, 0)
    else:
        mod_map = lambda i: (mod_base + i // per_seq, 0, 0)
    row = lambda i: (i, 0)
    const = lambda i: (0, 0)
    return pl.pallas_call(
        _postmix_kernel,
        grid=(n // tm,),
        in_specs=[
            pl.BlockSpec((tm, D_MODEL), row),
            pl.BlockSpec((tm, ATT_W), row),
            pl.BlockSpec((tm, SSM_W), row),
            pl.BlockSpec((tm, SSM_W), row),
            pl.BlockSpec((1, N_MOD, D_MODEL), mod_map),
            pl.BlockSpec((1, SSM_W), const),
            pl.BlockSpec((SSM_W, SSM_W), const),
            pl.BlockSpec((1, SSM_W), const),
            pl.BlockSpec((D_MODEL, D_MODEL), const),
            pl.BlockSpec((1, D_MODEL), const),
        ],
        out_specs=[pl.BlockSpec((tm, D_MODEL), row), pl.BlockSpec((tm, D_MODEL), row)],
        out_shape=[jax.ShapeDtypeStruct((n, D_MODEL), F32), jax.ShapeDtypeStruct((n, D_MODEL), BF16)],
        compiler_params=pltpu.CompilerParams(dimension_semantics=("parallel",),
                                             vmem_limit_bytes=VMEM_LIMIT),
        name="postmix",
    )(x, oatt, y, u, mods_l, d, wglu, bglu, wout, g2)


def _ffn_kernel(h2_ref, x1_ref, mod_ref, wup_ref, cw_ref, cb_ref, wd_ref, o_ref, *, seq_len):
    rows = h2_ref.shape[0]
    sub = FFN_SUB
    halo = BF16_ROWS
    tf = FFN_SLAB
    gt2 = mod_ref[0][5:6]
    row8 = lax.broadcasted_iota(jnp.int32, (8, 1), 0)

    def conv(up, lead, has_prev, has_next, cw, cb):
        main = up[lead:lead + sub]
        prev = pltpu.roll(main, 1, 0)
        nxt = pltpu.roll(main, sub - 1, 0)
        before = up[lead - 1:lead] if has_prev else jnp.zeros((1, tf), F32)
        after = up[lead + sub:lead + sub + 1] if has_next else jnp.zeros((1, tf), F32)
        prev = jnp.concatenate([jnp.where(row8 == 0, before, prev[0:8]), prev[8:]], axis=0)
        nxt = jnp.concatenate([nxt[:sub - 8], jnp.where(row8 == 7, after, nxt[sub - 8:])], axis=0)
        return prev * cw[0:1] + main * cw[1:2] + nxt * cw[2:3] + cb

    for k in range(rows // sub):
        r0 = k * sub
        has_prev = r0 % seq_len != 0
        has_next = (r0 + sub) % seq_len != 0
        lead = halo if has_prev else 0
        h = h2_ref[r0 - lead:r0 + sub + (halo if has_next else 0), :]
        acts = []
        for j in range(D_FF // tf):
            cv = slice(j * tf, (j + 1) * tf)
            cg = slice(D_FF + j * tf, D_FF + (j + 1) * tf)
            val = conv(_nn(h, wup_ref[:, cv]), lead, has_prev, has_next, cw_ref[:, cv], cb_ref[:, cv])
            gate = conv(_nn(h, wup_ref[:, cg]), lead, has_prev, has_next, cw_ref[:, cg], cb_ref[:, cg])
            acts.append((gate * _sigmoid(gate) * val).astype(BF16))
        act = jnp.concatenate(acts, axis=1)
        o_ref[r0:r0 + sub, :] = x1_ref[r0:r0 + sub, :] + gt2 * _nn(act, wd_ref[...])


def _ffn(h2, x1, mods_l, wup, cw, cb, wdown, seq_len, mod_base):
    n = h2.shape[0]
    rows = FFN_ROWS
    per_seq = seq_len // rows if seq_len >= rows else 1
    if mod_base == 0:
        mod_map = lambda i: (0, 0, 0)
    else:
        mod_map = lambda i: (mod_base + i // per_seq, 0, 0)
    const = lambda i: (0, 0)
    resident = dict(pipeline_mode=pl.Buffered(1))
    return pl.pallas_call(
        functools.partial(_ffn_kernel, seq_len=seq_len),
        grid=(n // rows,),
        in_specs=[
            pl.BlockSpec((rows, D_MODEL), lambda i: (i, 0)),
            pl.BlockSpec((rows, D_MODEL), lambda i: (i, 0)),
            pl.BlockSpec((1, N_MOD, D_MODEL), mod_map),
            pl.BlockSpec((D_MODEL, 2 * D_FF), const, **resident),
            pl.BlockSpec((3, 2 * D_FF), const),
            pl.BlockSpec((1, 2 * D_FF), const),
            pl.BlockSpec((D_FF, D_MODEL), const, **resident),
        ],
        out_specs=pl.BlockSpec((rows, D_MODEL), lambda i: (i, 0)),
        out_shape=jax.ShapeDtypeStruct((n, D_MODEL), F32),
        compiler_params=pltpu.CompilerParams(dimension_semantics=("parallel",),
                                             vmem_limit_bytes=VMEM_LIMIT),
        name="conv_ffn",
    )(h2, x1, mods_l, wup, cw, cb, wdown)


def _rope_tables(n_tokens):
    pos = jnp.arange(n_tokens, dtype=F32)
    row = jnp.floor(pos / GRID_W)
    col = pos - row * GRID_W
    nf = ROPE_AXIS // 2
    inv_freq = 1.0 / (ROPE_THETA ** (jnp.arange(nf, dtype=F32) / nf))
    lane = jnp.arange(LANES)
    freq = inv_freq[lane % nf]
    is_col = (lane % D_SUB) >= ROPE_AXIS
    ang = jnp.where(is_col[None, :], col[:, None], row[:, None]) * freq[None, :]
    second = (lane % ROPE_AXIS) >= nf
    cos = jnp.cos(ang)
    sin = jnp.sin(ang)
    sina = jnp.where(second[None, :], 0.0, -sin)
    sinb = jnp.where(second[None, :], sin, 0.0)
    return cos, sina, sinb


def _segment_transpose(vs):
    seg = lax.broadcasted_iota(jnp.int32, (1, LANES), 1) // SSM_CH
    vs = list(vs)
    for d in (4, 2, 1):
        keep = (seg & d) == 0
        shift = d * SSM_CH
        for a in range(GROUPS_PER_BLOCK):
            if a & d:
                continue
            lo, hi = vs[a], vs[a + d]
            vs[a] = jnp.where(keep, lo, pltpu.roll(hi, shift, 1))
            vs[a + d] = jnp.where(keep, pltpu.roll(lo, LANES - shift, 1), hi)
    return vs


def _chunk_layout_passes(ctx_batch, ctx_len, lat_batch, lat_len):
    ctx_rows = ctx_batch * (ctx_len // CHUNK)
    return ((0, ctx_batch, ctx_len, 0, BF16_ROWS // ctx_batch if ctx_batch < BF16_ROWS else 1),
            (1, lat_batch, lat_len, ctx_rows, BF16_ROWS // lat_batch if lat_batch < BF16_ROWS else 1))


def _to_chunks_kernel(uc_ref, ul_ref, o_ref, *, passes):
    for idx, batch, seq_len, row0, cstep in passes:
        u_ref = (uc_ref, ul_ref)[idx]

        def body(k, carry, u_ref=u_ref, batch=batch, seq_len=seq_len, row0=row0, cstep=cstep):
            for j in range(CHUNK // GROUPS_PER_BLOCK):
                pieces = [[] for _ in range(GROUPS_PER_BLOCK)]
                for cc in range(cstep):
                    base = (k * cstep + cc) * CHUNK + j * GROUPS_PER_BLOCK
                    vs = [u_ref[pl.ds(base + s, batch, stride=seq_len), :] for s in range(GROUPS_PER_BLOCK)]
                    for g, w in enumerate(_segment_transpose(vs)):
                        pieces[g].append(w)
                start = pl.multiple_of(row0 + k * cstep * batch, BF16_ROWS)
                for g in range(GROUPS_PER_BLOCK):
                    blk = pieces[g][0] if cstep == 1 else jnp.concatenate(pieces[g], axis=0)
                    o_ref[g, pl.ds(start, cstep * batch), j * LANES:(j + 1) * LANES] = blk.astype(o_ref.dtype)
            return carry

        lax.fori_loop(0, seq_len // CHUNK // cstep, body, 0)


def _from_chunks_kernel(y_ref, oc_ref, ol_ref, *, passes):
    for idx, batch, seq_len, row0, _ in passes:
        o_ref = (oc_ref, ol_ref)[idx]

        def body(c, carry, o_ref=o_ref, batch=batch, seq_len=seq_len, row0=row0):
            start = pl.multiple_of(row0 + c * batch, 8)
            for j in range(CHUNK // GROUPS_PER_BLOCK):
                ws = [y_ref[g, pl.ds(start, batch), j * LANES:(j + 1) * LANES] for g in range(GROUPS_PER_BLOCK)]
                base = c * CHUNK + j * GROUPS_PER_BLOCK
                for s, v in enumerate(_segment_transpose(ws)):
                    o_ref[pl.ds(base + s, batch, stride=seq_len), :] = v
            return carry

        lax.fori_loop(0, seq_len // CHUNK, body, 0)


def _to_chunks(uc, ul, ctx_batch, ctx_len, lat_batch, lat_len):
    passes = _chunk_layout_passes(ctx_batch, ctx_len, lat_batch, lat_len)
    rows = ctx_batch * (ctx_len // CHUNK) + lat_batch * (lat_len // CHUNK)
    return pl.pallas_call(
        functools.partial(_to_chunks_kernel, passes=passes),
        grid=(SSM_GROUPS // GROUPS_PER_BLOCK,),
        in_specs=[pl.BlockSpec((uc.shape[0], LANES), lambda j: (0, j)),
                  pl.BlockSpec((ul.shape[0], LANES), lambda j: (0, j))],
        out_specs=pl.BlockSpec((GROUPS_PER_BLOCK, rows, SSM_CH * CHUNK), lambda j: (j, 0, 0)),
        out_shape=jax.ShapeDtypeStruct((SSM_GROUPS, rows, SSM_CH * CHUNK), BF16),
        compiler_params=pltpu.CompilerParams(dimension_semantics=("parallel",),
                                             vmem_limit_bytes=VMEM_LIMIT),
        name="to_chunks",
    )(uc, ul)


def _from_chunks(yt, ctx_batch, ctx_len, lat_batch, lat_len):
    passes = _chunk_layout_passes(ctx_batch, ctx_len, lat_batch, lat_len)
    rows = yt.shape[1]
    n_ctx = ctx_batch * ctx_len
    n_lat = lat_batch * lat_len
    return pl.pallas_call(
        functools.partial(_from_chunks_kernel, passes=passes),
        grid=(SSM_GROUPS // GROUPS_PER_BLOCK,),
        in_specs=[pl.BlockSpec((GROUPS_PER_BLOCK, rows, SSM_CH * CHUNK), lambda j: (j, 0, 0))],
        out_specs=[pl.BlockSpec((n_ctx, LANES), lambda j: (0, j)),
                   pl.BlockSpec((n_lat, LANES), lambda j: (0, j))],
        out_shape=[jax.ShapeDtypeStruct((n_ctx, SSM_W), F32), jax.ShapeDtypeStruct((n_lat, SSM_W), F32)],
        compiler_params=pltpu.CompilerParams(dimension_semantics=("parallel",),
                                             vmem_limit_bytes=VMEM_LIMIT),
        name="from_chunks",
    )(yt)


def _dup_lanes(a):
    return jnp.concatenate([a, a], axis=-1)


def kernel(x_prompt, x_sample, cache_k, cache_v, state_ssm, c, c_ctx, w_mod, b_mod, g_norm1, w_in, q_norm, k_norm, lambda_q1, lambda_k1, lambda_q2, lambda_k2, subln_g, ssm_lambda_re, ssm_lambda_im, ssm_log_step, ssm_b_re, ssm_b_im, ssm_c_re, ssm_c_im, ssm_d, w_glu, b_glu, w_out, g_norm2, w_up, conv_w, conv_b, w_down):
    nb_ctx, len_ctx, _ = x_prompt.shape
    nb_lat, len_lat, _ = x_sample.shape
    past = cache_k.shape[2]

    n_cond = 16
    cond = jnp.zeros((n_cond, D_MODEL), F32).at[0].set(c_ctx).at[1:1 + nb_lat].set(c)
    mods = _modulation(cond, w_mod, b_mod).reshape(DEPTH, n_cond, N_MOD, D_MODEL)

    gidx = jnp.arange(ATT_W) // D_SUB
    gmat = jnp.where(gidx[:, None] == gidx[None, :], 1.0 / D_SUB, 0.0).astype(BF16)
    rope_tabs = _rope_tables(len_lat)

    xp = x_prompt.reshape(nb_ctx * len_ctx, D_MODEL)
    xs = x_sample.reshape(nb_lat * len_lat, D_MODEL)
    ks, vs, ss = [], [], []
    for l in range(DEPTH):
        lam_init = 0.8 - 0.6 * math.exp(-0.3 * l)
        mods_l = mods[l]
        g1 = g_norm1[l].reshape(1, D_MODEL)
        g2 = g_norm2[l].reshape(1, D_MODEL)
        win = w_in[l].astype(BF16)
        wout = w_out[l].astype(BF16)
        wglu = w_glu[l].astype(BF16)
        wup = w_up[l].astype(BF16)
        wdown = w_down[l].astype(BF16)
        qg = jnp.tile(q_norm[l], ATT_W // D_SUB).reshape(1, ATT_W)
        kg = jnp.tile(k_norm[l], ATT_W // D_SUB).reshape(1, ATT_W)
        sg = jnp.tile(subln_g[l], N_HEADS).reshape(1, ATT_W)
        lamp = jnp.stack([lambda_q1[l], lambda_k1[l], lambda_q2[l], lambda_k2[l]])
        d = ssm_d[l].reshape(1, SSM_W)
        bglu = b_glu[l].reshape(1, SSM_W)
        cb = conv_b[l].reshape(1, 2 * D_FF)
        cw = conv_w[l]
        kc = cache_k[:, l].reshape(nb_lat, past, ATT_W).astype(BF16)
        vc = cache_v[:, l].reshape(nb_lat, past, ATT_W).astype(BF16)

        zeros = jnp.zeros((SSM_GROUPS, SSM_P), F32)
        step = jnp.broadcast_to(ssm_log_step[l][:, :, None], (2, SSM_GROUPS, SSM_P))
        p1 = _dup_lanes(jnp.stack([ssm_lambda_re[l, 0], ssm_lambda_im[l, 0], step[0],
                                   ssm_lambda_re[l, 1], ssm_lambda_im[l, 1], step[1], zeros, zeros], axis=1))
        bt = lambda a: a.transpose(0, 2, 1)
        pb = _dup_lanes(jnp.stack([bt(ssm_b_re[l, 0]), bt(ssm_b_im[l, 0]), ssm_c_re[l, 0], ssm_c_im[l, 0],
                                   bt(ssm_b_re[l, 1]), bt(ssm_b_im[l, 1]), ssm_c_re[l, 1], ssm_c_im[l, 1]], axis=1))
        h0 = state_ssm[:, l].transpose(2, 0, 1, 4, 3).reshape(SSM_GROUPS, nb_lat, 4 * SSM_P)

        qc, kc_new, vc_new, uc = _premix(xp, mods_l, g1, win, gmat, qg, kg, None, len_ctx, 0, F32)
        ql, kl, vl, ul = _premix(xs, mods_l, g1, win, gmat, qg, kg, rope_tabs, len_lat, 1, BF16)

        oc = _attention(qc, kc_new, vc_new, lamp, sg, None, nb_ctx, len_ctx, lam_init)
        ol = _attention(ql, kl, vl, lamp, sg, (kc, vc), nb_lat, len_lat, lam_init)

        ut = _to_chunks(uc, ul, nb_ctx, len_ctx, nb_lat, len_lat)
        yt, hfin = _ssm(ut, h0, p1, pb, nb_ctx, len_ctx // CHUNK, nb_lat, len_lat // CHUNK)
        yc, yl = _from_chunks(yt, nb_ctx, len_ctx, nb_lat, len_lat)

        x1c, h2c = _postmix(xp, oc, yc, uc, mods_l, d, wglu, bglu, wout, g2, len_ctx, 0)
        x1l, h2l = _postmix(xs, ol, yl, ul, mods_l, d, wglu, bglu, wout, g2, len_lat, 1)

        xp = _ffn(h2c, x1c, mods_l, wup, cw, cb, wdown, len_ctx, 0)
        xs = _ffn(h2l, x1l, mods_l, wup, cw, cb, wdown, len_lat, 1)

        ks.append(kc_new.reshape(nb_ctx, len_ctx, N_HEADS, 2, D_SUB))
        vs.append(vc_new.reshape(nb_ctx, len_ctx, N_HEADS, D_V))
        ss.append(hfin.reshape(SSM_GROUPS, nb_ctx, 2, 2, SSM_P).transpose(1, 2, 0, 4, 3))

    return (xp.reshape(nb_ctx, len_ctx, D_MODEL), xs.reshape(nb_lat, len_lat, D_MODEL),
            jnp.stack(ks, axis=1), jnp.stack(vs, axis=1), jnp.stack(ss, axis=1))
```

```python
import functools
import math

import numpy as np
import jax
import jax.numpy as jnp
from jax import lax
from jax.experimental import pallas as pl
from jax.experimental.pallas import tpu as pltpu

F32 = jnp.float32
BF16 = jnp.bfloat16

D_MODEL = 1024
DEPTH = 2
GRID_W = 64
ATT_W = 512
SSM_W = 512
N_HEADS = 4
D_V = 128
D_SUB = 64
ROPE_AXIS = 32
ROPE_THETA = 10000.0
SSM_CH = 16
SSM_GROUPS = 32
SSM_P = 64
D_FF = 2048
N_MOD = 6
EPS = 1e-6

LANES = 128
BF16_ROWS = 16
GROUPS_PER_BLOCK = LANES // SSM_CH
CHUNK = 32
ATTN_TILE = 1024
ATTN_SUB = 256
MIX_TILE = 512
MIX_SUB = 256
FFN_ROWS = 1024
FFN_SUB = 256
FFN_SLAB = 512
MOD_COLS = 1536
VMEM_LIMIT = 56 * 1024 * 1024


def _nt(a, b):
    return lax.dot_general(a, b, (((1,), (1,)), ((), ())), preferred_element_type=F32)


def _nn(a, b):
    return jnp.dot(a, b, preferred_element_type=F32)


def _sigmoid(x):
    return 1.0 / (1.0 + jnp.exp(-x))


def _mod_kernel(c_ref, w_ref, b_ref, o_ref):
    c = c_ref[...]
    a = (c * _sigmoid(c)).astype(BF16)
    o_ref[0] = _nn(a, w_ref[0].astype(BF16)) + b_ref[0]


def _modulation(cond, w_mod, b_mod):
    nb = cond.shape[0]
    ncol = N_MOD * D_MODEL
    return pl.pallas_call(
        _mod_kernel,
        grid=(DEPTH, ncol // MOD_COLS),
        in_specs=[
            pl.BlockSpec((nb, D_MODEL), lambda l, j: (0, 0)),
            pl.BlockSpec((1, D_MODEL, MOD_COLS), lambda l, j: (l, 0, j)),
            pl.BlockSpec((1, 1, MOD_COLS), lambda l, j: (l, 0, j)),
        ],
        out_specs=pl.BlockSpec((1, nb, MOD_COLS), lambda l, j: (l, 0, j)),
        out_shape=jax.ShapeDtypeStruct((DEPTH, nb, ncol), F32),
        compiler_params=pltpu.CompilerParams(vmem_limit_bytes=VMEM_LIMIT),
        name="adaln_mod",
    )(cond, w_mod, b_mod.reshape(DEPTH, 1, ncol))


def _premix_kernel(x_ref, mod_ref, g1_ref, win_ref, gmat_ref, qg_ref, kg_ref, *rest, rope):
    kt_ref = None
    if rope:
        cos_ref, sina_ref, sinb_ref, q_ref, k_ref, v_ref, u_ref = rest
    else:
        q_ref, k_ref, v_ref, u_ref, kt_ref = rest
    mod = mod_ref[0]
    sh1 = mod[0:1]
    sc1 = mod[1:2]

    def head_norm(t, g):
        msq = _nn((t * t).astype(BF16), gmat_ref[...])
        return t * lax.rsqrt(msq + EPS) * g

    def rotate(t, rs):
        outs = []
        for j in range(ATT_W // LANES):
            s = t[:, j * LANES:(j + 1) * LANES]
            outs.append(s * cos_ref[rs, :] + pltpu.roll(s, LANES - 16, 1) * sina_ref[rs, :]
                        + pltpu.roll(s, 16, 1) * sinb_ref[rs, :])
        return jnp.concatenate(outs, axis=1)

    for sb in range(x_ref.shape[0] // MIX_SUB):
        rs = slice(sb * MIX_SUB, (sb + 1) * MIX_SUB)
        x = x_ref[rs, :]
        ms = jnp.mean(x * x, axis=-1, keepdims=True)
        h = x * lax.rsqrt(ms + EPS) * g1_ref[...]
        h = h * (1.0 + sc1) + sh1
        qkvu = _nn(h.astype(BF16), win_ref[...])
        q = head_norm(qkvu[:, 0:ATT_W], qg_ref[...])
        k = head_norm(qkvu[:, ATT_W:2 * ATT_W], kg_ref[...])
        if rope:
            q = rotate(q, rs)
            k = rotate(k, rs)
        q_ref[rs, :] = (q * (D_SUB ** -0.5)).astype(q_ref.dtype)
        k_ref[rs, :] = k.astype(k_ref.dtype)
        if kt_ref is not None:
            kt_ref[sb] = k.T
        v_ref[rs, :] = qkvu[:, 2 * ATT_W:3 * ATT_W].astype(v_ref.dtype)
        u_ref[rs, :] = qkvu[:, 3 * ATT_W:].astype(u_ref.dtype)


def _premix(x, mods_l, g1, win, gmat, qg, kg, rope_tabs, seq_len, mod_base, v_dtype):
    n = x.shape[0]
    tm = MIX_TILE
    per_seq = seq_len // tm
    rope = rope_tabs is not None
    assert rope or seq_len == MIX_SUB
    if mod_base == 0:
        mod_map = lambda i: (0, 0, 0)
    else:
        mod_map = lambda i: (mod_base + i // per_seq, 0, 0)
    row = lambda i: (i, 0)
    const = lambda i: (0, 0)
    in_specs = [
        pl.BlockSpec((tm, D_MODEL), row),
        pl.BlockSpec((1, N_MOD, D_MODEL), mod_map),
        pl.BlockSpec((1, D_MODEL), const),
        pl.BlockSpec((D_MODEL, 4 * ATT_W), const),
        pl.BlockSpec((ATT_W, ATT_W), const),
        pl.BlockSpec((1, ATT_W), const),
        pl.BlockSpec((1, ATT_W), const),
    ]
    args = [x, mods_l, g1, win, gmat, qg, kg]
    if rope:
        tab = lambda i: (i % per_seq, 0)
        in_specs += [pl.BlockSpec((tm, LANES), tab)] * 3
        args += list(rope_tabs)
    out = pl.BlockSpec((tm, ATT_W), row)
    out_specs = [out, out, out, out]
    out_shape = [jax.ShapeDtypeStruct((n, ATT_W), BF16),
                 jax.ShapeDtypeStruct((n, ATT_W), BF16),
                 jax.ShapeDtypeStruct((n, ATT_W), v_dtype),
                 jax.ShapeDtypeStruct((n, SSM_W), F32)]
    if not rope:
        out_specs.append(pl.BlockSpec((tm // seq_len, ATT_W, seq_len), lambda i: (i, 0, 0)))
        out_shape.append(jax.ShapeDtypeStruct((n // seq_len, ATT_W, seq_len), F32))
    return pl.pallas_call(
        functools.partial(_premix_kernel, rope=rope),
        grid=(n // tm,),
        in_specs=in_specs,
        out_specs=out_specs,
        out_shape=out_shape,
        compiler_params=pltpu.CompilerParams(dimension_semantics=("parallel",),
                                             vmem_limit_bytes=VMEM_LIMIT),
        name="premix_rope" if rope else "premix",
    )(*args)


def _attn_kernel(q_ref, k_ref, v_ref, lamp_ref, sg_ref, *rest, lam_init, has_cache):
    if has_cache:
        kc_ref, vc_ref, o_ref = rest
    else:
        (o_ref,) = rest
    tq = q_ref.shape[0]
    lp = lamp_ref[...]
    l1 = jnp.sum(lp[0:1] * lp[1:2], axis=-1, keepdims=True)
    l2 = jnp.sum(lp[2:3] * lp[3:4], axis=-1, keepdims=True)
    lam = jnp.exp(l1) - jnp.exp(l2) + lam_init
    first = lax.broadcasted_iota(jnp.int32, (1, LANES), 1) < D_SUB
    tsub = min(ATTN_SUB, tq)
    for h in range(N_HEADS):
        sl = slice(h * D_V, (h + 1) * D_V)
        kn = k_ref[:, sl].astype(BF16)
        vn = v_ref[:, sl].astype(BF16)
        for r0 in range(0, tq, tsub):
            qh = q_ref[r0:r0 + tsub, sl]
            zero = jnp.zeros_like(qh)
            qq = jnp.concatenate([jnp.where(first, qh, zero), jnp.where(first, zero, qh)], axis=0)
            if has_cache:
                ncache = kc_ref.shape[2]
                sc = jnp.concatenate([_nt(qq, kc_ref[0, 0, :, sl]), _nt(qq, kn)], axis=1)
            else:
                sc = _nt(qq, kn)
            m = jnp.max(sc, axis=-1, keepdims=True)
            e = jnp.exp(sc - m)
            ssum = jnp.sum(e, axis=-1, keepdims=True)
            eb = e.astype(BF16)
            if has_cache:
                pv = _nn(eb[:, :ncache], vc_ref[0, 0, :, sl]) + _nn(eb[:, ncache:], vn)
            else:
                pv = _nn(eb, vn)
            on = pv * (1.0 / ssum)
            o = on[:tsub] - lam * on[tsub:]
            ms = jnp.mean(o * o, axis=-1, keepdims=True)
            o_ref[r0:r0 + tsub, sl] = (o * lax.rsqrt(ms + EPS) * sg_ref[:, sl]
                                       * (1.0 - lam_init)).astype(o_ref.dtype)


def _attention(q, k, v, lamp, sg, cache, layer, batch, seq_len, lam_init):
    n = q.shape[0]
    tq = min(ATTN_TILE, seq_len)
    nq = seq_len // tq
    has_cache = cache is not None
    in_specs = [
        pl.BlockSpec((tq, ATT_W), lambda b, i: (b * nq + i, 0)),
        pl.BlockSpec((seq_len, ATT_W), lambda b, i: (b, 0)),
        pl.BlockSpec((seq_len, ATT_W), lambda b, i: (b, 0)),
        pl.BlockSpec((4, D_SUB), lambda b, i: (0, 0)),
        pl.BlockSpec((1, ATT_W), lambda b, i: (0, 0)),
    ]
    args = [q, k, v, lamp, sg]
    if has_cache:
        past = cache[0].shape[2]
        in_specs += [pl.BlockSpec((1, 1, past, ATT_W), lambda b, i: (b, layer, 0, 0))] * 2
        args += list(cache)
    return pl.pallas_call(
        functools.partial(_attn_kernel, lam_init=lam_init, has_cache=has_cache),
        grid=(batch, nq),
        in_specs=in_specs,
        out_specs=pl.BlockSpec((tq, ATT_W), lambda b, i: (b * nq + i, 0)),
        out_shape=jax.ShapeDtypeStruct((n, ATT_W), BF16),
        compiler_params=pltpu.CompilerParams(dimension_semantics=("parallel", "parallel"),
                                             vmem_limit_bytes=VMEM_LIMIT),
        name="attn_cache" if has_cache else "attn",
    )(*args)


def _ssm_kernel(ut_ref, h0_ref, p1_ref, pb_ref, yt_ref, hf_ref, r_scr, hin_scr, *,
                ctx_batch, ctx_chunks, lat_batch, lat_chunks):
    T = CHUNK
    W = SSM_CH * T
    lane = lax.broadcasted_iota(jnp.int32, (1, LANES), 1)
    lo = lane < SSM_P
    sgn_lr = jnp.where(lo, 1.0, -1.0).astype(F32)
    sgn_rl = -sgn_lr
    srow = lax.broadcasted_iota(jnp.int32, (T, LANES), 0).astype(F32)

    def swap(a):
        return pltpu.roll(a, SSM_P, 1)

    def tile_time(a):
        return jnp.concatenate([jnp.broadcast_to(a[i:i + 1, :], (SSM_CH, LANES)) for i in range(T)], axis=0)

    def rep_chan(a):
        return jnp.concatenate([a] * T, axis=0)

    def pair_pack(y):
        ys = swap(y)
        return jnp.where(lo, y, ys), jnp.where(lo, ys, y) * sgn_rl

    def cmul_const(x, yr2, yi2s):
        return x * yr2 + swap(x) * yi2s

    p1 = p1_ref[0]
    pb = pb_ref[0]

    def direction(d):
        lre2 = p1[3 * d + 0:3 * d + 1]
        lim2 = p1[3 * d + 1:3 * d + 2]
        ls2 = p1[3 * d + 2:3 * d + 3]
        bre2, bim2, cre2, cim2 = pb[4 * d + 0], pb[4 * d + 1], pb[4 * d + 2], pb[4 * d + 3]
        delta = jnp.exp(ls2)
        xr = lre2 * delta
        th = lim2 * delta

        def cpow_const(k):
            return jnp.exp(k * xr) * jnp.where(lo, jnp.cos(k * th), jnp.sin(k * th))

        mag1 = jnp.exp(xr)
        lbr2 = mag1 * jnp.cos(th)
        lbi2 = mag1 * jnp.sin(th)
        den = lre2 * lre2 + lim2 * lim2
        nr = lbr2 - 1.0
        cr2 = (nr * lre2 + lbi2 * lim2) / den
        ci2 = (lbi2 * lre2 - nr * lim2) / den
        bbr2 = rep_chan(cr2 * bre2 - ci2 * bim2)
        bbi2s = rep_chan((cr2 * bim2 + ci2 * bre2) * sgn_rl)
        ccr2 = rep_chan(cre2)
        cci2s = rep_chan(cim2 * sgn_rl)

        ang = srow * th
        cs = jnp.where(lo, jnp.cos(ang), jnp.sin(ang))
        wp = jnp.exp(srow * xr) * cs
        wn = jnp.exp(-(srow * xr)) * cs * sgn_lr

        def times(x, c1, c2s):
            return tile_time(x) * c1 + tile_time(swap(x)) * c2s

        lam_t = pair_pack(cpow_const(float(T)))
        if d == 0:
            left = times(wn, bbr2, bbi2s)
            bs = times(cmul_const(wn, *pair_pack(cpow_const(float(T - 1)))), bbr2, bbi2s)
            right = times(wp, ccr2, cci2s)
            cc = times(cmul_const(wp, *pair_pack(cpow_const(1.0))), ccr2, cci2s)
        else:
            left = times(wp, bbr2, bbi2s)
            bs = left
            right = times(wn, ccr2, cci2s)
            cc = times(cmul_const(wn, *lam_t), ccr2, cci2s)
        bbar = jnp.where(lo, bbr2, bbi2s)
        return left * sgn_lr, right, bs, cc * sgn_lr, lam_t, bbar

    lf, rf, bsf, ccf, lamt_f, bbar_f = direction(0)
    lb, rb, bsb, ccb, lamt_b, bbar_b = direction(1)

    s_idx = lax.broadcasted_iota(jnp.int32, (W, W), 0) // SSM_CH
    t_idx = lax.broadcasted_iota(jnp.int32, (W, W), 1) // SSM_CH
    a_t = (jnp.where(t_idx >= s_idx, _nt(lf.astype(BF16), rf.astype(BF16)), 0.0)
           + jnp.where(s_idx >= t_idx, _nt(lb.astype(BF16), rb.astype(BF16)), 0.0))
    w1 = jnp.concatenate([a_t, bsf, swap(bsf), bsb, swap(bsb)], axis=1).astype(BF16)
    r_scr[...] = _nn(ut_ref[0], w1)

    def chain(row0, nb, nchunks, col, lam_t, h, reverse):
        a1, a2s = lam_t
        hs = swap(h)
        order = range(nchunks - 1, -1, -1) if reverse else range(nchunks)
        hcol = slice(LANES, 2 * LANES) if reverse else slice(0, LANES)
        for c in order:
            rows = slice(row0 + c * nb, row0 + (c + 1) * nb)
            hin_scr[rows, hcol] = h
            s = r_scr[rows, col:col + LANES]
            ss = r_scr[rows, col + LANES:col + 2 * LANES]
            h, hs = h * a1 + hs * a2s + s, hs * a1 - h * a2s + ss
        return h

    zero = jnp.zeros((ctx_batch, LANES), F32)
    chain(0, ctx_batch, ctx_chunks, W, lamt_f, zero, False)
    chain(0, ctx_batch, ctx_chunks, W + 2 * LANES, lamt_b, zero, True)
    s_tiled = tile_time(srow)
    first_tok = jnp.where(s_tiled == 0.0, bbar_f, 0.0).astype(BF16)
    last_tok = jnp.where(s_tiled == float(T - 1), bbar_b, 0.0).astype(BF16)
    last_rows = (ctx_chunks - 1) * ctx_batch
    hf_ref[0, :, 0:LANES] = _nn(ut_ref[0, 0:ctx_batch, :], first_tok)
    hf_ref[0, :, LANES:2 * LANES] = _nn(ut_ref[0, last_rows:last_rows + ctx_batch, :], last_tok)
    lat0 = ctx_batch * ctx_chunks
    h0 = h0_ref[0]
    chain(lat0, lat_batch, lat_chunks, W, lamt_f, h0[:, 0:LANES], False)
    chain(lat0, lat_batch, lat_chunks, W + 2 * LANES, lamt_b, h0[:, LANES:2 * LANES], True)

    cc_cat = jnp.concatenate([ccf, ccb], axis=1).astype(BF16)
    yt_ref[0] = r_scr[:, 0:W] + _nt(hin_scr[...].astype(BF16), cc_cat)


def _ssm(ut, h0, p1, pb, ctx_batch, ctx_chunks, lat_batch, lat_chunks):
    g, n, w = ut.shape
    return pl.pallas_call(
        functools.partial(_ssm_kernel, ctx_batch=ctx_batch, ctx_chunks=ctx_chunks,
                          lat_batch=lat_batch, lat_chunks=lat_chunks),
        grid=(g,),
        in_specs=[
            pl.BlockSpec((1, n, w), lambda i: (i, 0, 0)),
            pl.BlockSpec((1, lat_batch, 2 * LANES), lambda i: (i, 0, 0)),
            pl.BlockSpec((1, 8, LANES), lambda i: (i, 0, 0)),
            pl.BlockSpec((1, 8, SSM_CH, LANES), lambda i: (i, 0, 0, 0)),
        ],
        out_specs=[
            pl.BlockSpec((1, n, w), lambda i: (i, 0, 0)),
            pl.BlockSpec((1, ctx_batch, 2 * LANES), lambda i: (i, 0, 0)),
        ],
        out_shape=[jax.ShapeDtypeStruct((g, n, w), F32),
                   jax.ShapeDtypeStruct((g, ctx_batch, 2 * LANES), F32)],
        scratch_shapes=[pltpu.VMEM((n, w + 4 * LANES), F32), pltpu.VMEM((n, 2 * LANES), F32)],
        compiler_params=pltpu.CompilerParams(dimension_semantics=("parallel",),
                                             vmem_limit_bytes=VMEM_LIMIT),
        name="s5_scan",
    )(ut, h0, p1, pb)


def _postmix_kernel(x_ref, oatt_ref, y_ref, u_ref, mod_ref, d_ref, wglu_ref, bglu_ref, wout_ref, g2_ref,
                    x1_ref, h2_ref):
    mod = mod_ref[0]
    gt1 = mod[2:3]
    sh2 = mod[3:4]
    sc2 = mod[4:5]
    for sb in range(x_ref.shape[0] // MIX_SUB):
        rs = slice(sb * MIX_SUB, (sb + 1) * MIX_SUB)
        y = y_ref[rs, :] + d_ref[...] * u_ref[rs, :]
        z = jax.nn.gelu(y)
        gate = _sigmoid(_nn(z.astype(BF16), wglu_ref[...]) + bglu_ref[...])
        o_ssm = (z * gate).astype(BF16)
        mix = _nn(oatt_ref[rs, :], wout_ref[0:ATT_W, :]) + _nn(o_ssm, wout_ref[ATT_W:, :])
        x1 = x_ref[rs, :] + gt1 * mix
        x1_ref[rs, :] = x1
        ms = jnp.mean(x1 * x1, axis=-1, keepdims=True)
        h2 = x1 * lax.rsqrt(ms + EPS) * g2_ref[...]
        h2_ref[rs, :] = (h2 * (1.0 + sc2) + sh2).astype(BF16)


def _postmix(x, oatt, y, u, mods_l, d, wglu, bglu, wout, g2, seq_len, mod_base):
    n = x.shape[0]
    tm = MIX_TILE
    per_seq = seq_len // tm
    if mod_base == 0:
        mod_map = lambda i: (0, 0, 0)
    else:
        mod_map = lambda i: (mod_base + i // per_seq, 0, 0)
    row = lambda i: (i, 0)
    const = lambda i: (0, 0)
    return pl.pallas_call(
        _postmix_kernel,
        grid=(n // tm,),
        in_specs=[
            pl.BlockSpec((tm, D_MODEL), row),
            pl.BlockSpec((tm, ATT_W), row),
            pl.BlockSpec((tm, SSM_W), row),
            pl.BlockSpec((tm, SSM_W), row),
            pl.BlockSpec((1, N_MOD, D_MODEL), mod_map),
            pl.BlockSpec((1, SSM_W), const),
            pl.BlockSpec((SSM_W, SSM_W), const),
            pl.BlockSpec((1, SSM_W), const),
            pl.BlockSpec((D_MODEL, D_MODEL), const),
            pl.BlockSpec((1, D_MODEL), const),
        ],
        out_specs=[pl.BlockSpec((tm, D_MODEL), row), pl.BlockSpec((tm, D_MODEL), row)],
        out_shape=[jax.ShapeDtypeStruct((n, D_MODEL), F32), jax.ShapeDtypeStruct((n, D_MODEL), BF16)],
        compiler_params=pltpu.CompilerParams(dimension_semantics=("parallel",),
                                             vmem_limit_bytes=VMEM_LIMIT),
        name="postmix",
    )(x, oatt, y, u, mods_l, d, wglu, bglu, wout, g2)


def _ffn_kernel(h2_ref, x1_ref, mod_ref, wup_ref, cw_ref, cb_ref, wd_ref, o_ref, *, seq_len):
    rows = h2_ref.shape[0]
    sub = FFN_SUB
    halo = BF16_ROWS
    tf = FFN_SLAB
    gt2 = mod_ref[0][5:6]
    row8 = lax.broadcasted_iota(jnp.int32, (8, 1), 0)

    def conv(up, lead, has_prev, has_next, cw, cb):
        main = up[lead:lead + sub]
        prev = pltpu.roll(main, 1, 0)
        nxt = pltpu.roll(main, sub - 1, 0)
        before = up[lead - 1:lead] if has_prev else jnp.zeros((1, tf), F32)
        after = up[lead + sub:lead + sub + 1] if has_next else jnp.zeros((1, tf), F32)
        prev = jnp.concatenate([jnp.where(row8 == 0, before, prev[0:8]), prev[8:]], axis=0)
        nxt = jnp.concatenate([nxt[:sub - 8], jnp.where(row8 == 7, after, nxt[sub - 8:])], axis=0)
        return prev * cw[0:1] + main * cw[1:2] + nxt * cw[2:3] + cb

    for k in range(rows // sub):
        r0 = k * sub
        has_prev = r0 % seq_len != 0
        has_next = (r0 + sub) % seq_len != 0
        lead = halo if has_prev else 0
        h = h2_ref[r0 - lead:r0 + sub + (halo if has_next else 0), :]
        acts = []
        for j in range(D_FF // tf):
            cv = slice(j * tf, (j + 1) * tf)
            cg = slice(D_FF + j * tf, D_FF + (j + 1) * tf)
            val = conv(_nn(h, wup_ref[:, cv]), lead, has_prev, has_next, cw_ref[:, cv], cb_ref[:, cv])
            gate = conv(_nn(h, wup_ref[:, cg]), lead, has_prev, has_next, cw_ref[:, cg], cb_ref[:, cg])
            acts.append((gate * _sigmoid(gate) * val).astype(BF16))
        act = jnp.concatenate(acts, axis=1)
        o_ref[r0:r0 + sub, :] = x1_ref[r0:r0 + sub, :] + gt2 * _nn(act, wd_ref[...])


def _ffn(h2, x1, mods_l, wup, cw, cb, wdown, seq_len, mod_base):
    n = h2.shape[0]
    rows = FFN_ROWS
    per_seq = seq_len // rows if seq_len >= rows else 1
    if mod_base == 0:
        mod_map = lambda i: (0, 0, 0)
    else:
        mod_map = lambda i: (mod_base + i // per_seq, 0, 0)
    const = lambda i: (0, 0)
    resident = dict(pipeline_mode=pl.Buffered(1))
    return pl.pallas_call(
        functools.partial(_ffn_kernel, seq_len=seq_len),
        grid=(n // rows,),
        in_specs=[
            pl.BlockSpec((rows, D_MODEL), lambda i: (i, 0)),
            pl.BlockSpec((rows, D_MODEL), lambda i: (i, 0)),
            pl.BlockSpec((1, N_MOD, D_MODEL), mod_map),
            pl.BlockSpec((D_MODEL, 2 * D_FF), const, **resident),
            pl.BlockSpec((3, 2 * D_FF), const),
            pl.BlockSpec((1, 2 * D_FF), const),
            pl.BlockSpec((D_FF, D_MODEL), const, **resident),
        ],
        out_specs=pl.BlockSpec((rows, D_MODEL), lambda i: (i, 0)),
        out_shape=jax.ShapeDtypeStruct((n, D_MODEL), F32),
        compiler_params=pltpu.CompilerParams(dimension_semantics=("parallel",),
                                             vmem_limit_bytes=VMEM_LIMIT),
        name="conv_ffn",
    )(h2, x1, mods_l, wup, cw, cb, wdown)


def _rope_tables(n_tokens):
    pos = np.arange(n_tokens)
    row = (pos // GRID_W).astype(np.float64)
    col = (pos % GRID_W).astype(np.float64)
    nf = ROPE_AXIS // 2
    inv_freq = 1.0 / (ROPE_THETA ** (np.arange(nf, dtype=np.float64) / nf))
    lane = np.arange(LANES)
    freq = inv_freq[lane % nf]
    is_col = (lane % D_SUB) >= ROPE_AXIS
    ang = np.where(is_col[None, :], col[:, None], row[:, None]) * freq[None, :]
    second = (lane % ROPE_AXIS) >= nf
    cos = np.cos(ang)
    sin = np.sin(ang)
    sina = np.where(second[None, :], 0.0, -sin)
    sinb = np.where(second[None, :], sin, 0.0)
    return tuple(jnp.asarray(t, dtype=F32) for t in (cos, sina, sinb))


def _segment_transpose(vs):
    seg = lax.broadcasted_iota(jnp.int32, (1, LANES), 1) // SSM_CH
    vs = list(vs)
    for d in (4, 2, 1):
        keep = (seg & d) == 0
        shift = d * SSM_CH
        for a in range(GROUPS_PER_BLOCK):
            if a & d:
                continue
            lo, hi = vs[a], vs[a + d]
            vs[a] = jnp.where(keep, lo, pltpu.roll(hi, shift, 1))
            vs[a + d] = jnp.where(keep, pltpu.roll(lo, LANES - shift, 1), hi)
    return vs


def _chunk_layout_passes(ctx_batch, ctx_len, lat_batch, lat_len):
    ctx_rows = ctx_batch * (ctx_len // CHUNK)
    return ((0, ctx_batch, ctx_len, 0, BF16_ROWS // ctx_batch if ctx_batch < BF16_ROWS else 1),
            (1, lat_batch, lat_len, ctx_rows, BF16_ROWS // lat_batch if lat_batch < BF16_ROWS else 1))


def _to_chunks_kernel(uc_ref, ul_ref, o_ref, *, passes):
    for idx, batch, seq_len, row0, cstep in passes:
        u_ref = (uc_ref, ul_ref)[idx]

        def body(k, carry, u_ref=u_ref, batch=batch, seq_len=seq_len, row0=row0, cstep=cstep):
            for j in range(CHUNK // GROUPS_PER_BLOCK):
                pieces = [[] for _ in range(GROUPS_PER_BLOCK)]
                for cc in range(cstep):
                    base = (k * cstep + cc) * CHUNK + j * GROUPS_PER_BLOCK
                    vs = [u_ref[pl.ds(base + s, batch, stride=seq_len), :] for s in range(GROUPS_PER_BLOCK)]
                    for g, w in enumerate(_segment_transpose(vs)):
                        pieces[g].append(w)
                start = pl.multiple_of(row0 + k * cstep * batch, BF16_ROWS)
                for g in range(GROUPS_PER_BLOCK):
                    blk = pieces[g][0] if cstep == 1 else jnp.concatenate(pieces[g], axis=0)
                    o_ref[g, pl.ds(start, cstep * batch), j * LANES:(j + 1) * LANES] = blk.astype(o_ref.dtype)
            return carry

        lax.fori_loop(0, seq_len // CHUNK // cstep, body, 0)


def _from_chunks_kernel(y_ref, oc_ref, ol_ref, *, passes):
    for idx, batch, seq_len, row0, _ in passes:
        o_ref = (oc_ref, ol_ref)[idx]

        def body(c, carry, o_ref=o_ref, batch=batch, seq_len=seq_len, row0=row0):
            start = pl.multiple_of(row0 + c * batch, 8)
            for j in range(CHUNK // GROUPS_PER_BLOCK):
                ws = [y_ref[g, pl.ds(start, batch), j * LANES:(j + 1) * LANES] for g in range(GROUPS_PER_BLOCK)]
                base = c * CHUNK + j * GROUPS_PER_BLOCK
                for s, v in enumerate(_segment_transpose(ws)):
                    o_ref[pl.ds(base + s, batch, stride=seq_len), :] = v
            return carry

        lax.fori_loop(0, seq_len // CHUNK, body, 0)


def _to_chunks(uc, ul, ctx_batch, ctx_len, lat_batch, lat_len):
    passes = _chunk_layout_passes(ctx_batch, ctx_len, lat_batch, lat_len)
    rows = ctx_batch * (ctx_len // CHUNK) + lat_batch * (lat_len // CHUNK)
    return pl.pallas_call(
        functools.partial(_to_chunks_kernel, passes=passes),
        grid=(SSM_GROUPS // GROUPS_PER_BLOCK,),
        in_specs=[pl.BlockSpec((uc.shape[0], LANES), lambda j: (0, j)),
                  pl.BlockSpec((ul.shape[0], LANES), lambda j: (0, j))],
        out_specs=pl.BlockSpec((GROUPS_PER_BLOCK, rows, SSM_CH * CHUNK), lambda j: (j, 0, 0)),
        out_shape=jax.ShapeDtypeStruct((SSM_GROUPS, rows, SSM_CH * CHUNK), BF16),
        compiler_params=pltpu.CompilerParams(dimension_semantics=("parallel",),
                                             vmem_limit_bytes=VMEM_LIMIT),
        name="to_chunks",
    )(uc, ul)


def _from_chunks(yt, ctx_batch, ctx_len, lat_batch, lat_len):
    passes = _chunk_layout_passes(ctx_batch, ctx_len, lat_batch, lat_len)
    rows = yt.shape[1]
    n_ctx = ctx_batch * ctx_len
    n_lat = lat_batch * lat_len
    return pl.pallas_call(
        functools.partial(_from_chunks_kernel, passes=passes),
        grid=(SSM_GROUPS // GROUPS_PER_BLOCK,),
        in_specs=[pl.BlockSpec((GROUPS_PER_BLOCK, rows, SSM_CH * CHUNK), lambda j: (j, 0, 0))],
        out_specs=[pl.BlockSpec((n_ctx, LANES), lambda j: (0, j)),
                   pl.BlockSpec((n_lat, LANES), lambda j: (0, j))],
        out_shape=[jax.ShapeDtypeStruct((n_ctx, SSM_W), F32), jax.ShapeDtypeStruct((n_lat, SSM_W), F32)],
        compiler_params=pltpu.CompilerParams(dimension_semantics=("parallel",),
                                             vmem_limit_bytes=VMEM_LIMIT),
        name="from_chunks",
    )(yt)


def _dup_lanes(a):
    return jnp.concatenate([a, a], axis=-1)


def kernel(x_prompt, x_sample, cache_k, cache_v, state_ssm, c, c_ctx, w_mod, b_mod, g_norm1, w_in, q_norm, k_norm, lambda_q1, lambda_k1, lambda_q2, lambda_k2, subln_g, ssm_lambda_re, ssm_lambda_im, ssm_log_step, ssm_b_re, ssm_b_im, ssm_c_re, ssm_c_im, ssm_d, w_glu, b_glu, w_out, g_norm2, w_up, conv_w, conv_b, w_down):
    nb_ctx, len_ctx, _ = x_prompt.shape
    nb_lat, len_lat, _ = x_sample.shape
    past = cache_k.shape[2]

    n_cond = 16
    cond = jnp.zeros((n_cond, D_MODEL), F32).at[0].set(c_ctx).at[1:1 + nb_lat].set(c)
    mods = _modulation(cond, w_mod, b_mod).reshape(DEPTH, n_cond, N_MOD, D_MODEL)

    gidx = np.arange(ATT_W) // D_SUB
    gmat = jnp.asarray(np.where(gidx[:, None] == gidx[None, :], 1.0 / D_SUB, 0.0), dtype=BF16)
    cache_kb = cache_k.reshape(nb_lat, DEPTH, past, ATT_W).astype(BF16)
    cache_vb = cache_v.reshape(nb_lat, DEPTH, past, ATT_W).astype(BF16)
    rope_tabs = _rope_tables(len_lat)

    xp = x_prompt.reshape(nb_ctx * len_ctx, D_MODEL)
    xs = x_sample.reshape(nb_lat * len_lat, D_MODEL)
    ks, vs, ss = [], [], []
    for l in range(DEPTH):
        lam_init = 0.8 - 0.6 * math.exp(-0.3 * l)
        mods_l = mods[l]
        g1 = g_norm1[l].reshape(1, D_MODEL)
        g2 = g_norm2[l].reshape(1, D_MODEL)
        win = w_in[l].astype(BF16)
        wout = w_out[l].astype(BF16)
        wglu = w_glu[l].astype(BF16)
        wup = w_up[l].astype(BF16)
        wdown = w_down[l].astype(BF16)
        qg = jnp.tile(q_norm[l], ATT_W // D_SUB).reshape(1, ATT_W)
        kg = jnp.tile(k_norm[l], ATT_W // D_SUB).reshape(1, ATT_W)
        sg = jnp.tile(subln_g[l], N_HEADS).reshape(1, ATT_W)
        lamp = jnp.stack([lambda_q1[l], lambda_k1[l], lambda_q2[l], lambda_k2[l]])
        d = ssm_d[l].reshape(1, SSM_W)
        bglu = b_glu[l].reshape(1, SSM_W)
        cb = conv_b[l].reshape(1, 2 * D_FF)
        cw = conv_w[l]

        zeros = jnp.zeros((SSM_GROUPS, SSM_P), F32)
        step = jnp.broadcast_to(ssm_log_step[l][:, :, None], (2, SSM_GROUPS, SSM_P))
        p1 = _dup_lanes(jnp.stack([ssm_lambda_re[l, 0], ssm_lambda_im[l, 0], step[0],
                                   ssm_lambda_re[l, 1], ssm_lambda_im[l, 1], step[1], zeros, zeros], axis=1))
        bt = lambda a: a.transpose(0, 2, 1)
        pb = _dup_lanes(jnp.stack([bt(ssm_b_re[l, 0]), bt(ssm_b_im[l, 0]), ssm_c_re[l, 0], ssm_c_im[l, 0],
                                   bt(ssm_b_re[l, 1]), bt(ssm_b_im[l, 1]), ssm_c_re[l, 1], ssm_c_im[l, 1]], axis=1))
        h0 = state_ssm[:, l].transpose(2, 0, 1, 4, 3).reshape(SSM_GROUPS, nb_lat, 4 * SSM_P)

        qc, kc_new, vc_new, uc, kt = _premix(xp, mods_l, g1, win, gmat, qg, kg, None, len_ctx, 0, F32)
        ql, kl, vl, ul = _premix(xs, mods_l, g1, win, gmat, qg, kg, rope_tabs, len_lat, 1, BF16)

        oc = _attention(qc, kc_new, vc_new, lamp, sg, None, l, nb_ctx, len_ctx, lam_init)
        ol = _attention(ql, kl, vl, lamp, sg, (cache_kb, cache_vb), l, nb_lat, len_lat, lam_init)

        ut = _to_chunks(uc, ul, nb_ctx, len_ctx, nb_lat, len_lat)
        yt, hfin = _ssm(ut, h0, p1, pb, nb_ctx, len_ctx // CHUNK, nb_lat, len_lat // CHUNK)
        yc, yl = _from_chunks(yt, nb_ctx, len_ctx, nb_lat, len_lat)

        x1c, h2c = _postmix(xp, oc, yc, uc, mods_l, d, wglu, bglu, wout, g2, len_ctx, 0)
        x1l, h2l = _postmix(xs, ol, yl, ul, mods_l, d, wglu, bglu, wout, g2, len_lat, 1)

        xp = _ffn(h2c, x1c, mods_l, wup, cw, cb, wdown, len_ctx, 0)
        xs = _ffn(h2l, x1l, mods_l, wup, cw, cb, wdown, len_lat, 1)

        ks.append(kt)
        vs.append(vc_new.reshape(nb_ctx, len_ctx, N_HEADS, D_V))
        ss.append(hfin.reshape(SSM_GROUPS, nb_ctx, 2, 2, SSM_P).transpose(1, 2, 0, 4, 3))

    new_k = jnp.stack(ks, axis=1).reshape(nb_ctx, DEPTH, N_HEADS, 2, D_SUB, len_ctx).transpose(0, 1, 5, 2, 3, 4)
    return (xp.reshape(nb_ctx, len_ctx, D_MODEL), xs.reshape(nb_lat, len_lat, D_MODEL),
            new_k, jnp.stack(vs, axis=1), jnp.stack(ss, axis=1))
```

```python
import functools
import math

import numpy as np
import jax
import jax.numpy as jnp
from jax import lax
from jax.experimental import pallas as pl
from jax.experimental.pallas import tpu as pltpu

F32 = jnp.float32
BF16 = jnp.bfloat16

D_MODEL = 1024
DEPTH = 2
GRID_W = 64
ATT_W = 512
SSM_W = 512
N_HEADS = 4
D_V = 128
D_SUB = 64
ROPE_AXIS = 32
ROPE_THETA = 10000.0
SSM_CH = 16
SSM_GROUPS = 32
SSM_P = 64
D_FF = 2048
N_MOD = 6
EPS = 1e-6

LANES = 128
BF16_ROWS = 16
GROUPS_PER_BLOCK = LANES // SSM_CH
CHUNK = 32
ATTN_TILE = 1024
ATTN_SUB = 256
MIX_TILE = 512
MIX_SUB = 256
FFN_ROWS = 1024
FFN_SUB = 256
FFN_SLAB = 512
MOD_COLS = 1536
VMEM_LIMIT = 56 * 1024 * 1024


def _nt(a, b):
    return lax.dot_general(a, b, (((1,), (1,)), ((), ())), preferred_element_type=F32)


def _nn(a, b):
    return jnp.dot(a, b, preferred_element_type=F32)


def _sigmoid(x):
    return 1.0 / (1.0 + jnp.exp(-x))


def _mod_kernel(c_ref, w_ref, b_ref, o_ref):
    c = c_ref[...]
    a = (c * _sigmoid(c)).astype(BF16)
    o_ref[0] = _nn(a, w_ref[0].astype(BF16)) + b_ref[0]


def _modulation(cond, w_mod, b_mod):
    nb = cond.shape[0]
    ncol = N_MOD * D_MODEL
    return pl.pallas_call(
        _mod_kernel,
        grid=(DEPTH, ncol // MOD_COLS),
        in_specs=[
            pl.BlockSpec((nb, D_MODEL), lambda l, j: (0, 0)),
            pl.BlockSpec((1, D_MODEL, MOD_COLS), lambda l, j: (l, 0, j)),
            pl.BlockSpec((1, 1, MOD_COLS), lambda l, j: (l, 0, j)),
        ],
        out_specs=pl.BlockSpec((1, nb, MOD_COLS), lambda l, j: (l, 0, j)),
        out_shape=jax.ShapeDtypeStruct((DEPTH, nb, ncol), F32),
        compiler_params=pltpu.CompilerParams(vmem_limit_bytes=VMEM_LIMIT),
        name="adaln_mod",
    )(cond, w_mod, b_mod.reshape(DEPTH, 1, ncol))


def _premix_kernel(x_ref, mod_ref, g1_ref, win_ref, gmat_ref, qg_ref, kg_ref, *rest, rope):
    kt_ref = vt_ref = None
    if rope:
        cos_ref, sina_ref, sinb_ref, q_ref, k_ref, v_ref, u_ref = rest
    else:
        q_ref, k_ref, v_ref, u_ref, kt_ref, vt_ref = rest
    mod = mod_ref[0]
    sh1 = mod[0:1]
    sc1 = mod[1:2]

    def head_norm(t, g):
        msq = _nn((t * t).astype(BF16), gmat_ref[...])
        return t * lax.rsqrt(msq + EPS) * g

    def rotate(t, rs):
        outs = []
        for j in range(ATT_W // LANES):
            s = t[:, j * LANES:(j + 1) * LANES]
            outs.append(s * cos_ref[rs, :] + pltpu.roll(s, LANES - 16, 1) * sina_ref[rs, :]
                        + pltpu.roll(s, 16, 1) * sinb_ref[rs, :])
        return jnp.concatenate(outs, axis=1)

    for sb in range(x_ref.shape[0] // MIX_SUB):
        rs = slice(sb * MIX_SUB, (sb + 1) * MIX_SUB)
        x = x_ref[rs, :]
        ms = jnp.mean(x * x, axis=-1, keepdims=True)
        h = x * lax.rsqrt(ms + EPS) * g1_ref[...]
        h = h * (1.0 + sc1) + sh1
        qkvu = _nn(h.astype(BF16), win_ref[0])
        q = head_norm(qkvu[:, 0:ATT_W], qg_ref[...])
        k = head_norm(qkvu[:, ATT_W:2 * ATT_W], kg_ref[...])
        if rope:
            q = rotate(q, rs)
            k = rotate(k, rs)
        q_ref[rs, :] = (q * (D_SUB ** -0.5)).astype(q_ref.dtype)
        k_ref[rs, :] = k.astype(k_ref.dtype)
        v = qkvu[:, 2 * ATT_W:3 * ATT_W]
        if kt_ref is not None:
            kt_ref[sb] = k.T
            for hd in range(N_HEADS):
                vt_ref[pl.ds(sb * MIX_SUB * N_HEADS + hd, MIX_SUB, stride=N_HEADS), :] = (
                    v[:, hd * D_V:(hd + 1) * D_V])
        v_ref[rs, :] = v.astype(v_ref.dtype)
        u_ref[rs, :] = qkvu[:, 3 * ATT_W:].astype(u_ref.dtype)


def _premix(x, mods_l, g1, win, layer, gmat, qg, kg, rope_tabs, seq_len, mod_base):
    n = x.shape[0]
    tm = MIX_TILE
    per_seq = seq_len // tm
    rope = rope_tabs is not None
    assert rope or seq_len == MIX_SUB
    if mod_base == 0:
        mod_map = lambda i: (0, 0, 0)
    else:
        mod_map = lambda i: (mod_base + i // per_seq, 0, 0)
    row = lambda i: (i, 0)
    const = lambda i: (0, 0)
    in_specs = [
        pl.BlockSpec((tm, D_MODEL), row),
        pl.BlockSpec((1, N_MOD, D_MODEL), mod_map),
        pl.BlockSpec((1, D_MODEL), const),
        pl.BlockSpec((1, D_MODEL, 4 * ATT_W), lambda i: (layer, 0, 0)),
        pl.BlockSpec((ATT_W, ATT_W), const),
        pl.BlockSpec((1, ATT_W), const),
        pl.BlockSpec((1, ATT_W), const),
    ]
    args = [x, mods_l, g1, win, gmat, qg, kg]
    if rope:
        tab = lambda i: (i % per_seq, 0)
        in_specs += [pl.BlockSpec((tm, LANES), tab)] * 3
        args += list(rope_tabs)
    out = pl.BlockSpec((tm, ATT_W), row)
    out_specs = [out, out, out, out]
    out_shape = [jax.ShapeDtypeStruct((n, ATT_W), BF16),
                 jax.ShapeDtypeStruct((n, ATT_W), BF16),
                 jax.ShapeDtypeStruct((n, ATT_W), BF16),
                 jax.ShapeDtypeStruct((n, SSM_W), F32)]
    if not rope:
        out_specs.append(pl.BlockSpec((tm // seq_len, ATT_W, seq_len), lambda i: (i, 0, 0)))
        out_shape.append(jax.ShapeDtypeStruct((n // seq_len, ATT_W, seq_len), F32))
        out_specs.append(pl.BlockSpec((tm * N_HEADS, D_V), lambda i: (i, 0)))
        out_shape.append(jax.ShapeDtypeStruct((n * N_HEADS, D_V), F32))
    return pl.pallas_call(
        functools.partial(_premix_kernel, rope=rope),
        grid=(n // tm,),
        in_specs=in_specs,
        out_specs=out_specs,
        out_shape=out_shape,
        compiler_params=pltpu.CompilerParams(dimension_semantics=("parallel",),
                                             vmem_limit_bytes=VMEM_LIMIT),
        name="premix_rope" if rope else "premix",
    )(*args)


def _attn_kernel(q_ref, k_ref, v_ref, lamp_ref, sg_ref, *rest, lam_init, has_cache):
    if has_cache:
        kc_ref, vc_ref, o_ref = rest
    else:
        (o_ref,) = rest
    tq = q_ref.shape[0]
    lp = lamp_ref[...]
    l1 = jnp.sum(lp[0:1] * lp[1:2], axis=-1, keepdims=True)
    l2 = jnp.sum(lp[2:3] * lp[3:4], axis=-1, keepdims=True)
    lam = jnp.exp(l1) - jnp.exp(l2) + lam_init
    first = lax.broadcasted_iota(jnp.int32, (1, LANES), 1) < D_SUB
    tsub = min(ATTN_SUB, tq)
    for h in range(N_HEADS):
        sl = slice(h * D_V, (h + 1) * D_V)
        kn = k_ref[:, sl]
        vn = v_ref[:, sl]
        if has_cache:
            ncache = kc_ref.shape[3]
            kct = kc_ref[0, 0, sl, :].astype(BF16)
            vct = vc_ref[0, 0, pl.ds(h, ncache, stride=N_HEADS), :].astype(BF16)
        for r0 in range(0, tq, tsub):
            qh = q_ref[r0:r0 + tsub, sl]
            zero = jnp.zeros_like(qh)
            qq = jnp.concatenate([jnp.where(first, qh, zero), jnp.where(first, zero, qh)], axis=0)
            if has_cache:
                sc = jnp.concatenate([_nn(qq, kct), _nt(qq, kn)], axis=1)
            else:
                sc = _nt(qq, kn)
            m = jnp.max(sc, axis=-1, keepdims=True)
            e = jnp.exp(sc - m)
            ssum = jnp.sum(e, axis=-1, keepdims=True)
            eb = e.astype(BF16)
            if has_cache:
                pv = _nn(eb[:, :ncache], vct) + _nn(eb[:, ncache:], vn)
            else:
                pv = _nn(eb, vn)
            on = pv * (1.0 / ssum)
            o = on[:tsub] - lam * on[tsub:]
            ms = jnp.mean(o * o, axis=-1, keepdims=True)
            o_ref[r0:r0 + tsub, sl] = (o * lax.rsqrt(ms + EPS) * sg_ref[:, sl]
                                       * (1.0 - lam_init)).astype(o_ref.dtype)


def _attention(q, k, v, lamp, sg, cache, layer, batch, seq_len, lam_init):
    n = q.shape[0]
    tq = min(ATTN_TILE, seq_len)
    nq = seq_len // tq
    has_cache = cache is not None
    in_specs = [
        pl.BlockSpec((tq, ATT_W), lambda b, i: (b * nq + i, 0)),
        pl.BlockSpec((seq_len, ATT_W), lambda b, i: (b, 0)),
        pl.BlockSpec((seq_len, ATT_W), lambda b, i: (b, 0)),
        pl.BlockSpec((4, D_SUB), lambda b, i: (0, 0)),
        pl.BlockSpec((1, ATT_W), lambda b, i: (0, 0)),
    ]
    args = [q, k, v, lamp, sg]
    if has_cache:
        past = cache[0].shape[3]
        in_specs += [pl.BlockSpec((1, 1, ATT_W, past), lambda b, i: (b, layer, 0, 0)),
                     pl.BlockSpec((1, 1, past * N_HEADS, D_V), lambda b, i: (b, layer, 0, 0))]
        args += list(cache)
    return pl.pallas_call(
        functools.partial(_attn_kernel, lam_init=lam_init, has_cache=has_cache),
        grid=(batch, nq),
        in_specs=in_specs,
        out_specs=pl.BlockSpec((tq, ATT_W), lambda b, i: (b * nq + i, 0)),
        out_shape=jax.ShapeDtypeStruct((n, ATT_W), BF16),
        compiler_params=pltpu.CompilerParams(dimension_semantics=("parallel", "parallel"),
                                             vmem_limit_bytes=VMEM_LIMIT),
        name="attn_cache" if has_cache else "attn",
    )(*args)


def _ssm_kernel(ut_ref, h0_ref, p1_ref, pb_ref, yt_ref, hf_ref, r_scr, hin_scr, *,
                ctx_batch, ctx_chunks, lat_batch, lat_chunks):
    T = CHUNK
    W = SSM_CH * T
    lane = lax.broadcasted_iota(jnp.int32, (1, LANES), 1)
    lo = lane < SSM_P
    sgn_lr = jnp.where(lo, 1.0, -1.0).astype(F32)
    sgn_rl = -sgn_lr
    srow = lax.broadcasted_iota(jnp.int32, (T, LANES), 0).astype(F32)

    def swap(a):
        return pltpu.roll(a, SSM_P, 1)

    def tile_time(a):
        return jnp.concatenate([jnp.broadcast_to(a[i:i + 1, :], (SSM_CH, LANES)) for i in range(T)], axis=0)

    def rep_chan(a):
        return jnp.concatenate([a] * T, axis=0)

    def pair_pack(y):
        ys = swap(y)
        return jnp.where(lo, y, ys), jnp.where(lo, ys, y) * sgn_rl

    def cmul_const(x, yr2, yi2s):
        return x * yr2 + swap(x) * yi2s

    p1 = p1_ref[0]
    pb = pb_ref[0]

    def direction(d):
        lre2 = p1[3 * d + 0:3 * d + 1]
        lim2 = p1[3 * d + 1:3 * d + 2]
        ls2 = p1[3 * d + 2:3 * d + 3]
        bre2, bim2, cre2, cim2 = pb[4 * d + 0], pb[4 * d + 1], pb[4 * d + 2], pb[4 * d + 3]
        delta = jnp.exp(ls2)
        xr = lre2 * delta
        th = lim2 * delta

        def cpow_const(k):
            return jnp.exp(k * xr) * jnp.where(lo, jnp.cos(k * th), jnp.sin(k * th))

        mag1 = jnp.exp(xr)
        lbr2 = mag1 * jnp.cos(th)
        lbi2 = mag1 * jnp.sin(th)
        den = lre2 * lre2 + lim2 * lim2
        nr = lbr2 - 1.0
        cr2 = (nr * lre2 + lbi2 * lim2) / den
        ci2 = (lbi2 * lre2 - nr * lim2) / den
        bbr2 = rep_chan(cr2 * bre2 - ci2 * bim2)
        bbi2s = rep_chan((cr2 * bim2 + ci2 * bre2) * sgn_rl)
        ccr2 = rep_chan(cre2)
        cci2s = rep_chan(cim2 * sgn_rl)

        ang = srow * th
        cs = jnp.where(lo, jnp.cos(ang), jnp.sin(ang))
        wp = jnp.exp(srow * xr) * cs
        wn = jnp.exp(-(srow * xr)) * cs * sgn_lr

        def times(x, c1, c2s):
            return tile_time(x) * c1 + tile_time(swap(x)) * c2s

        lam_t = pair_pack(cpow_const(float(T)))
        if d == 0:
            left = times(wn, bbr2, bbi2s)
            bs = times(cmul_const(wn, *pair_pack(cpow_const(float(T - 1)))), bbr2, bbi2s)
            right = times(wp, ccr2, cci2s)
            cc = times(cmul_const(wp, *pair_pack(cpow_const(1.0))), ccr2, cci2s)
        else:
            left = times(wp, bbr2, bbi2s)
            bs = left
            right = times(wn, ccr2, cci2s)
            cc = times(cmul_const(wn, *lam_t), ccr2, cci2s)
        bbar = jnp.where(lo, bbr2, bbi2s)
        return left * sgn_lr, right, bs, cc * sgn_lr, lam_t, bbar

    lf, rf, bsf, ccf, lamt_f, bbar_f = direction(0)
    lb, rb, bsb, ccb, lamt_b, bbar_b = direction(1)

    s_idx = lax.broadcasted_iota(jnp.int32, (W, W), 0) // SSM_CH
    t_idx = lax.broadcasted_iota(jnp.int32, (W, W), 1) // SSM_CH
    a_t = (jnp.where(t_idx >= s_idx, _nt(lf.astype(BF16), rf.astype(BF16)), 0.0)
           + jnp.where(s_idx >= t_idx, _nt(lb.astype(BF16), rb.astype(BF16)), 0.0))
    w1 = jnp.concatenate([a_t, bsf, swap(bsf), bsb, swap(bsb)], axis=1).astype(BF16)
    r_scr[...] = _nn(ut_ref[0], w1)

    def chain(row0, nb, nchunks, col, lam_t, h, reverse):
        a1, a2s = lam_t
        hs = swap(h)
        order = range(nchunks - 1, -1, -1) if reverse else range(nchunks)
        hcol = slice(LANES, 2 * LANES) if reverse else slice(0, LANES)
        for c in order:
            rows = slice(row0 + c * nb, row0 + (c + 1) * nb)
            hin_scr[rows, hcol] = h
            s = r_scr[rows, col:col + LANES]
            ss = r_scr[rows, col + LANES:col + 2 * LANES]
            h, hs = h * a1 + hs * a2s + s, hs * a1 - h * a2s + ss
        return h

    zero = jnp.zeros((ctx_batch, LANES), F32)
    chain(0, ctx_batch, ctx_chunks, W, lamt_f, zero, False)
    chain(0, ctx_batch, ctx_chunks, W + 2 * LANES, lamt_b, zero, True)
    s_tiled = tile_time(srow)
    first_tok = jnp.where(s_tiled == 0.0, bbar_f, 0.0).astype(BF16)
    last_tok = jnp.where(s_tiled == float(T - 1), bbar_b, 0.0).astype(BF16)
    last_rows = (ctx_chunks - 1) * ctx_batch
    hf_ref[0, :, 0:LANES] = _nn(ut_ref[0, 0:ctx_batch, :], first_tok)
    hf_ref[0, :, LANES:2 * LANES] = _nn(ut_ref[0, last_rows:last_rows + ctx_batch, :], last_tok)
    lat0 = ctx_batch * ctx_chunks
    h0 = h0_ref[0]
    chain(lat0, lat_batch, lat_chunks, W, lamt_f, h0[:, 0:LANES], False)
    chain(lat0, lat_batch, lat_chunks, W + 2 * LANES, lamt_b, h0[:, LANES:2 * LANES], True)

    cc_cat = jnp.concatenate([ccf, ccb], axis=1).astype(BF16)
    yt_ref[0] = r_scr[:, 0:W] + _nt(hin_scr[...].astype(BF16), cc_cat)


def _ssm(ut, h0, p1, pb, ctx_batch, ctx_chunks, lat_batch, lat_chunks):
    g, n, w = ut.shape
    return pl.pallas_call(
        functools.partial(_ssm_kernel, ctx_batch=ctx_batch, ctx_chunks=ctx_chunks,
                          lat_batch=lat_batch, lat_chunks=lat_chunks),
        grid=(g,),
        in_specs=[
            pl.BlockSpec((1, n, w), lambda i: (i, 0, 0)),
            pl.BlockSpec((1, lat_batch, 2 * LANES), lambda i: (i, 0, 0)),
            pl.BlockSpec((1, 8, LANES), lambda i: (i, 0, 0)),
            pl.BlockSpec((1, 8, SSM_CH, LANES), lambda i: (i, 0, 0, 0)),
        ],
        out_specs=[
            pl.BlockSpec((1, n, w), lambda i: (i, 0, 0)),
            pl.BlockSpec((1, ctx_batch, 2 * LANES), lambda i: (i, 0, 0)),
        ],
        out_shape=[jax.ShapeDtypeStruct((g, n, w), F32),
                   jax.ShapeDtypeStruct((g, ctx_batch, 2 * LANES), F32)],
        scratch_shapes=[pltpu.VMEM((n, w + 4 * LANES), F32), pltpu.VMEM((n, 2 * LANES), F32)],
        compiler_params=pltpu.CompilerParams(dimension_semantics=("parallel",),
                                             vmem_limit_bytes=VMEM_LIMIT),
        name="s5_scan",
    )(ut, h0, p1, pb)


def _postmix_kernel(x_ref, oatt_ref, y_ref, u_ref, mod_ref, d_ref, wglu_ref, bglu_ref, wout_ref, g2_ref,
                    x1_ref, h2_ref):
    mod = mod_ref[0]
    gt1 = mod[2:3]
    sh2 = mod[3:4]
    sc2 = mod[4:5]
    for sb in range(x_ref.shape[0] // MIX_SUB):
        rs = slice(sb * MIX_SUB, (sb + 1) * MIX_SUB)
        y = y_ref[rs, :] + d_ref[...] * u_ref[rs, :]
        z = jax.nn.gelu(y)
        gate = _sigmoid(_nn(z.astype(BF16), wglu_ref[0]) + bglu_ref[...])
        o_ssm = (z * gate).astype(BF16)
        mix = _nn(oatt_ref[rs, :], wout_ref[0, 0:ATT_W, :]) + _nn(o_ssm, wout_ref[0, ATT_W:, :])
        x1 = x_ref[rs, :] + gt1 * mix
        x1_ref[rs, :] = x1
        ms = jnp.mean(x1 * x1, axis=-1, keepdims=True)
        h2 = x1 * lax.rsqrt(ms + EPS) * g2_ref[...]
        h2_ref[rs, :] = (h2 * (1.0 + sc2) + sh2).astype(BF16)


def _postmix(x, oatt, y, u, mods_l, d, wglu, bglu, wout, layer, g2, seq_len, mod_base):
    n = x.shape[0]
    tm = MIX_TILE
    per_seq = seq_len // tm
    if mod_base == 0:
        mod_map = lambda i: (0, 0, 0)
    else:
        mod_map = lambda i: (mod_base + i // per_seq, 0, 0)
    row = lambda i: (i, 0)
    const = lambda i: (0, 0)
    return pl.pallas_call(
        _postmix_kernel,
        grid=(n // tm,),
        in_specs=[
            pl.BlockSpec((tm, D_MODEL), row),
            pl.BlockSpec((tm, ATT_W), row),
            pl.BlockSpec((tm, SSM_W), row),
            pl.BlockSpec((tm, SSM_W), row),
            pl.BlockSpec((1, N_MOD, D_MODEL), mod_map),
            pl.BlockSpec((1, SSM_W), const),
            pl.BlockSpec((1, SSM_W, SSM_W), lambda i: (layer, 0, 0)),
            pl.BlockSpec((1, SSM_W), const),
            pl.BlockSpec((1, D_MODEL, D_MODEL), lambda i: (layer, 0, 0)),
            pl.BlockSpec((1, D_MODEL), const),
        ],
        out_specs=[pl.BlockSpec((tm, D_MODEL), row), pl.BlockSpec((tm, D_MODEL), row)],
        out_shape=[jax.ShapeDtypeStruct((n, D_MODEL), F32), jax.ShapeDtypeStruct((n, D_MODEL), BF16)],
        compiler_params=pltpu.CompilerParams(dimension_semantics=("parallel",),
                                             vmem_limit_bytes=VMEM_LIMIT),
        name="postmix",
    )(x, oatt, y, u, mods_l, d, wglu, bglu, wout, g2)


def _ffn_kernel(h2_ref, x1_ref, mod_ref, wup_ref, cw_ref, cb_ref, wd_ref, o_ref, *, seq_len):
    rows = h2_ref.shape[0]
    tf = FFN_SLAB
    gt2 = mod_ref[0][5:6]
    row8 = lax.broadcasted_iota(jnp.int32, (8, 1), 0)
    h2 = h2_ref[...]

    def conv(up, cw, cb):
        prev = pltpu.roll(up, 1, 0)
        nxt = pltpu.roll(up, rows - 1, 0)
        pp, nn = [], []
        for r in range(0, rows, seq_len):
            e = r + seq_len
            pp += [jnp.where(row8 == 0, 0.0, prev[r:r + 8]), prev[r + 8:e]]
            nn += [nxt[r:e - 8], jnp.where(row8 == 7, 0.0, nxt[e - 8:e])]
        prev = jnp.concatenate(pp, axis=0)
        nxt = jnp.concatenate(nn, axis=0)
        return prev * cw[0:1] + up * cw[1:2] + nxt * cw[2:3] + cb

    acts = []
    for j in range(D_FF // tf):
        cv = slice(j * tf, (j + 1) * tf)
        cg = slice(D_FF + j * tf, D_FF + (j + 1) * tf)
        val = conv(_nn(h2, wup_ref[0, :, cv]), cw_ref[:, cv], cb_ref[:, cv])
        gate = conv(_nn(h2, wup_ref[0, :, cg]), cw_ref[:, cg], cb_ref[:, cg])
        acts.append((gate * _sigmoid(gate) * val).astype(BF16))
    act = jnp.concatenate(acts, axis=1)
    o_ref[...] = x1_ref[...] + gt2 * _nn(act, wd_ref[0])


def _ffn(h2, x1, mods_l, wup, cw, cb, wdown, layer, seq_len, mod_base):
    n = h2.shape[0]
    rows = FFN_ROWS
    per_seq = seq_len // rows if seq_len >= rows else 1
    if mod_base == 0:
        mod_map = lambda i: (0, 0, 0)
    else:
        mod_map = lambda i: (mod_base + i // per_seq, 0, 0)
    const = lambda i: (0, 0)
    resident = dict(pipeline_mode=pl.Buffered(1))
    return pl.pallas_call(
        functools.partial(_ffn_kernel, seq_len=seq_len),
        grid=(n // rows,),
        in_specs=[
            pl.BlockSpec((rows, D_MODEL), lambda i: (i, 0)),
            pl.BlockSpec((rows, D_MODEL), lambda i: (i, 0)),
            pl.BlockSpec((1, N_MOD, D_MODEL), mod_map),
            pl.BlockSpec((1, D_MODEL, 2 * D_FF), lambda i: (layer, 0, 0), **resident),
            pl.BlockSpec((3, 2 * D_FF), const),
            pl.BlockSpec((1, 2 * D_FF), const),
            pl.BlockSpec((1, D_FF, D_MODEL), lambda i: (layer, 0, 0), **resident),
        ],
        out_specs=pl.BlockSpec((rows, D_MODEL), lambda i: (i, 0)),
        out_shape=jax.ShapeDtypeStruct((n, D_MODEL), F32),
        compiler_params=pltpu.CompilerParams(dimension_semantics=("parallel",),
                                             vmem_limit_bytes=VMEM_LIMIT),
        name="conv_ffn",
    )(h2, x1, mods_l, wup, cw, cb, wdown)


def _rope_tables(n_tokens):
    pos = np.arange(n_tokens)
    row = (pos // GRID_W).astype(np.float64)
    col = (pos % GRID_W).astype(np.float64)
    nf = ROPE_AXIS // 2
    inv_freq = 1.0 / (ROPE_THETA ** (np.arange(nf, dtype=np.float64) / nf))
    lane = np.arange(LANES)
    freq = inv_freq[lane % nf]
    is_col = (lane % D_SUB) >= ROPE_AXIS
    ang = np.where(is_col[None, :], col[:, None], row[:, None]) * freq[None, :]
    second = (lane % ROPE_AXIS) >= nf
    cos = np.cos(ang)
    sin = np.sin(ang)
    sina = np.where(second[None, :], 0.0, -sin)
    sinb = np.where(second[None, :], sin, 0.0)
    return tuple(jnp.asarray(t, dtype=F32) for t in (cos, sina, sinb))


def _segment_transpose(vs):
    seg = lax.broadcasted_iota(jnp.int32, (1, LANES), 1) // SSM_CH
    vs = list(vs)
    for d in (4, 2, 1):
        keep = (seg & d) == 0
        shift = d * SSM_CH
        for a in range(GROUPS_PER_BLOCK):
            if a & d:
                continue
            lo, hi = vs[a], vs[a + d]
            vs[a] = jnp.where(keep, lo, pltpu.roll(hi, shift, 1))
            vs[a + d] = jnp.where(keep, pltpu.roll(lo, LANES - shift, 1), hi)
    return vs


def _chunk_layout_passes(ctx_batch, ctx_len, lat_batch, lat_len):
    ctx_rows = ctx_batch * (ctx_len // CHUNK)
    return ((0, ctx_batch, ctx_len, 0, BF16_ROWS // ctx_batch if ctx_batch < BF16_ROWS else 1),
            (1, lat_batch, lat_len, ctx_rows, BF16_ROWS // lat_batch if lat_batch < BF16_ROWS else 1))


def _to_chunks_kernel(uc_ref, ul_ref, o_ref, *, passes):
    for idx, batch, seq_len, row0, cstep in passes:
        u_ref = (uc_ref, ul_ref)[idx]

        def body(k, carry, u_ref=u_ref, batch=batch, seq_len=seq_len, row0=row0, cstep=cstep):
            for j in range(CHUNK // GROUPS_PER_BLOCK):
                pieces = [[] for _ in range(GROUPS_PER_BLOCK)]
                for cc in range(cstep):
                    base = (k * cstep + cc) * CHUNK + j * GROUPS_PER_BLOCK
                    vs = [u_ref[pl.ds(base + s, batch, stride=seq_len), :] for s in range(GROUPS_PER_BLOCK)]
                    for g, w in enumerate(_segment_transpose(vs)):
                        pieces[g].append(w)
                start = pl.multiple_of(row0 + k * cstep * batch, BF16_ROWS)
                for g in range(GROUPS_PER_BLOCK):
                    blk = pieces[g][0] if cstep == 1 else jnp.concatenate(pieces[g], axis=0)
                    o_ref[g, pl.ds(start, cstep * batch), j * LANES:(j + 1) * LANES] = blk.astype(o_ref.dtype)
            return carry

        lax.fori_loop(0, seq_len // CHUNK // cstep, body, 0)


def _from_chunks_kernel(y_ref, oc_ref, ol_ref, *, passes):
    for idx, batch, seq_len, row0, _ in passes:
        o_ref = (oc_ref, ol_ref)[idx]

        def body(c, carry, o_ref=o_ref, batch=batch, seq_len=seq_len, row0=row0):
            start = pl.multiple_of(row0 + c * batch, 8)
            for j in range(CHUNK // GROUPS_PER_BLOCK):
                ws = [y_ref[g, pl.ds(start, batch), j * LANES:(j + 1) * LANES] for g in range(GROUPS_PER_BLOCK)]
                base = c * CHUNK + j * GROUPS_PER_BLOCK
                for s, v in enumerate(_segment_transpose(ws)):
                    o_ref[pl.ds(base + s, batch, stride=seq_len), :] = v
            return carry

        lax.fori_loop(0, seq_len // CHUNK, body, 0)


def _to_chunks(uc, ul, ctx_batch, ctx_len, lat_batch, lat_len):
    passes = _chunk_layout_passes(ctx_batch, ctx_len, lat_batch, lat_len)
    rows = ctx_batch * (ctx_len // CHUNK) + lat_batch * (lat_len // CHUNK)
    return pl.pallas_call(
        functools.partial(_to_chunks_kernel, passes=passes),
        grid=(SSM_GROUPS // GROUPS_PER_BLOCK,),
        in_specs=[pl.BlockSpec((uc.shape[0], LANES), lambda j: (0, j)),
                  pl.BlockSpec((ul.shape[0], LANES), lambda j: (0, j))],
        out_specs=pl.BlockSpec((GROUPS_PER_BLOCK, rows, SSM_CH * CHUNK), lambda j: (j, 0, 0)),
        out_shape=jax.ShapeDtypeStruct((SSM_GROUPS, rows, SSM_CH * CHUNK), BF16),
        compiler_params=pltpu.CompilerParams(dimension_semantics=("parallel",),
                                             vmem_limit_bytes=VMEM_LIMIT),
        name="to_chunks",
    )(uc, ul)


def _from_chunks(yt, ctx_batch, ctx_len, lat_batch, lat_len):
    passes = _chunk_layout_passes(ctx_batch, ctx_len, lat_batch, lat_len)
    rows = yt.shape[1]
    n_ctx = ctx_batch * ctx_len
    n_lat = lat_batch * lat_len
    return pl.pallas_call(
        functools.partial(_from_chunks_kernel, passes=passes),
        grid=(SSM_GROUPS // GROUPS_PER_BLOCK,),
        in_specs=[pl.BlockSpec((GROUPS_PER_BLOCK, rows, SSM_CH * CHUNK), lambda j: (j, 0, 0))],
        out_specs=[pl.BlockSpec((n_ctx, LANES), lambda j: (0, j)),
                   pl.BlockSpec((n_lat, LANES), lambda j: (0, j))],
        out_shape=[jax.ShapeDtypeStruct((n_ctx, SSM_W), F32), jax.ShapeDtypeStruct((n_lat, SSM_W), F32)],
        compiler_params=pltpu.CompilerParams(dimension_semantics=("parallel",),
                                             vmem_limit_bytes=VMEM_LIMIT),
        name="from_chunks",
    )(yt)


def _dup_lanes(a):
    return jnp.concatenate([a, a], axis=-1)


def kernel(x_prompt, x_sample, cache_k, cache_v, state_ssm, c, c_ctx, w_mod, b_mod, g_norm1, w_in, q_norm, k_norm, lambda_q1, lambda_k1, lambda_q2, lambda_k2, subln_g, ssm_lambda_re, ssm_lambda_im, ssm_log_step, ssm_b_re, ssm_b_im, ssm_c_re, ssm_c_im, ssm_d, w_glu, b_glu, w_out, g_norm2, w_up, conv_w, conv_b, w_down):
    nb_ctx, len_ctx, _ = x_prompt.shape
    nb_lat, len_lat, _ = x_sample.shape
    past = cache_k.shape[2]

    n_cond = 16
    cond = jnp.zeros((n_cond, D_MODEL), F32).at[0].set(c_ctx).at[1:1 + nb_lat].set(c)
    mods = _modulation(cond, w_mod, b_mod).reshape(DEPTH, n_cond, N_MOD, D_MODEL)

    gidx = np.arange(ATT_W) // D_SUB
    gmat = jnp.asarray(np.where(gidx[:, None] == gidx[None, :], 1.0 / D_SUB, 0.0), dtype=BF16)
    cache_kt = cache_k.transpose(0, 1, 3, 4, 5, 2).reshape(nb_lat, DEPTH, ATT_W, past)
    cache_vr = cache_v.reshape(nb_lat, DEPTH, past * N_HEADS, D_V)
    rope_tabs = _rope_tables(len_lat)
    win_b = w_in.astype(BF16)
    wout_b = w_out.astype(BF16)
    wglu_b = w_glu.astype(BF16)
    wup_b = w_up.astype(BF16)
    wdown_b = w_down.astype(BF16)

    xp = x_prompt.reshape(nb_ctx * len_ctx, D_MODEL)
    xs = x_sample.reshape(nb_lat * len_lat, D_MODEL)
    ks, vs, ss = [], [], []
    for l in range(DEPTH):
        lam_init = 0.8 - 0.6 * math.exp(-0.3 * l)
        mods_l = mods[l]
        g1 = g_norm1[l].reshape(1, D_MODEL)
        g2 = g_norm2[l].reshape(1, D_MODEL)
        qg = jnp.tile(q_norm[l], ATT_W // D_SUB).reshape(1, ATT_W)
        kg = jnp.tile(k_norm[l], ATT_W // D_SUB).reshape(1, ATT_W)
        sg = jnp.tile(subln_g[l], N_HEADS).reshape(1, ATT_W)
        lamp = jnp.stack([lambda_q1[l], lambda_k1[l], lambda_q2[l], lambda_k2[l]])
        d = ssm_d[l].reshape(1, SSM_W)
        bglu = b_glu[l].reshape(1, SSM_W)
        cb = conv_b[l].reshape(1, 2 * D_FF)
        cw = conv_w[l]

        zeros = jnp.zeros((SSM_GROUPS, SSM_P), F32)
        step = jnp.broadcast_to(ssm_log_step[l][:, :, None], (2, SSM_GROUPS, SSM_P))
        p1 = _dup_lanes(jnp.stack([ssm_lambda_re[l, 0], ssm_lambda_im[l, 0], step[0],
                                   ssm_lambda_re[l, 1], ssm_lambda_im[l, 1], step[1], zeros, zeros], axis=1))
        bt = lambda a: a.transpose(0, 2, 1)
        pb = _dup_lanes(jnp.stack([bt(ssm_b_re[l, 0]), bt(ssm_b_im[l, 0]), ssm_c_re[l, 0], ssm_c_im[l, 0],
                                   bt(ssm_b_re[l, 1]), bt(ssm_b_im[l, 1]), ssm_c_re[l, 1], ssm_c_im[l, 1]], axis=1))
        h0 = state_ssm[:, l].transpose(2, 0, 1, 4, 3).reshape(SSM_GROUPS, nb_lat, 4 * SSM_P)

        qc, kc_new, vc_new, uc, kt, vt = _premix(xp, mods_l, g1, win_b, l, gmat, qg, kg, None, len_ctx, 0)
        ql, kl, vl, ul = _premix(xs, mods_l, g1, win_b, l, gmat, qg, kg, rope_tabs, len_lat, 1)

        oc = _attention(qc, kc_new, vc_new, lamp, sg, None, l, nb_ctx, len_ctx, lam_init)
        ol = _attention(ql, kl, vl, lamp, sg, (cache_kt, cache_vr), l, nb_lat, len_lat, lam_init)

        ut = _to_chunks(uc, ul, nb_ctx, len_ctx, nb_lat, len_lat)
        yt, hfin = _ssm(ut, h0, p1, pb, nb_ctx, len_ctx // CHUNK, nb_lat, len_lat // CHUNK)
        yc, yl = _from_chunks(yt, nb_ctx, len_ctx, nb_lat, len_lat)

        x1c, h2c = _postmix(xp, oc, yc, uc, mods_l, d, wglu_b, bglu, wout_b, l, g2, len_ctx, 0)
        x1l, h2l = _postmix(xs, ol, yl, ul, mods_l, d, wglu_b, bglu, wout_b, l, g2, len_lat, 1)

        xp = _ffn(h2c, x1c, mods_l, wup_b, cw, cb, wdown_b, l, len_ctx, 0)
        xs = _ffn(h2l, x1l, mods_l, wup_b, cw, cb, wdown_b, l, len_lat, 1)

        ks.append(kt)
        vs.append(vt.reshape(nb_ctx, len_ctx, N_HEADS, D_V))
        ss.append(hfin.reshape(SSM_GROUPS, nb_ctx, 2, 2, SSM_P).transpose(1, 2, 0, 4, 3))

    new_k = jnp.stack(ks, axis=1).reshape(nb_ctx, DEPTH, N_HEADS, 2, D_SUB, len_ctx).transpose(0, 1, 5, 2, 3, 4)
    return (xp.reshape(nb_ctx, len_ctx, D_MODEL), xs.reshape(nb_lat, len_lat, D_MODEL),
            new_k, jnp.stack(vs, axis=1), jnp.stack(ss, axis=1))
```

```python
import functools
import math

import numpy as np
import jax
import jax.numpy as jnp
from jax import lax
from jax.experimental import pallas as pl
from jax.experimental.pallas import tpu as pltpu

F32 = jnp.float32
BF16 = jnp.bfloat16

D_MODEL = 1024
DEPTH = 2
GRID_W = 64
ATT_W = 512
SSM_W = 512
N_HEADS = 4
D_V = 128
D_SUB = 64
ROPE_AXIS = 32
ROPE_THETA = 10000.0
SSM_CH = 16
SSM_GROUPS = 32
SSM_P = 64
D_FF = 2048
N_MOD = 6
EPS = 1e-6

LANES = 128
BF16_ROWS = 16
GROUPS_PER_BLOCK = LANES // SSM_CH
CHUNK = 32
ATTN_TILE = 1024
ATTN_SUB = 256
MIX_TILE = 512
MIX_SUB = 256
FFN_ROWS = 1024
FFN_SLAB = 512
MOD_COLS = 1536
VMEM_LIMIT = 56 * 1024 * 1024


def _nt(a, b):
    return lax.dot_general(a, b, (((1,), (1,)), ((), ())), preferred_element_type=F32)


def _nn(a, b):
    return jnp.dot(a, b, preferred_element_type=F32)


def _sigmoid(x):
    return 1.0 / (1.0 + jnp.exp(-x))


def _mod_kernel(c_ref, w_ref, b_ref, o_ref):
    c = c_ref[...]
    a = (c * _sigmoid(c)).astype(BF16)
    o_ref[0] = _nn(a, w_ref[0].astype(BF16)) + b_ref[0]


def _modulation(cond, w_mod, b_mod):
    nb = cond.shape[0]
    ncol = N_MOD * D_MODEL
    return pl.pallas_call(
        _mod_kernel,
        grid=(DEPTH, ncol // MOD_COLS),
        in_specs=[
            pl.BlockSpec((nb, D_MODEL), lambda l, j: (0, 0)),
            pl.BlockSpec((1, D_MODEL, MOD_COLS), lambda l, j: (l, 0, j)),
            pl.BlockSpec((1, 1, MOD_COLS), lambda l, j: (l, 0, j)),
        ],
        out_specs=pl.BlockSpec((1, nb, MOD_COLS), lambda l, j: (l, 0, j)),
        out_shape=jax.ShapeDtypeStruct((DEPTH, nb, ncol), F32),
        compiler_params=pltpu.CompilerParams(vmem_limit_bytes=VMEM_LIMIT),
        name="adaln_mod",
    )(cond, w_mod, b_mod.reshape(DEPTH, 1, ncol))


def _premix_kernel(x_ref, mod_ref, g1_ref, win_ref, gmat_ref, qg_ref, kg_ref, *rest, rope):
    kt_ref = vt_ref = None
    if rope:
        cos_ref, sina_ref, sinb_ref, q_ref, k_ref, v_ref, u_ref = rest
    else:
        q_ref, k_ref, v_ref, u_ref, kt_ref, vt_ref = rest
    mod = mod_ref[0]
    sh1 = mod[0:1]
    sc1 = mod[1:2]

    def head_norm(t, g):
        msq = _nn((t * t).astype(BF16), gmat_ref[...])
        return t * lax.rsqrt(msq + EPS) * g

    def rotate(t, rs):
        outs = []
        for j in range(ATT_W // LANES):
            s = t[:, j * LANES:(j + 1) * LANES]
            outs.append(s * cos_ref[rs, :] + pltpu.roll(s, LANES - 16, 1) * sina_ref[rs, :]
                        + pltpu.roll(s, 16, 1) * sinb_ref[rs, :])
        return jnp.concatenate(outs, axis=1)

    for sb in range(x_ref.shape[0] // MIX_SUB):
        rs = slice(sb * MIX_SUB, (sb + 1) * MIX_SUB)
        x = x_ref[rs, :]
        ms = jnp.mean(x * x, axis=-1, keepdims=True)
        h = x * lax.rsqrt(ms + EPS) * g1_ref[...]
        h = h * (1.0 + sc1) + sh1
        qkvu = _nn(h.astype(BF16), win_ref[0])
        q = head_norm(qkvu[:, 0:ATT_W], qg_ref[...])
        k = head_norm(qkvu[:, ATT_W:2 * ATT_W], kg_ref[...])
        if rope:
            q = rotate(q, rs)
            k = rotate(k, rs)
        q_ref[rs, :] = (q * (D_SUB ** -0.5)).astype(q_ref.dtype)
        k_ref[rs, :] = k.astype(k_ref.dtype)
        v = qkvu[:, 2 * ATT_W:3 * ATT_W]
        if kt_ref is not None:
            kt_ref[sb] = k.T
            for hd in range(N_HEADS):
                vt_ref[pl.ds(sb * MIX_SUB * N_HEADS + hd, MIX_SUB, stride=N_HEADS), :] = (
                    v[:, hd * D_V:(hd + 1) * D_V])
        v_ref[rs, :] = v.astype(v_ref.dtype)
        u_ref[rs, :] = qkvu[:, 3 * ATT_W:].astype(u_ref.dtype)


def _premix(x, mods_l, g1, win, layer, gmat, qg, kg, rope_tabs, seq_len, mod_base):
    n = x.shape[0]
    tm = MIX_TILE
    per_seq = seq_len // tm
    rope = rope_tabs is not None
    assert rope or seq_len == MIX_SUB
    if mod_base == 0:
        mod_map = lambda i: (0, 0, 0)
    else:
        mod_map = lambda i: (mod_base + i // per_seq, 0, 0)
    row = lambda i: (i, 0)
    const = lambda i: (0, 0)
    in_specs = [
        pl.BlockSpec((tm, D_MODEL), row),
        pl.BlockSpec((1, N_MOD, D_MODEL), mod_map),
        pl.BlockSpec((1, D_MODEL), const),
        pl.BlockSpec((1, D_MODEL, 4 * ATT_W), lambda i: (layer, 0, 0)),
        pl.BlockSpec((ATT_W, ATT_W), const),
        pl.BlockSpec((1, ATT_W), const),
        pl.BlockSpec((1, ATT_W), const),
    ]
    args = [x, mods_l, g1, win, gmat, qg, kg]
    if rope:
        tab = lambda i: (i % per_seq, 0)
        in_specs += [pl.BlockSpec((tm, LANES), tab)] * 3
        args += list(rope_tabs)
    out = pl.BlockSpec((tm, ATT_W), row)
    out_specs = [out, out, out, out]
    out_shape = [jax.ShapeDtypeStruct((n, ATT_W), BF16),
                 jax.ShapeDtypeStruct((n, ATT_W), BF16),
                 jax.ShapeDtypeStruct((n, ATT_W), BF16),
                 jax.ShapeDtypeStruct((n, SSM_W), F32)]
    if not rope:
        out_specs.append(pl.BlockSpec((tm // seq_len, ATT_W, seq_len), lambda i: (i, 0, 0)))
        out_shape.append(jax.ShapeDtypeStruct((n // seq_len, ATT_W, seq_len), F32))
        out_specs.append(pl.BlockSpec((tm * N_HEADS, D_V), lambda i: (i, 0)))
        out_shape.append(jax.ShapeDtypeStruct((n * N_HEADS, D_V), F32))
    return pl.pallas_call(
        functools.partial(_premix_kernel, rope=rope),
        grid=(n // tm,),
        in_specs=in_specs,
        out_specs=out_specs,
        out_shape=out_shape,
        compiler_params=pltpu.CompilerParams(dimension_semantics=("parallel",),
                                             vmem_limit_bytes=VMEM_LIMIT),
        name="premix_rope" if rope else "premix",
    )(*args)


def _attn_kernel(q_ref, k_ref, v_ref, lamp_ref, sg_ref, *rest, lam_init, has_cache):
    if has_cache:
        kc_ref, vc_ref, o_ref = rest
    else:
        (o_ref,) = rest
    tq = q_ref.shape[0]
    lp = lamp_ref[...]
    l1 = jnp.sum(lp[0:1] * lp[1:2], axis=-1, keepdims=True)
    l2 = jnp.sum(lp[2:3] * lp[3:4], axis=-1, keepdims=True)
    lam = jnp.exp(l1) - jnp.exp(l2) + lam_init
    first = lax.broadcasted_iota(jnp.int32, (1, LANES), 1) < D_SUB
    tsub = min(ATTN_SUB, tq)
    for h in range(N_HEADS):
        sl = slice(h * D_V, (h + 1) * D_V)
        kn = k_ref[:, sl]
        vn = v_ref[:, sl]
        if has_cache:
            ncache = kc_ref.shape[3]
            kct = kc_ref[0, 0, sl, :].astype(BF16)
            vct = vc_ref[0, 0, pl.ds(h, ncache, stride=N_HEADS), :].astype(BF16)
        for r0 in range(0, tq, tsub):
            qh = q_ref[r0:r0 + tsub, sl]
            zero = jnp.zeros_like(qh)
            qq = jnp.concatenate([jnp.where(first, qh, zero), jnp.where(first, zero, qh)], axis=0)
            if has_cache:
                sc = jnp.concatenate([_nn(qq, kct), _nt(qq, kn)], axis=1)
            else:
                sc = _nt(qq, kn)
            m = jnp.max(sc, axis=-1, keepdims=True)
            e = jnp.exp(sc - m)
            ssum = jnp.sum(e, axis=-1, keepdims=True)
            eb = e.astype(BF16)
            if has_cache:
                pv = _nn(eb[:, :ncache], vct) + _nn(eb[:, ncache:], vn)
            else:
                pv = _nn(eb, vn)
            on = pv * (1.0 / ssum)
            o = on[:tsub] - lam * on[tsub:]
            ms = jnp.mean(o * o, axis=-1, keepdims=True)
            o_ref[r0:r0 + tsub, sl] = (o * lax.rsqrt(ms + EPS) * sg_ref[:, sl]
                                       * (1.0 - lam_init)).astype(o_ref.dtype)


def _attention(q, k, v, lamp, sg, cache, layer, batch, seq_len, lam_init):
    n = q.shape[0]
    tq = min(ATTN_TILE, seq_len)
    nq = seq_len // tq
    has_cache = cache is not None
    in_specs = [
        pl.BlockSpec((tq, ATT_W), lambda b, i: (b * nq + i, 0)),
        pl.BlockSpec((seq_len, ATT_W), lambda b, i: (b, 0)),
        pl.BlockSpec((seq_len, ATT_W), lambda b, i: (b, 0)),
        pl.BlockSpec((4, D_SUB), lambda b, i: (0, 0)),
        pl.BlockSpec((1, ATT_W), lambda b, i: (0, 0)),
    ]
    args = [q, k, v, lamp, sg]
    if has_cache:
        past = cache[0].shape[3]
        in_specs += [pl.BlockSpec((1, 1, ATT_W, past), lambda b, i: (b, layer, 0, 0)),
                     pl.BlockSpec((1, 1, past * N_HEADS, D_V), lambda b, i: (b, layer, 0, 0))]
        args += list(cache)
    return pl.pallas_call(
        functools.partial(_attn_kernel, lam_init=lam_init, has_cache=has_cache),
        grid=(batch, nq),
        in_specs=in_specs,
        out_specs=pl.BlockSpec((tq, ATT_W), lambda b, i: (b * nq + i, 0)),
        out_shape=jax.ShapeDtypeStruct((n, ATT_W), BF16),
        compiler_params=pltpu.CompilerParams(dimension_semantics=("parallel", "parallel"),
                                             vmem_limit_bytes=VMEM_LIMIT),
        name="attn_cache" if has_cache else "attn",
    )(*args)


def _ssm_kernel(ut_ref, h0_ref, p1_ref, pb_ref, yt_ref, hf_ref, r_scr, hin_scr, *,
                ctx_batch, ctx_chunks, lat_batch, lat_chunks):
    T = CHUNK
    W = SSM_CH * T
    lane = lax.broadcasted_iota(jnp.int32, (1, LANES), 1)
    lo = lane < SSM_P
    sgn_lr = jnp.where(lo, 1.0, -1.0).astype(F32)
    sgn_rl = -sgn_lr
    srow = lax.broadcasted_iota(jnp.int32, (T, LANES), 0).astype(F32)

    def swap(a):
        return pltpu.roll(a, SSM_P, 1)

    def tile_time(a):
        return jnp.concatenate([jnp.broadcast_to(a[i:i + 1, :], (SSM_CH, LANES)) for i in range(T)], axis=0)

    def rep_chan(a):
        return jnp.concatenate([a] * T, axis=0)

    def pair_pack(y):
        ys = swap(y)
        return jnp.where(lo, y, ys), jnp.where(lo, ys, y) * sgn_rl

    def cmul_const(x, yr2, yi2s):
        return x * yr2 + swap(x) * yi2s

    p1 = p1_ref[0]
    pb = pb_ref[0]

    def direction(d):
        lre2 = p1[3 * d + 0:3 * d + 1]
        lim2 = p1[3 * d + 1:3 * d + 2]
        ls2 = p1[3 * d + 2:3 * d + 3]
        bre2, bim2, cre2, cim2 = pb[4 * d + 0], pb[4 * d + 1], pb[4 * d + 2], pb[4 * d + 3]
        delta = jnp.exp(ls2)
        xr = lre2 * delta
        th = lim2 * delta

        def cpow_const(k):
            return jnp.exp(k * xr) * jnp.where(lo, jnp.cos(k * th), jnp.sin(k * th))

        mag1 = jnp.exp(xr)
        lbr2 = mag1 * jnp.cos(th)
        lbi2 = mag1 * jnp.sin(th)
        den = lre2 * lre2 + lim2 * lim2
        nr = lbr2 - 1.0
        cr2 = (nr * lre2 + lbi2 * lim2) / den
        ci2 = (lbi2 * lre2 - nr * lim2) / den
        bbr2 = rep_chan(cr2 * bre2 - ci2 * bim2)
        bbi2s = rep_chan((cr2 * bim2 + ci2 * bre2) * sgn_rl)
        ccr2 = rep_chan(cre2)
        cci2s = rep_chan(cim2 * sgn_rl)

        ang = srow * th
        cs = jnp.where(lo, jnp.cos(ang), jnp.sin(ang))
        wp = jnp.exp(srow * xr) * cs
        wn = jnp.exp(-(srow * xr)) * cs * sgn_lr

        def times(x, c1, c2s):
            return tile_time(x) * c1 + tile_time(swap(x)) * c2s

        lam_t = pair_pack(cpow_const(float(T)))
        if d == 0:
            left = times(wn, bbr2, bbi2s)
            bs = times(cmul_const(wn, *pair_pack(cpow_const(float(T - 1)))), bbr2, bbi2s)
            right = times(wp, ccr2, cci2s)
            cc = times(cmul_const(wp, *pair_pack(cpow_const(1.0))), ccr2, cci2s)
        else:
            left = times(wp, bbr2, bbi2s)
            bs = left
            right = times(wn, ccr2, cci2s)
            cc = times(cmul_const(wn, *lam_t), ccr2, cci2s)
        bbar = jnp.where(lo, bbr2, bbi2s)
        return left * sgn_lr, right, bs, cc * sgn_lr, lam_t, bbar

    lf, rf, bsf, ccf, lamt_f, bbar_f = direction(0)
    lb, rb, bsb, ccb, lamt_b, bbar_b = direction(1)

    s_idx = lax.broadcasted_iota(jnp.int32, (W, W), 0) // SSM_CH
    t_idx = lax.broadcasted_iota(jnp.int32, (W, W), 1) // SSM_CH
    a_t = (jnp.where(t_idx >= s_idx, _nt(lf.astype(BF16), rf.astype(BF16)), 0.0)
           + jnp.where(s_idx >= t_idx, _nt(lb.astype(BF16), rb.astype(BF16)), 0.0))
    w1 = jnp.concatenate([a_t, bsf, swap(bsf), bsb, swap(bsb)], axis=1).astype(BF16)
    r_scr[...] = _nn(ut_ref[0], w1)

    def chain(row0, nb, nchunks, col, lam_t, h, reverse):
        a1, a2s = lam_t
        hs = swap(h)
        order = range(nchunks - 1, -1, -1) if reverse else range(nchunks)
        hcol = slice(LANES, 2 * LANES) if reverse else slice(0, LANES)
        for c in order:
            rows = slice(row0 + c * nb, row0 + (c + 1) * nb)
            hin_scr[rows, hcol] = h
            s = r_scr[rows, col:col + LANES]
            ss = r_scr[rows, col + LANES:col + 2 * LANES]
            h, hs = h * a1 + hs * a2s + s, hs * a1 - h * a2s + ss
        return h

    zero = jnp.zeros((ctx_batch, LANES), F32)
    chain(0, ctx_batch, ctx_chunks, W, lamt_f, zero, False)
    chain(0, ctx_batch, ctx_chunks, W + 2 * LANES, lamt_b, zero, True)
    s_tiled = tile_time(srow)
    first_tok = jnp.where(s_tiled == 0.0, bbar_f, 0.0).astype(BF16)
    last_tok = jnp.where(s_tiled == float(T - 1), bbar_b, 0.0).astype(BF16)
    last_rows = (ctx_chunks - 1) * ctx_batch
    hf_ref[0, :, 0:LANES] = _nn(ut_ref[0, 0:ctx_batch, :], first_tok)
    hf_ref[0, :, LANES:2 * LANES] = _nn(ut_ref[0, last_rows:last_rows + ctx_batch, :], last_tok)
    lat0 = ctx_batch * ctx_chunks
    h0 = h0_ref[0]
    chain(lat0, lat_batch, lat_chunks, W, lamt_f, h0[:, 0:LANES], False)
    chain(lat0, lat_batch, lat_chunks, W + 2 * LANES, lamt_b, h0[:, LANES:2 * LANES], True)

    cc_cat = jnp.concatenate([ccf, ccb], axis=1).astype(BF16)
    yt_ref[0] = r_scr[:, 0:W] + _nt(hin_scr[...].astype(BF16), cc_cat)


def _ssm(ut, h0, p1, pb, ctx_batch, ctx_chunks, lat_batch, lat_chunks):
    g, n, w = ut.shape
    return pl.pallas_call(
        functools.partial(_ssm_kernel, ctx_batch=ctx_batch, ctx_chunks=ctx_chunks,
                          lat_batch=lat_batch, lat_chunks=lat_chunks),
        grid=(g,),
        in_specs=[
            pl.BlockSpec((1, n, w), lambda i: (i, 0, 0)),
            pl.BlockSpec((1, lat_batch, 2 * LANES), lambda i: (i, 0, 0)),
            pl.BlockSpec((1, 8, LANES), lambda i: (i, 0, 0)),
            pl.BlockSpec((1, 8, SSM_CH, LANES), lambda i: (i, 0, 0, 0)),
        ],
        out_specs=[
            pl.BlockSpec((1, n, w), lambda i: (i, 0, 0)),
            pl.BlockSpec((1, ctx_batch, 2 * LANES), lambda i: (i, 0, 0)),
        ],
        out_shape=[jax.ShapeDtypeStruct((g, n, w), F32),
                   jax.ShapeDtypeStruct((g, ctx_batch, 2 * LANES), F32)],
        scratch_shapes=[pltpu.VMEM((n, w + 4 * LANES), F32), pltpu.VMEM((n, 2 * LANES), F32)],
        compiler_params=pltpu.CompilerParams(dimension_semantics=("parallel",),
                                             vmem_limit_bytes=VMEM_LIMIT),
        name="s5_scan",
    )(ut, h0, p1, pb)


def _tail_kernel(x_ref, oatt_ref, y_ref, u_ref, mod_ref, d_ref, wglu_ref, bglu_ref, wout_ref, g2_ref,
                 wup_ref, cw_ref, cb_ref, wd_ref, o_ref, x1_scr, h2_scr, *, seq_len):
    rows = x_ref.shape[0]
    tf = FFN_SLAB
    mod = mod_ref[0]
    gt1 = mod[2:3]
    sh2 = mod[3:4]
    sc2 = mod[4:5]
    gt2 = mod[5:6]
    for sb in range(rows // MIX_SUB):
        rs = slice(sb * MIX_SUB, (sb + 1) * MIX_SUB)
        y = y_ref[rs, :] + d_ref[...] * u_ref[rs, :]
        z = jax.nn.gelu(y)
        gate = _sigmoid(_nn(z.astype(BF16), wglu_ref[0]) + bglu_ref[...])
        o_ssm = (z * gate).astype(BF16)
        mix = _nn(oatt_ref[rs, :], wout_ref[0, 0:ATT_W, :]) + _nn(o_ssm, wout_ref[0, ATT_W:, :])
        x1 = x_ref[rs, :] + gt1 * mix
        x1_scr[rs, :] = x1
        ms = jnp.mean(x1 * x1, axis=-1, keepdims=True)
        h2 = x1 * lax.rsqrt(ms + EPS) * g2_ref[...]
        h2_scr[rs, :] = (h2 * (1.0 + sc2) + sh2).astype(BF16)

    row8 = lax.broadcasted_iota(jnp.int32, (8, 1), 0)
    h2 = h2_scr[...]

    def conv(up, cw, cb):
        prev = pltpu.roll(up, 1, 0)
        nxt = pltpu.roll(up, rows - 1, 0)
        pp, nn = [], []
        for r in range(0, rows, seq_len):
            e = r + seq_len
            pp += [jnp.where(row8 == 0, 0.0, prev[r:r + 8]), prev[r + 8:e]]
            nn += [nxt[r:e - 8], jnp.where(row8 == 7, 0.0, nxt[e - 8:e])]
        prev = jnp.concatenate(pp, axis=0)
        nxt = jnp.concatenate(nn, axis=0)
        return prev * cw[0:1] + up * cw[1:2] + nxt * cw[2:3] + cb

    acts = []
    for j in range(D_FF // tf):
        cv = slice(j * tf, (j + 1) * tf)
        cg = slice(D_FF + j * tf, D_FF + (j + 1) * tf)
        val = conv(_nn(h2, wup_ref[0, :, cv]), cw_ref[:, cv], cb_ref[:, cv])
        gate = conv(_nn(h2, wup_ref[0, :, cg]), cw_ref[:, cg], cb_ref[:, cg])
        acts.append((gate * _sigmoid(gate) * val).astype(BF16))
    act = jnp.concatenate(acts, axis=1)
    o_ref[...] = x1_scr[...] + gt2 * _nn(act, wd_ref[0])


def _tail(x, oatt, y, u, mods_l, d, wglu, bglu, wout, g2, wup, cw, cb, wdown, layer, seq_len, mod_base):
    n = x.shape[0]
    rows = FFN_ROWS
    per_seq = seq_len // rows if seq_len >= rows else 1
    if mod_base == 0:
        mod_map = lambda i: (0, 0, 0)
    else:
        mod_map = lambda i: (mod_base + i // per_seq, 0, 0)
    row = lambda i: (i, 0)
    const = lambda i: (0, 0)
    lay = lambda i: (layer, 0, 0)
    resident = dict(pipeline_mode=pl.Buffered(1))
    return pl.pallas_call(
        functools.partial(_tail_kernel, seq_len=seq_len),
        grid=(n // rows,),
        in_specs=[
            pl.BlockSpec((rows, D_MODEL), row),
            pl.BlockSpec((rows, ATT_W), row),
            pl.BlockSpec((rows, SSM_W), row),
            pl.BlockSpec((rows, SSM_W), row),
            pl.BlockSpec((1, N_MOD, D_MODEL), mod_map),
            pl.BlockSpec((1, SSM_W), const),
            pl.BlockSpec((1, SSM_W, SSM_W), lay, **resident),
            pl.BlockSpec((1, SSM_W), const),
            pl.BlockSpec((1, D_MODEL, D_MODEL), lay, **resident),
            pl.BlockSpec((1, D_MODEL), const),
            pl.BlockSpec((1, D_MODEL, 2 * D_FF), lay, **resident),
            pl.BlockSpec((3, 2 * D_FF), const),
            pl.BlockSpec((1, 2 * D_FF), const),
            pl.BlockSpec((1, D_FF, D_MODEL), lay, **resident),
        ],
        out_specs=pl.BlockSpec((rows, D_MODEL), row),
        out_shape=jax.ShapeDtypeStruct((n, D_MODEL), F32),
        scratch_shapes=[pltpu.VMEM((rows, D_MODEL), F32), pltpu.VMEM((rows, D_MODEL), BF16)],
        compiler_params=pltpu.CompilerParams(dimension_semantics=("parallel",),
                                             vmem_limit_bytes=VMEM_LIMIT),
        name="layer_tail",
    )(x, oatt, y, u, mods_l, d, wglu, bglu, wout, g2, wup, cw, cb, wdown)


def _rope_tables(n_tokens):
    pos = np.arange(n_tokens)
    row = (pos // GRID_W).astype(np.float64)
    col = (pos % GRID_W).astype(np.float64)
    nf = ROPE_AXIS // 2
    inv_freq = 1.0 / (ROPE_THETA ** (np.arange(nf, dtype=np.float64) / nf))
    lane = np.arange(LANES)
    freq = inv_freq[lane % nf]
    is_col = (lane % D_SUB) >= ROPE_AXIS
    ang = np.where(is_col[None, :], col[:, None], row[:, None]) * freq[None, :]
    second = (lane % ROPE_AXIS) >= nf
    cos = np.cos(ang)
    sin = np.sin(ang)
    sina = np.where(second[None, :], 0.0, -sin)
    sinb = np.where(second[None, :], sin, 0.0)
    return tuple(jnp.asarray(t, dtype=F32) for t in (cos, sina, sinb))


def _segment_transpose(vs):
    seg = lax.broadcasted_iota(jnp.int32, (1, LANES), 1) // SSM_CH
    vs = list(vs)
    for d in (4, 2, 1):
        keep = (seg & d) == 0
        shift = d * SSM_CH
        for a in range(GROUPS_PER_BLOCK):
            if a & d:
                continue
            lo, hi = vs[a], vs[a + d]
            vs[a] = jnp.where(keep, lo, pltpu.roll(hi, shift, 1))
            vs[a + d] = jnp.where(keep, pltpu.roll(lo, LANES - shift, 1), hi)
    return vs


def _chunk_layout_passes(ctx_batch, ctx_len, lat_batch, lat_len):
    ctx_rows = ctx_batch * (ctx_len // CHUNK)
    return ((0, ctx_batch, ctx_len, 0, BF16_ROWS // ctx_batch if ctx_batch < BF16_ROWS else 1),
            (1, lat_batch, lat_len, ctx_rows, BF16_ROWS // lat_batch if lat_batch < BF16_ROWS else 1))


def _to_chunks_kernel(uc_ref, ul_ref, o_ref, *, passes):
    for idx, batch, seq_len, row0, cstep in passes:
        u_ref = (uc_ref, ul_ref)[idx]

        def body(k, carry, u_ref=u_ref, batch=batch, seq_len=seq_len, row0=row0, cstep=cstep):
            for j in range(CHUNK // GROUPS_PER_BLOCK):
                pieces = [[] for _ in range(GROUPS_PER_BLOCK)]
                for cc in range(cstep):
                    base = (k * cstep + cc) * CHUNK + j * GROUPS_PER_BLOCK
                    vs = [u_ref[pl.ds(base + s, batch, stride=seq_len), :] for s in range(GROUPS_PER_BLOCK)]
                    for g, w in enumerate(_segment_transpose(vs)):
                        pieces[g].append(w)
                start = pl.multiple_of(row0 + k * cstep * batch, BF16_ROWS)
                for g in range(GROUPS_PER_BLOCK):
                    blk = pieces[g][0] if cstep == 1 else jnp.concatenate(pieces[g], axis=0)
                    o_ref[g, pl.ds(start, cstep * batch), j * LANES:(j + 1) * LANES] = blk.astype(o_ref.dtype)
            return carry

        lax.fori_loop(0, seq_len // CHUNK // cstep, body, 0)


def _from_chunks_kernel(y_ref, oc_ref, ol_ref, *, passes):
    for idx, batch, seq_len, row0, _ in passes:
        o_ref = (oc_ref, ol_ref)[idx]

        def body(c, carry, o_ref=o_ref, batch=batch, seq_len=seq_len, row0=row0):
            start = pl.multiple_of(row0 + c * batch, 8)
            for j in range(CHUNK // GROUPS_PER_BLOCK):
                ws = [y_ref[g, pl.ds(start, batch), j * LANES:(j + 1) * LANES] for g in range(GROUPS_PER_BLOCK)]
                base = c * CHUNK + j * GROUPS_PER_BLOCK
                for s, v in enumerate(_segment_transpose(ws)):
                    o_ref[pl.ds(base + s, batch, stride=seq_len), :] = v
            return carry

        lax.fori_loop(0, seq_len // CHUNK, body, 0)


def _to_chunks(uc, ul, ctx_batch, ctx_len, lat_batch, lat_len):
    passes = _chunk_layout_passes(ctx_batch, ctx_len, lat_batch, lat_len)
    rows = ctx_batch * (ctx_len // CHUNK) + lat_batch * (lat_len // CHUNK)
    return pl.pallas_call(
        functools.partial(_to_chunks_kernel, passes=passes),
        grid=(SSM_GROUPS // GROUPS_PER_BLOCK,),
        in_specs=[pl.BlockSpec((uc.shape[0], LANES), lambda j: (0, j)),
                  pl.BlockSpec((ul.shape[0], LANES), lambda j: (0, j))],
        out_specs=pl.BlockSpec((GROUPS_PER_BLOCK, rows, SSM_CH * CHUNK), lambda j: (j, 0, 0)),
        out_shape=jax.ShapeDtypeStruct((SSM_GROUPS, rows, SSM_CH * CHUNK), BF16),
        compiler_params=pltpu.CompilerParams(dimension_semantics=("parallel",),
                                             vmem_limit_bytes=VMEM_LIMIT),
        name="to_chunks",
    )(uc, ul)


def _from_chunks(yt, ctx_batch, ctx_len, lat_batch, lat_len):
    passes = _chunk_layout_passes(ctx_batch, ctx_len, lat_batch, lat_len)
    rows = yt.shape[1]
    n_ctx = ctx_batch * ctx_len
    n_lat = lat_batch * lat_len
    return pl.pallas_call(
        functools.partial(_from_chunks_kernel, passes=passes),
        grid=(SSM_GROUPS // GROUPS_PER_BLOCK,),
        in_specs=[pl.BlockSpec((GROUPS_PER_BLOCK, rows, SSM_CH * CHUNK), lambda j: (j, 0, 0))],
        out_specs=[pl.BlockSpec((n_ctx, LANES), lambda j: (0, j)),
                   pl.BlockSpec((n_lat, LANES), lambda j: (0, j))],
        out_shape=[jax.ShapeDtypeStruct((n_ctx, SSM_W), F32), jax.ShapeDtypeStruct((n_lat, SSM_W), F32)],
        compiler_params=pltpu.CompilerParams(dimension_semantics=("parallel",),
                                             vmem_limit_bytes=VMEM_LIMIT),
        name="from_chunks",
    )(yt)


def _dup_lanes(a):
    return jnp.concatenate([a, a], axis=-1)


def kernel(x_prompt, x_sample, cache_k, cache_v, state_ssm, c, c_ctx, w_mod, b_mod, g_norm1, w_in, q_norm, k_norm, lambda_q1, lambda_k1, lambda_q2, lambda_k2, subln_g, ssm_lambda_re, ssm_lambda_im, ssm_log_step, ssm_b_re, ssm_b_im, ssm_c_re, ssm_c_im, ssm_d, w_glu, b_glu, w_out, g_norm2, w_up, conv_w, conv_b, w_down):
    nb_ctx, len_ctx, _ = x_prompt.shape
    nb_lat, len_lat, _ = x_sample.shape
    past = cache_k.shape[2]

    n_cond = 16
    cond = jnp.zeros((n_cond, D_MODEL), F32).at[0].set(c_ctx).at[1:1 + nb_lat].set(c)
    mods = _modulation(cond, w_mod, b_mod).reshape(DEPTH, n_cond, N_MOD, D_MODEL)

    gidx = np.arange(ATT_W) // D_SUB
    gmat = jnp.asarray(np.where(gidx[:, None] == gidx[None, :], 1.0 / D_SUB, 0.0), dtype=BF16)
    cache_kt = cache_k.transpose(0, 1, 3, 4, 5, 2).reshape(nb_lat, DEPTH, ATT_W, past)
    cache_vr = cache_v.reshape(nb_lat, DEPTH, past * N_HEADS, D_V)
    rope_tabs = _rope_tables(len_lat)
    win_b = w_in.astype(BF16)
    wout_b = w_out.astype(BF16)
    wglu_b = w_glu.astype(BF16)
    wup_b = w_up.astype(BF16)
    wdown_b = w_down.astype(BF16)

    xp = x_prompt.reshape(nb_ctx * len_ctx, D_MODEL)
    xs = x_sample.reshape(nb_lat * len_lat, D_MODEL)
    ks, vs, ss = [], [], []
    for l in range(DEPTH):
        lam_init = 0.8 - 0.6 * math.exp(-0.3 * l)
        mods_l = mods[l]
        g1 = g_norm1[l].reshape(1, D_MODEL)
        g2 = g_norm2[l].reshape(1, D_MODEL)
        qg = jnp.tile(q_norm[l], ATT_W // D_SUB).reshape(1, ATT_W)
        kg = jnp.tile(k_norm[l], ATT_W // D_SUB).reshape(1, ATT_W)
        sg = jnp.tile(subln_g[l], N_HEADS).reshape(1, ATT_W)
        lamp = jnp.stack([lambda_q1[l], lambda_k1[l], lambda_q2[l], lambda_k2[l]])
        d = ssm_d[l].reshape(1, SSM_W)
        bglu = b_glu[l].reshape(1, SSM_W)
        cb = conv_b[l].reshape(1, 2 * D_FF)
        cw = conv_w[l]

        zeros = jnp.zeros((SSM_GROUPS, SSM_P), F32)
        step = jnp.broadcast_to(ssm_log_step[l][:, :, None], (2, SSM_GROUPS, SSM_P))
        p1 = _dup_lanes(jnp.stack([ssm_lambda_re[l, 0], ssm_lambda_im[l, 0], step[0],
                                   ssm_lambda_re[l, 1], ssm_lambda_im[l, 1], step[1], zeros, zeros], axis=1))
        bt = lambda a: a.transpose(0, 2, 1)
        pb = _dup_lanes(jnp.stack([bt(ssm_b_re[l, 0]), bt(ssm_b_im[l, 0]), ssm_c_re[l, 0], ssm_c_im[l, 0],
                                   bt(ssm_b_re[l, 1]), bt(ssm_b_im[l, 1]), ssm_c_re[l, 1], ssm_c_im[l, 1]], axis=1))
        h0 = state_ssm[:, l].transpose(2, 0, 1, 4, 3).reshape(SSM_GROUPS, nb_lat, 4 * SSM_P)

        qc, kc_new, vc_new, uc, kt, vt = _premix(xp, mods_l, g1, win_b, l, gmat, qg, kg, None, len_ctx, 0)
        ql, kl, vl, ul = _premix(xs, mods_l, g1, win_b, l, gmat, qg, kg, rope_tabs, len_lat, 1)

        oc = _attention(qc, kc_new, vc_new, lamp, sg, None, l, nb_ctx, len_ctx, lam_init)
        ol = _attention(ql, kl, vl, lamp, sg, (cache_kt, cache_vr), l, nb_lat, len_lat, lam_init)

        ut = _to_chunks(uc, ul, nb_ctx, len_ctx, nb_lat, len_lat)
        yt, hfin = _ssm(ut, h0, p1, pb, nb_ctx, len_ctx // CHUNK, nb_lat, len_lat // CHUNK)
        yc, yl = _from_chunks(yt, nb_ctx, len_ctx, nb_lat, len_lat)

        xp = _tail(xp, oc, yc, uc, mods_l, d, wglu_b, bglu, wout_b, g2, wup_b, cw, cb, wdown_b, l, len_ctx, 0)
        xs = _tail(xs, ol, yl, ul, mods_l, d, wglu_b, bglu, wout_b, g2, wup_b, cw, cb, wdown_b, l, len_lat, 1)

        ks.append(kt)
        vs.append(vt.reshape(nb_ctx, len_ctx, N_HEADS, D_V))
        ss.append(hfin.reshape(SSM_GROUPS, nb_ctx, 2, 2, SSM_P).transpose(1, 2, 0, 4, 3))

    new_k = jnp.stack(ks, axis=1).reshape(nb_ctx, DEPTH, N_HEADS, 2, D_SUB, len_ctx).transpose(0, 1, 5, 2, 3, 4)
    return (xp.reshape(nb_ctx, len_ctx, D_MODEL), xs.reshape(nb_lat, len_lat, D_MODEL),
            new_k, jnp.stack(vs, axis=1), jnp.stack(ss, axis=1))
```

```python
import functools
import math

import numpy as np
import jax
import jax.numpy as jnp
from jax import lax
from jax.experimental import pallas as pl
from jax.experimental.pallas import tpu as pltpu

F32 = jnp.float32
BF16 = jnp.bfloat16

D_MODEL = 1024
DEPTH = 2
GRID_W = 64
ATT_W = 512
SSM_W = 512
N_HEADS = 4
D_V = 128
D_SUB = 64
ROPE_AXIS = 32
ROPE_THETA = 10000.0
SSM_CH = 16
SSM_GROUPS = 32
SSM_P = 64
D_FF = 2048
N_MOD = 6
EPS = 1e-6

LANES = 128
GROUPS_PER_BLOCK = LANES // SSM_CH
CHUNK = 32
ATTN_TILE = 1024
ATTN_SUB = 256
MIX_TILE = 512
MIX_SUB = 256
FFN_ROWS = 1024
FFN_SLAB = 512
MOD_COLS = 1536
VMEM_LIMIT = 56 * 1024 * 1024


def _nt(a, b):
    return lax.dot_general(a, b, (((1,), (1,)), ((), ())), preferred_element_type=F32)


def _nn(a, b):
    return jnp.dot(a, b, preferred_element_type=F32)


def _sigmoid(x):
    return 1.0 / (1.0 + jnp.exp(-x))


def _mod_kernel(c_ref, w_ref, b_ref, o_ref):
    c = c_ref[...]
    a = (c * _sigmoid(c)).astype(BF16)
    o_ref[0] = _nn(a, w_ref[0].astype(BF16)) + b_ref[0]


def _modulation(cond, w_mod, b_mod):
    nb = cond.shape[0]
    ncol = N_MOD * D_MODEL
    return pl.pallas_call(
        _mod_kernel,
        grid=(DEPTH, ncol // MOD_COLS),
        in_specs=[
            pl.BlockSpec((nb, D_MODEL), lambda l, j: (0, 0)),
            pl.BlockSpec((1, D_MODEL, MOD_COLS), lambda l, j: (l, 0, j)),
            pl.BlockSpec((1, 1, MOD_COLS), lambda l, j: (l, 0, j)),
        ],
        out_specs=pl.BlockSpec((1, nb, MOD_COLS), lambda l, j: (l, 0, j)),
        out_shape=jax.ShapeDtypeStruct((DEPTH, nb, ncol), F32),
        compiler_params=pltpu.CompilerParams(vmem_limit_bytes=VMEM_LIMIT),
        name="adaln_mod",
    )(cond, w_mod, b_mod.reshape(DEPTH, 1, ncol))


def _premix_kernel(x_ref, mod_ref, g1_ref, win_ref, gmat_ref, qg_ref, kg_ref, *rest, rope):
    kt_ref = vt_ref = None
    if rope:
        cos_ref, sina_ref, sinb_ref, q_ref, k_ref, v_ref, u_ref = rest
    else:
        q_ref, k_ref, v_ref, u_ref, kt_ref, vt_ref = rest
    mod = mod_ref[0]
    sh1 = mod[0:1]
    sc1 = mod[1:2]

    def head_norm(t, g):
        msq = _nn((t * t).astype(BF16), gmat_ref[...])
        return t * lax.rsqrt(msq + EPS) * g

    def rotate(t, rs):
        outs = []
        for j in range(ATT_W // LANES):
            s = t[:, j * LANES:(j + 1) * LANES]
            outs.append(s * cos_ref[rs, :] + pltpu.roll(s, LANES - 16, 1) * sina_ref[rs, :]
                        + pltpu.roll(s, 16, 1) * sinb_ref[rs, :])
        return jnp.concatenate(outs, axis=1)

    for sb in range(x_ref.shape[0] // MIX_SUB):
        rs = slice(sb * MIX_SUB, (sb + 1) * MIX_SUB)
        x = x_ref[rs, :]
        ms = jnp.mean(x * x, axis=-1, keepdims=True)
        h = x * lax.rsqrt(ms + EPS) * g1_ref[...]
        h = h * (1.0 + sc1) + sh1
        qkvu = _nn(h.astype(BF16), win_ref[0])
        q = head_norm(qkvu[:, 0:ATT_W], qg_ref[...])
        k = head_norm(qkvu[:, ATT_W:2 * ATT_W], kg_ref[...])
        if rope:
            q = rotate(q, rs)
            k = rotate(k, rs)
        q_ref[rs, :] = (q * (D_SUB ** -0.5)).astype(q_ref.dtype)
        k_ref[rs, :] = k.astype(k_ref.dtype)
        v = qkvu[:, 2 * ATT_W:3 * ATT_W]
        if kt_ref is not None:
            kt_ref[sb] = k.T
            for hd in range(N_HEADS):
                vt_ref[pl.ds(sb * MIX_SUB * N_HEADS + hd, MIX_SUB, stride=N_HEADS), :] = (
                    v[:, hd * D_V:(hd + 1) * D_V])
        v_ref[rs, :] = v.astype(v_ref.dtype)
        u_ref[rs, :] = qkvu[:, 3 * ATT_W:].astype(u_ref.dtype)


def _premix(x, mods_l, g1, win, layer, gmat, qg, kg, rope_tabs, seq_len, mod_base):
    n = x.shape[0]
    tm = MIX_TILE
    per_seq = seq_len // tm
    rope = rope_tabs is not None
    assert rope or seq_len == MIX_SUB
    if mod_base == 0:
        mod_map = lambda i: (0, 0, 0)
    else:
        mod_map = lambda i: (mod_base + i // per_seq, 0, 0)
    row = lambda i: (i, 0)
    const = lambda i: (0, 0)
    in_specs = [
        pl.BlockSpec((tm, D_MODEL), row),
        pl.BlockSpec((1, N_MOD, D_MODEL), mod_map),
        pl.BlockSpec((1, D_MODEL), const),
        pl.BlockSpec((1, D_MODEL, 4 * ATT_W), lambda i: (layer, 0, 0)),
        pl.BlockSpec((ATT_W, ATT_W), const),
        pl.BlockSpec((1, ATT_W), const),
        pl.BlockSpec((1, ATT_W), const),
    ]
    args = [x, mods_l, g1, win, gmat, qg, kg]
    if rope:
        tab = lambda i: (i % per_seq, 0)
        in_specs += [pl.BlockSpec((tm, LANES), tab)] * 3
        args += list(rope_tabs)
    out = pl.BlockSpec((tm, ATT_W), row)
    out_specs = [out, out, out, out]
    out_shape = [jax.ShapeDtypeStruct((n, ATT_W), BF16),
                 jax.ShapeDtypeStruct((n, ATT_W), BF16),
                 jax.ShapeDtypeStruct((n, ATT_W), BF16),
                 jax.ShapeDtypeStruct((n, SSM_W), F32)]
    if not rope:
        out_specs.append(pl.BlockSpec((tm // seq_len, ATT_W, seq_len), lambda i: (i, 0, 0)))
        out_shape.append(jax.ShapeDtypeStruct((n // seq_len, ATT_W, seq_len), F32))
        out_specs.append(pl.BlockSpec((tm * N_HEADS, D_V), lambda i: (i, 0)))
        out_shape.append(jax.ShapeDtypeStruct((n * N_HEADS, D_V), F32))
    return pl.pallas_call(
        functools.partial(_premix_kernel, rope=rope),
        grid=(n // tm,),
        in_specs=in_specs,
        out_specs=out_specs,
        out_shape=out_shape,
        compiler_params=pltpu.CompilerParams(dimension_semantics=("parallel",),
                                             vmem_limit_bytes=VMEM_LIMIT),
        name="premix_rope" if rope else "premix",
    )(*args)


def _attn_kernel(q_ref, k_ref, v_ref, lamp_ref, sg_ref, *rest, lam_init, has_cache):
    if has_cache:
        kc_ref, vc_ref, o_ref = rest
    else:
        (o_ref,) = rest
    tq = q_ref.shape[0]
    lp = lamp_ref[...]
    l1 = jnp.sum(lp[0:1] * lp[1:2], axis=-1, keepdims=True)
    l2 = jnp.sum(lp[2:3] * lp[3:4], axis=-1, keepdims=True)
    lam = jnp.exp(l1) - jnp.exp(l2) + lam_init
    first = lax.broadcasted_iota(jnp.int32, (1, LANES), 1) < D_SUB
    tsub = min(ATTN_SUB, tq)
    for h in range(N_HEADS):
        sl = slice(h * D_V, (h + 1) * D_V)
        kn = k_ref[:, sl]
        vn = v_ref[:, sl]
        if has_cache:
            ncache = kc_ref.shape[3]
            kct = kc_ref[0, 0, sl, :].astype(BF16)
            vct = vc_ref[0, 0, pl.ds(h, ncache, stride=N_HEADS), :].astype(BF16)
        for r0 in range(0, tq, tsub):
            qh = q_ref[r0:r0 + tsub, sl]
            zero = jnp.zeros_like(qh)
            qq = jnp.concatenate([jnp.where(first, qh, zero), jnp.where(first, zero, qh)], axis=0)
            if has_cache:
                sc = jnp.concatenate([_nn(qq, kct), _nt(qq, kn)], axis=1)
            else:
                sc = _nt(qq, kn)
            m = jnp.max(sc, axis=-1, keepdims=True)
            e = jnp.exp(sc - m)
            ssum = jnp.sum(e, axis=-1, keepdims=True)
            eb = e.astype(BF16)
            if has_cache:
                pv = _nn(eb[:, :ncache], vct) + _nn(eb[:, ncache:], vn)
            else:
                pv = _nn(eb, vn)
            on = pv * (1.0 / ssum)
            o = on[:tsub] - lam * on[tsub:]
            ms = jnp.mean(o * o, axis=-1, keepdims=True)
            o_ref[r0:r0 + tsub, sl] = (o * lax.rsqrt(ms + EPS) * sg_ref[:, sl]
                                       * (1.0 - lam_init)).astype(o_ref.dtype)


def _attention(q, k, v, lamp, sg, cache, layer, batch, seq_len, lam_init):
    n = q.shape[0]
    tq = min(ATTN_TILE, seq_len)
    nq = seq_len // tq
    has_cache = cache is not None
    in_specs = [
        pl.BlockSpec((tq, ATT_W), lambda b, i: (b * nq + i, 0)),
        pl.BlockSpec((seq_len, ATT_W), lambda b, i: (b, 0)),
        pl.BlockSpec((seq_len, ATT_W), lambda b, i: (b, 0)),
        pl.BlockSpec((4, D_SUB), lambda b, i: (0, 0)),
        pl.BlockSpec((1, ATT_W), lambda b, i: (0, 0)),
    ]
    args = [q, k, v, lamp, sg]
    if has_cache:
        past = cache[0].shape[3]
        in_specs += [pl.BlockSpec((1, 1, ATT_W, past), lambda b, i: (b, layer, 0, 0)),
                     pl.BlockSpec((1, 1, past * N_HEADS, D_V), lambda b, i: (b, layer, 0, 0))]
        args += list(cache)
    return pl.pallas_call(
        functools.partial(_attn_kernel, lam_init=lam_init, has_cache=has_cache),
        grid=(batch, nq),
        in_specs=in_specs,
        out_specs=pl.BlockSpec((tq, ATT_W), lambda b, i: (b * nq + i, 0)),
        out_shape=jax.ShapeDtypeStruct((n, ATT_W), BF16),
        compiler_params=pltpu.CompilerParams(dimension_semantics=("parallel", "parallel"),
                                             vmem_limit_bytes=VMEM_LIMIT),
        name="attn_cache" if has_cache else "attn",
    )(*args)


def _ssm_kernel(ut_ref, h0_ref, p1_ref, pb_ref, yt_ref, hf_ref, r_scr, hin_scr, *,
                ctx_batch, ctx_chunks, lat_batch, lat_chunks):
    T = CHUNK
    W = SSM_CH * T
    lane = lax.broadcasted_iota(jnp.int32, (1, LANES), 1)
    lo = lane < SSM_P
    sgn_lr = jnp.where(lo, 1.0, -1.0).astype(F32)
    sgn_rl = -sgn_lr
    srow = lax.broadcasted_iota(jnp.int32, (T, LANES), 0).astype(F32)

    def swap(a):
        return pltpu.roll(a, SSM_P, 1)

    def tile_time(a):
        return jnp.concatenate([jnp.broadcast_to(a[i:i + 1, :], (SSM_CH, LANES)) for i in range(T)], axis=0)

    def rep_chan(a):
        return jnp.concatenate([a] * T, axis=0)

    def pair_pack(y):
        ys = swap(y)
        return jnp.where(lo, y, ys), jnp.where(lo, ys, y) * sgn_rl

    def cmul_const(x, yr2, yi2s):
        return x * yr2 + swap(x) * yi2s

    p1 = p1_ref[0]
    pb = pb_ref[0]

    def direction(d):
        lre2 = p1[3 * d + 0:3 * d + 1]
        lim2 = p1[3 * d + 1:3 * d + 2]
        ls2 = p1[3 * d + 2:3 * d + 3]
        bre2, bim2, cre2, cim2 = pb[4 * d + 0], pb[4 * d + 1], pb[4 * d + 2], pb[4 * d + 3]
        delta = jnp.exp(ls2)
        xr = lre2 * delta
        th = lim2 * delta

        def cpow_const(k):
            return jnp.exp(k * xr) * jnp.where(lo, jnp.cos(k * th), jnp.sin(k * th))

        mag1 = jnp.exp(xr)
        lbr2 = mag1 * jnp.cos(th)
        lbi2 = mag1 * jnp.sin(th)
        den = lre2 * lre2 + lim2 * lim2
        nr = lbr2 - 1.0
        cr2 = (nr * lre2 + lbi2 * lim2) / den
        ci2 = (lbi2 * lre2 - nr * lim2) / den
        bbr2 = rep_chan(cr2 * bre2 - ci2 * bim2)
        bbi2s = rep_chan((cr2 * bim2 + ci2 * bre2) * sgn_rl)
        ccr2 = rep_chan(cre2)
        cci2s = rep_chan(cim2 * sgn_rl)

        ang = srow * th
        cs = jnp.where(lo, jnp.cos(ang), jnp.sin(ang))
        wp = jnp.exp(srow * xr) * cs
        wn = jnp.exp(-(srow * xr)) * cs * sgn_lr

        def times(x, c1, c2s):
            return tile_time(x) * c1 + tile_time(swap(x)) * c2s

        lam_t = pair_pack(cpow_const(float(T)))
        if d == 0:
            left = times(wn, bbr2, bbi2s)
            bs = times(cmul_const(wn, *pair_pack(cpow_const(float(T - 1)))), bbr2, bbi2s)
            right = times(wp, ccr2, cci2s)
            cc = times(cmul_const(wp, *pair_pack(cpow_const(1.0))), ccr2, cci2s)
        else:
            left = times(wp, bbr2, bbi2s)
            bs = left
            right = times(wn, ccr2, cci2s)
            cc = times(cmul_const(wn, *lam_t), ccr2, cci2s)
        bbar = jnp.where(lo, bbr2, bbi2s)
        return left * sgn_lr, right, bs, cc * sgn_lr, lam_t, bbar

    lf, rf, bsf, ccf, lamt_f, bbar_f = direction(0)
    lb, rb, bsb, ccb, lamt_b, bbar_b = direction(1)

    s_idx = lax.broadcasted_iota(jnp.int32, (W, W), 0) // SSM_CH
    t_idx = lax.broadcasted_iota(jnp.int32, (W, W), 1) // SSM_CH
    a_t = (jnp.where(t_idx >= s_idx, _nt(lf.astype(BF16), rf.astype(BF16)), 0.0)
           + jnp.where(s_idx >= t_idx, _nt(lb.astype(BF16), rb.astype(BF16)), 0.0))
    w1 = jnp.concatenate([a_t, bsf, swap(bsf), bsb, swap(bsb)], axis=1).astype(BF16)
    r_scr[...] = _nn(ut_ref[0], w1)

    def chain(row0, nb, nchunks, col, lam_t, h, reverse):
        a1, a2s = lam_t
        hs = swap(h)
        order = range(nchunks - 1, -1, -1) if reverse else range(nchunks)
        hcol = slice(LANES, 2 * LANES) if reverse else slice(0, LANES)
        for c in order:
            rows = slice(row0 + c * nb, row0 + (c + 1) * nb)
            hin_scr[rows, hcol] = h
            s = r_scr[rows, col:col + LANES]
            ss = r_scr[rows, col + LANES:col + 2 * LANES]
            h, hs = h * a1 + hs * a2s + s, hs * a1 - h * a2s + ss
        return h

    zero = jnp.zeros((ctx_batch, LANES), F32)
    chain(0, ctx_batch, ctx_chunks, W, lamt_f, zero, False)
    chain(0, ctx_batch, ctx_chunks, W + 2 * LANES, lamt_b, zero, True)
    s_tiled = tile_time(srow)
    first_tok = jnp.where(s_tiled == 0.0, bbar_f, 0.0).astype(BF16)
    last_tok = jnp.where(s_tiled == float(T - 1), bbar_b, 0.0).astype(BF16)
    last_rows = (ctx_chunks - 1) * ctx_batch
    hf_ref[0, :, 0:LANES] = _nn(ut_ref[0, 0:ctx_batch, :], first_tok)
    hf_ref[0, :, LANES:2 * LANES] = _nn(ut_ref[0, last_rows:last_rows + ctx_batch, :], last_tok)
    lat0 = ctx_batch * ctx_chunks
    h0 = h0_ref[0]
    chain(lat0, lat_batch, lat_chunks, W, lamt_f, h0[:, 0:LANES], False)
    chain(lat0, lat_batch, lat_chunks, W + 2 * LANES, lamt_b, h0[:, LANES:2 * LANES], True)

    cc_cat = jnp.concatenate([ccf, ccb], axis=1).astype(BF16)
    yt_ref[0] = (r_scr[:, 0:W] + _nt(hin_scr[...].astype(BF16), cc_cat)).astype(yt_ref.dtype)


def _ssm(ut, h0, p1, pb, ctx_batch, ctx_chunks, lat_batch, lat_chunks):
    g, n, w = ut.shape
    return pl.pallas_call(
        functools.partial(_ssm_kernel, ctx_batch=ctx_batch, ctx_chunks=ctx_chunks,
                          lat_batch=lat_batch, lat_chunks=lat_chunks),
        grid=(g,),
        in_specs=[
            pl.BlockSpec((1, n, w), lambda i: (i, 0, 0)),
            pl.BlockSpec((1, lat_batch, 2 * LANES), lambda i: (i, 0, 0)),
            pl.BlockSpec((1, 8, LANES), lambda i: (i, 0, 0)),
            pl.BlockSpec((1, 8, SSM_CH, LANES), lambda i: (i, 0, 0, 0)),
        ],
        out_specs=[
            pl.BlockSpec((1, n, w), lambda i: (i, 0, 0)),
            pl.BlockSpec((1, ctx_batch, 2 * LANES), lambda i: (i, 0, 0)),
        ],
        out_shape=[jax.ShapeDtypeStruct((g, n, w), BF16),
                   jax.ShapeDtypeStruct((g, ctx_batch, 2 * LANES), F32)],
        scratch_shapes=[pltpu.VMEM((n, w + 4 * LANES), F32), pltpu.VMEM((n, 2 * LANES), F32)],
        compiler_params=pltpu.CompilerParams(dimension_semantics=("parallel",),
                                             vmem_limit_bytes=VMEM_LIMIT),
        name="s5_scan",
    )(ut, h0, p1, pb)


def _tail_kernel(x_ref, oatt_ref, y_ref, u_ref, mod_ref, d_ref, wglu_ref, bglu_ref, wout_ref, g2_ref,
                 wup_ref, cw_ref, cb_ref, wd_ref, o_ref, x1_scr, h2_scr, *, seq_len):
    rows = x_ref.shape[0]
    tf = FFN_SLAB
    mod = mod_ref[0]
    gt1 = mod[2:3]
    sh2 = mod[3:4]
    sc2 = mod[4:5]
    gt2 = mod[5:6]
    for sb in range(rows // MIX_SUB):
        rs = slice(sb * MIX_SUB, (sb + 1) * MIX_SUB)
        y = y_ref[rs, :].astype(F32) + d_ref[...] * u_ref[rs, :]
        z = jax.nn.gelu(y)
        gate = _sigmoid(_nn(z.astype(BF16), wglu_ref[0]) + bglu_ref[...])
        o_ssm = (z * gate).astype(BF16)
        mix = _nn(oatt_ref[rs, :], wout_ref[0, 0:ATT_W, :]) + _nn(o_ssm, wout_ref[0, ATT_W:, :])
        x1 = x_ref[rs, :] + gt1 * mix
        x1_scr[rs, :] = x1
        ms = jnp.mean(x1 * x1, axis=-1, keepdims=True)
        h2 = x1 * lax.rsqrt(ms + EPS) * g2_ref[...]
        h2_scr[rs, :] = (h2 * (1.0 + sc2) + sh2).astype(BF16)

    row8 = lax.broadcasted_iota(jnp.int32, (8, 1), 0)
    h2 = h2_scr[...]

    def conv(up, cw, cb):
        prev = pltpu.roll(up, 1, 0)
        nxt = pltpu.roll(up, rows - 1, 0)
        pp, nn = [], []
        for r in range(0, rows, seq_len):
            e = r + seq_len
            pp += [jnp.where(row8 == 0, 0.0, prev[r:r + 8]), prev[r + 8:e]]
            nn += [nxt[r:e - 8], jnp.where(row8 == 7, 0.0, nxt[e - 8:e])]
        prev = jnp.concatenate(pp, axis=0)
        nxt = jnp.concatenate(nn, axis=0)
        return prev * cw[0:1] + up * cw[1:2] + nxt * cw[2:3] + cb

    acts = []
    for j in range(D_FF // tf):
        cv = slice(j * tf, (j + 1) * tf)
        cg = slice(D_FF + j * tf, D_FF + (j + 1) * tf)
        val = conv(_nn(h2, wup_ref[0, :, cv]), cw_ref[:, cv], cb_ref[:, cv])
        gate = conv(_nn(h2, wup_ref[0, :, cg]), cw_ref[:, cg], cb_ref[:, cg])
        acts.append((gate * _sigmoid(gate) * val).astype(BF16))
    act = jnp.concatenate(acts, axis=1)
    o_ref[...] = x1_scr[...] + gt2 * _nn(act, wd_ref[0])


def _tail(x, oatt, y, u, mods_l, d, wglu, bglu, wout, g2, wup, cw, cb, wdown, layer, seq_len, mod_base):
    n = x.shape[0]
    rows = FFN_ROWS
    per_seq = seq_len // rows if seq_len >= rows else 1
    if mod_base == 0:
        mod_map = lambda i: (0, 0, 0)
    else:
        mod_map = lambda i: (mod_base + i // per_seq, 0, 0)
    row = lambda i: (i, 0)
    const = lambda i: (0, 0)
    lay = lambda i: (layer, 0, 0)
    resident = dict(pipeline_mode=pl.Buffered(1))
    return pl.pallas_call(
        functools.partial(_tail_kernel, seq_len=seq_len),
        grid=(n // rows,),
        in_specs=[
            pl.BlockSpec((rows, D_MODEL), row),
            pl.BlockSpec((rows, ATT_W), row),
            pl.BlockSpec((rows, SSM_W), row),
            pl.BlockSpec((rows, SSM_W), row),
            pl.BlockSpec((1, N_MOD, D_MODEL), mod_map),
            pl.BlockSpec((1, SSM_W), const),
            pl.BlockSpec((1, SSM_W, SSM_W), lay, **resident),
            pl.BlockSpec((1, SSM_W), const),
            pl.BlockSpec((1, D_MODEL, D_MODEL), lay, **resident),
            pl.BlockSpec((1, D_MODEL), const),
            pl.BlockSpec((1, D_MODEL, 2 * D_FF), lay, **resident),
            pl.BlockSpec((3, 2 * D_FF), const),
            pl.BlockSpec((1, 2 * D_FF), const),
            pl.BlockSpec((1, D_FF, D_MODEL), lay, **resident),
        ],
        out_specs=pl.BlockSpec((rows, D_MODEL), row),
        out_shape=jax.ShapeDtypeStruct((n, D_MODEL), F32),
        scratch_shapes=[pltpu.VMEM((rows, D_MODEL), F32), pltpu.VMEM((rows, D_MODEL), BF16)],
        compiler_params=pltpu.CompilerParams(dimension_semantics=("parallel",),
                                             vmem_limit_bytes=VMEM_LIMIT),
        name="layer_tail",
    )(x, oatt, y, u, mods_l, d, wglu, bglu, wout, g2, wup, cw, cb, wdown)


def _rope_tables(n_tokens):
    pos = np.arange(n_tokens)
    row = (pos // GRID_W).astype(np.float64)
    col = (pos % GRID_W).astype(np.float64)
    nf = ROPE_AXIS // 2
    inv_freq = 1.0 / (ROPE_THETA ** (np.arange(nf, dtype=np.float64) / nf))
    lane = np.arange(LANES)
    freq = inv_freq[lane % nf]
    is_col = (lane % D_SUB) >= ROPE_AXIS
    ang = np.where(is_col[None, :], col[:, None], row[:, None]) * freq[None, :]
    second = (lane % ROPE_AXIS) >= nf
    cos = np.cos(ang)
    sin = np.sin(ang)
    sina = np.where(second[None, :], 0.0, -sin)
    sinb = np.where(second[None, :], sin, 0.0)
    return tuple(jnp.asarray(t, dtype=F32) for t in (cos, sina, sinb))


TIME_BLOCKS = CHUNK // GROUPS_PER_BLOCK


def _chunk_layout_passes(ctx_batch, ctx_len, lat_batch, lat_len):
    return ((0, ctx_batch, ctx_len, 0), (1, lat_batch, lat_len, ctx_batch * (ctx_len // CHUNK)))


def _row_permutation(batch):
    n = batch * CHUNK
    p = np.zeros((n, n), np.float32)
    b, s = np.meshgrid(np.arange(batch), np.arange(CHUNK), indexing="ij")
    p[(s * batch + b).ravel(), (b * CHUNK + s).ravel()] = 1.0
    return p


def _lane_permutation():
    n = GROUPS_PER_BLOCK * LANES
    q = np.zeros((n, n), np.float32)
    t, g, ch = np.meshgrid(np.arange(GROUPS_PER_BLOCK), np.arange(GROUPS_PER_BLOCK), np.arange(SSM_CH),
                           indexing="ij")
    q[(t * LANES + g * SSM_CH + ch).ravel(), (g * LANES + t * SSM_CH + ch).ravel()] = 1.0
    return q


def _to_chunks_kernel(uc_ref, ul_ref, pc_ref, pl_ref, q_ref, o_ref, v_scr, *, passes):
    for idx, batch, seq_len, row0 in passes:
        u_ref = (uc_ref, ul_ref)[idx]
        p_ref = (pc_ref, pl_ref)[idx]
        for c in range(seq_len // CHUNK):
            x = jnp.concatenate([u_ref[b * seq_len + c * CHUNK:b * seq_len + (c + 1) * CHUNK, :]
                                 for b in range(batch)], axis=0)
            r = _nn(p_ref[...], x.astype(BF16))
            rows = slice(row0 + c * batch, row0 + (c + 1) * batch)
            for s in range(CHUNK):
                j, t = divmod(s, GROUPS_PER_BLOCK)
                v_scr[j, rows, t * LANES:(t + 1) * LANES] = r[s * batch:(s + 1) * batch]
    for j in range(TIME_BLOCKS):
        w = _nn(v_scr[j].astype(BF16), q_ref[...])
        for g in range(GROUPS_PER_BLOCK):
            o_ref[g, :, j * LANES:(j + 1) * LANES] = w[:, g * LANES:(g + 1) * LANES].astype(o_ref.dtype)


def _from_chunks_kernel(y_ref, pc_ref, pl_ref, q_ref, oc_ref, ol_ref, v_scr, *, passes):
    for j in range(TIME_BLOCKS):
        yj = jnp.concatenate([y_ref[g, :, j * LANES:(j + 1) * LANES] for g in range(GROUPS_PER_BLOCK)], axis=1)
        v_scr[j] = _nn(yj, q_ref[...])
    for idx, batch, seq_len, row0 in passes:
        o_ref = (oc_ref, ol_ref)[idx]
        p_ref = (pc_ref, pl_ref)[idx]
        for c in range(seq_len // CHUNK):
            rows = slice(row0 + c * batch, row0 + (c + 1) * batch)
            x = jnp.concatenate([v_scr[s // GROUPS_PER_BLOCK, rows,
                                       (s % GROUPS_PER_BLOCK) * LANES:(s % GROUPS_PER_BLOCK + 1) * LANES]
                                 for s in range(CHUNK)], axis=0)
            r = _nn(p_ref[...], x.astype(BF16))
            for b in range(batch):
                o_ref[b * seq_len + c * CHUNK:b * seq_len + (c + 1) * CHUNK, :] = (
                    r[b * CHUNK:(b + 1) * CHUNK].astype(o_ref.dtype))


def _to_chunks(uc, ul, ctx_batch, ctx_len, lat_batch, lat_len):
    passes = _chunk_layout_passes(ctx_batch, ctx_len, lat_batch, lat_len)
    rows = ctx_batch * (ctx_len // CHUNK) + lat_batch * (lat_len // CHUNK)
    pc = jnp.asarray(_row_permutation(ctx_batch), dtype=BF16)
    pn = jnp.asarray(_row_permutation(lat_batch), dtype=BF16)
    q = jnp.asarray(_lane_permutation(), dtype=BF16)
    const = lambda j: (0, 0)
    return pl.pallas_call(
        functools.partial(_to_chunks_kernel, passes=passes),
        grid=(SSM_GROUPS // GROUPS_PER_BLOCK,),
        in_specs=[pl.BlockSpec((uc.shape[0], LANES), lambda j: (0, j)),
                  pl.BlockSpec((ul.shape[0], LANES), lambda j: (0, j)),
                  pl.BlockSpec(pc.shape, const), pl.BlockSpec(pn.shape, const), pl.BlockSpec(q.shape, const)],
        out_specs=pl.BlockSpec((GROUPS_PER_BLOCK, rows, SSM_CH * CHUNK), lambda j: (j, 0, 0)),
        out_shape=jax.ShapeDtypeStruct((SSM_GROUPS, rows, SSM_CH * CHUNK), BF16),
        scratch_shapes=[pltpu.VMEM((TIME_BLOCKS, rows, GROUPS_PER_BLOCK * LANES), F32)],
        compiler_params=pltpu.CompilerParams(dimension_semantics=("parallel",),
                                             vmem_limit_bytes=VMEM_LIMIT),
        name="to_chunks",
    )(uc, ul, pc, pn, q)


def _from_chunks(yt, ctx_batch, ctx_len, lat_batch, lat_len):
    passes = _chunk_layout_passes(ctx_batch, ctx_len, lat_batch, lat_len)
    rows = yt.shape[1]
    n_ctx = ctx_batch * ctx_len
    n_lat = lat_batch * lat_len
    pc = jnp.asarray(_row_permutation(ctx_batch).T, dtype=BF16)
    pn = jnp.asarray(_row_permutation(lat_batch).T, dtype=BF16)
    q = jnp.asarray(_lane_permutation().T, dtype=BF16)
    const = lambda j: (0, 0)
    return pl.pallas_call(
        functools.partial(_from_chunks_kernel, passes=passes),
        grid=(SSM_GROUPS // GROUPS_PER_BLOCK,),
        in_specs=[pl.BlockSpec((GROUPS_PER_BLOCK, rows, SSM_CH * CHUNK), lambda j: (j, 0, 0)),
                  pl.BlockSpec(pc.shape, const), pl.BlockSpec(pn.shape, const), pl.BlockSpec(q.shape, const)],
        out_specs=[pl.BlockSpec((n_ctx, LANES), lambda j: (0, j)),
                   pl.BlockSpec((n_lat, LANES), lambda j: (0, j))],
        out_shape=[jax.ShapeDtypeStruct((n_ctx, SSM_W), BF16), jax.ShapeDtypeStruct((n_lat, SSM_W), BF16)],
        scratch_shapes=[pltpu.VMEM((TIME_BLOCKS, rows, GROUPS_PER_BLOCK * LANES), F32)],
        compiler_params=pltpu.CompilerParams(dimension_semantics=("parallel",),
                                             vmem_limit_bytes=VMEM_LIMIT),
        name="from_chunks",
    )(yt, pc, pn, q)


def _dup_lanes(a):
    return jnp.concatenate([a, a], axis=-1)


def kernel(x_prompt, x_sample, cache_k, cache_v, state_ssm, c, c_ctx, w_mod, b_mod, g_norm1, w_in, q_norm, k_norm, lambda_q1, lambda_k1, lambda_q2, lambda_k2, subln_g, ssm_lambda_re, ssm_lambda_im, ssm_log_step, ssm_b_re, ssm_b_im, ssm_c_re, ssm_c_im, ssm_d, w_glu, b_glu, w_out, g_norm2, w_up, conv_w, conv_b, w_down):
    nb_ctx, len_ctx, _ = x_prompt.shape
    nb_lat, len_lat, _ = x_sample.shape
    past = cache_k.shape[2]

    n_cond = 16
    cond = jnp.zeros((n_cond, D_MODEL), F32).at[0].set(c_ctx).at[1:1 + nb_lat].set(c)
    mods = _modulation(cond, w_mod, b_mod).reshape(DEPTH, n_cond, N_MOD, D_MODEL)

    gidx = np.arange(ATT_W) // D_SUB
    gmat = jnp.asarray(np.where(gidx[:, None] == gidx[None, :], 1.0 / D_SUB, 0.0), dtype=BF16)
    cache_kt = cache_k.transpose(0, 1, 3, 4, 5, 2).reshape(nb_lat, DEPTH, ATT_W, past)
    cache_vr = cache_v.reshape(nb_lat, DEPTH, past * N_HEADS, D_V)
    rope_tabs = _rope_tables(len_lat)
    win_b = w_in.astype(BF16)
    wout_b = w_out.astype(BF16)
    wglu_b = w_glu.astype(BF16)
    wup_b = w_up.astype(BF16)
    wdown_b = w_down.astype(BF16)

    xp = x_prompt.reshape(nb_ctx * len_ctx, D_MODEL)
    xs = x_sample.reshape(nb_lat * len_lat, D_MODEL)
    ks, vs, ss = [], [], []
    for l in range(DEPTH):
        lam_init = 0.8 - 0.6 * math.exp(-0.3 * l)
        mods_l = mods[l]
        g1 = g_norm1[l].reshape(1, D_MODEL)
        g2 = g_norm2[l].reshape(1, D_MODEL)
        qg = jnp.tile(q_norm[l], ATT_W // D_SUB).reshape(1, ATT_W)
        kg = jnp.tile(k_norm[l], ATT_W // D_SUB).reshape(1, ATT_W)
        sg = jnp.tile(subln_g[l], N_HEADS).reshape(1, ATT_W)
        lamp = jnp.stack([lambda_q1[l], lambda_k1[l], lambda_q2[l], lambda_k2[l]])
        d = ssm_d[l].reshape(1, SSM_W)
        bglu = b_glu[l].reshape(1, SSM_W)
        cb = conv_b[l].reshape(1, 2 * D_FF)
        cw = conv_w[l]

        zeros = jnp.zeros((SSM_GROUPS, SSM_P), F32)
        step = jnp.broadcast_to(ssm_log_step[l][:, :, None], (2, SSM_GROUPS, SSM_P))
        p1 = _dup_lanes(jnp.stack([ssm_lambda_re[l, 0], ssm_lambda_im[l, 0], step[0],
                                   ssm_lambda_re[l, 1], ssm_lambda_im[l, 1], step[1], zeros, zeros], axis=1))
        bt = lambda a: a.transpose(0, 2, 1)
        pb = _dup_lanes(jnp.stack([bt(ssm_b_re[l, 0]), bt(ssm_b_im[l, 0]), ssm_c_re[l, 0], ssm_c_im[l, 0],
                                   bt(ssm_b_re[l, 1]), bt(ssm_b_im[l, 1]), ssm_c_re[l, 1], ssm_c_im[l, 1]], axis=1))
        h0 = state_ssm[:, l].transpose(2, 0, 1, 4, 3).reshape(SSM_GROUPS, nb_lat, 4 * SSM_P)

        qc, kc_new, vc_new, uc, kt, vt = _premix(xp, mods_l, g1, win_b, l, gmat, qg, kg, None, len_ctx, 0)
        ql, kl, vl, ul = _premix(xs, mods_l, g1, win_b, l, gmat, qg, kg, rope_tabs, len_lat, 1)

        oc = _attention(qc, kc_new, vc_new, lamp, sg, None, l, nb_ctx, len_ctx, lam_init)
        ol = _attention(ql, kl, vl, lamp, sg, (cache_kt, cache_vr), l, nb_lat, len_lat, lam_init)

        ut = _to_chunks(uc, ul, nb_ctx, len_ctx, nb_lat, len_lat)
        yt, hfin = _ssm(ut, h0, p1, pb, nb_ctx, len_ctx // CHUNK, nb_lat, len_lat // CHUNK)
        yc, yl = _from_chunks(yt, nb_ctx, len_ctx, nb_lat, len_lat)

        xp = _tail(xp, oc, yc, uc, mods_l, d, wglu_b, bglu, wout_b, g2, wup_b, cw, cb, wdown_b, l, len_ctx, 0)
        xs = _tail(xs, ol, yl, ul, mods_l, d, wglu_b, bglu, wout_b, g2, wup_b, cw, cb, wdown_b, l, len_lat, 1)

        ks.append(kt)
        vs.append(vt.reshape(nb_ctx, len_ctx, N_HEADS, D_V))
        ss.append(hfin.reshape(SSM_GROUPS, nb_ctx, 2, 2, SSM_P).transpose(1, 2, 0, 4, 3))

    new_k = jnp.stack(ks, axis=1).reshape(nb_ctx, DEPTH, N_HEADS, 2, D_SUB, len_ctx).transpose(0, 1, 5, 2, 3, 4)
    return (xp.reshape(nb_ctx, len_ctx, D_MODEL), xs.reshape(nb_lat, len_lat, D_MODEL),
            new_k, jnp.stack(vs, axis=1), jnp.stack(ss, axis=1))
```

```python
import functools
import math

import numpy as np
import jax
import jax.numpy as jnp
from jax import lax
from jax.experimental import pallas as pl
from jax.experimental.pallas import tpu as pltpu

F32 = jnp.float32
BF16 = jnp.bfloat16

D_MODEL = 1024
DEPTH = 2
GRID_W = 64
ATT_W = 512
SSM_W = 512
N_HEADS = 4
D_V = 128
D_SUB = 64
ROPE_AXIS = 32
ROPE_THETA = 10000.0
SSM_CH = 16
SSM_GROUPS = 32
SSM_P = 64
D_FF = 2048
N_MOD = 6
EPS = 1e-6

LANES = 128
GROUPS_PER_BLOCK = LANES // SSM_CH
CHUNK = 32
S5_GROUPS_PER_STEP = 2
ATTN_TILE = 1024
ATTN_SUB = 256
MIX_TILE = 512
MIX_SUB = 256
FFN_ROWS = 1024
FFN_SLAB = 512
MOD_COLS = 1536
VMEM_LIMIT = 56 * 1024 * 1024


def _nt(a, b):
    return lax.dot_general(a, b, (((1,), (1,)), ((), ())), preferred_element_type=F32)


def _nn(a, b):
    return jnp.dot(a, b, preferred_element_type=F32)


def _sigmoid(x):
    return 1.0 / (1.0 + jnp.exp(-x))


def _mod_kernel(c_ref, w_ref, b_ref, o_ref):
    c = c_ref[...]
    a = (c * _sigmoid(c)).astype(BF16)
    o_ref[0] = _nn(a, w_ref[0].astype(BF16)) + b_ref[0]


def _modulation(cond, w_mod, b_mod):
    nb = cond.shape[0]
    ncol = N_MOD * D_MODEL
    return pl.pallas_call(
        _mod_kernel,
        grid=(DEPTH, ncol // MOD_COLS),
        in_specs=[
            pl.BlockSpec((nb, D_MODEL), lambda l, j: (0, 0)),
            pl.BlockSpec((1, D_MODEL, MOD_COLS), lambda l, j: (l, 0, j)),
            pl.BlockSpec((1, 1, MOD_COLS), lambda l, j: (l, 0, j)),
        ],
        out_specs=pl.BlockSpec((1, nb, MOD_COLS), lambda l, j: (l, 0, j)),
        out_shape=jax.ShapeDtypeStruct((DEPTH, nb, ncol), F32),
        compiler_params=pltpu.CompilerParams(vmem_limit_bytes=VMEM_LIMIT),
        name="adaln_mod",
    )(cond, w_mod, b_mod.reshape(DEPTH, 1, ncol))


def _premix_kernel(x_ref, mod_ref, g1_ref, win_ref, gmat_ref, qg_ref, kg_ref, *rest, rope):
    kt_ref = vt_ref = None
    if rope:
        cos_ref, sina_ref, sinb_ref, q_ref, k_ref, v_ref, u_ref = rest
    else:
        q_ref, k_ref, v_ref, u_ref, kt_ref, vt_ref = rest[-6:]
    mod = mod_ref[0]
    sh1 = mod[0:1]
    sc1 = mod[1:2]

    def head_norm(t, g):
        msq = _nn((t * t).astype(BF16), gmat_ref[...])
        return t * lax.rsqrt(msq + EPS) * g

    def rotate(t, rs):
        outs = []
        for j in range(ATT_W // LANES):
            s = t[:, j * LANES:(j + 1) * LANES]
            outs.append(s * cos_ref[rs, :] + pltpu.roll(s, LANES - 16, 1) * sina_ref[rs, :]
                        + pltpu.roll(s, 16, 1) * sinb_ref[rs, :])
        return jnp.concatenate(outs, axis=1)

    for sb in range(x_ref.shape[0] // MIX_SUB):
        rs = slice(sb * MIX_SUB, (sb + 1) * MIX_SUB)
        x = x_ref[rs, :]
        ms = jnp.mean(x * x, axis=-1, keepdims=True)
        h = x * lax.rsqrt(ms + EPS) * g1_ref[...]
        h = h * (1.0 + sc1) + sh1
        qkvu = _nn(h.astype(BF16), win_ref[0])
        q = head_norm(qkvu[:, 0:ATT_W], qg_ref[...])
        k = head_norm(qkvu[:, ATT_W:2 * ATT_W], kg_ref[...])
        if rope:
            q = rotate(q, rs)
            k = rotate(k, rs)
        q_ref[rs, :] = (q * (D_SUB ** -0.5)).astype(q_ref.dtype)
        k_ref[rs, :] = k.astype(k_ref.dtype)
        v = qkvu[:, 2 * ATT_W:3 * ATT_W]
        if kt_ref is not None:
            kt = k.T
            for slot in range(kt_ref.shape[1]):
                kt_ref[sb, slot] = kt
                for hd in range(N_HEADS):
                    vt_ref[sb, slot, pl.ds(hd, MIX_SUB, stride=N_HEADS), :] = v[:, hd * D_V:(hd + 1) * D_V]
        v_ref[rs, :] = v.astype(v_ref.dtype)
        u_ref[rs, :] = qkvu[:, 3 * ATT_W:].astype(u_ref.dtype)


def _premix(x, mods_l, g1, win, layer, gmat, qg, kg, rope_tabs, seq_len, mod_base, carry=None):
    n = x.shape[0]
    tm = MIX_TILE
    per_seq = seq_len // tm
    rope = rope_tabs is not None
    assert rope or seq_len == MIX_SUB
    if mod_base == 0:
        mod_map = lambda i: (0, 0, 0)
    else:
        mod_map = lambda i: (mod_base + i // per_seq, 0, 0)
    row = lambda i: (i, 0)
    const = lambda i: (0, 0)
    in_specs = [
        pl.BlockSpec((tm, D_MODEL), row),
        pl.BlockSpec((1, N_MOD, D_MODEL), mod_map),
        pl.BlockSpec((1, D_MODEL), const),
        pl.BlockSpec((1, D_MODEL, 4 * ATT_W), lambda i: (layer, 0, 0)),
        pl.BlockSpec((ATT_W, ATT_W), const),
        pl.BlockSpec((1, ATT_W), const),
        pl.BlockSpec((1, ATT_W), const),
    ]
    args = [x, mods_l, g1, win, gmat, qg, kg]
    if rope:
        tab = lambda i: (i % per_seq, 0)
        in_specs += [pl.BlockSpec((tm, LANES), tab)] * 3
        args += list(rope_tabs)
    out = pl.BlockSpec((tm, ATT_W), row)
    out_specs = [out, out, out, out]
    out_shape = [jax.ShapeDtypeStruct((n, ATT_W), BF16),
                 jax.ShapeDtypeStruct((n, ATT_W), BF16),
                 jax.ShapeDtypeStruct((n, ATT_W), BF16),
                 jax.ShapeDtypeStruct((n, SSM_W), F32)]
    aliases = {}
    if not rope:
        nseq = tm // seq_len
        slots, first = (DEPTH, 0) if carry is None else (1, layer)
        out_specs.append(pl.BlockSpec((nseq, slots, ATT_W, seq_len), lambda i: (i, first, 0, 0)))
        out_shape.append(jax.ShapeDtypeStruct((n // seq_len, DEPTH, ATT_W, seq_len), F32))
        out_specs.append(pl.BlockSpec((nseq, slots, seq_len * N_HEADS, D_V), lambda i: (i, first, 0, 0)))
        out_shape.append(jax.ShapeDtypeStruct((n // seq_len, DEPTH, seq_len * N_HEADS, D_V), F32))
        if carry is not None:
            aliases = {len(args): 4, len(args) + 1: 5}
            in_specs += [pl.BlockSpec(memory_space=pl.ANY)] * 2
            args += list(carry)
    return pl.pallas_call(
        functools.partial(_premix_kernel, rope=rope),
        grid=(n // tm,),
        in_specs=in_specs,
        out_specs=out_specs,
        out_shape=out_shape,
        input_output_aliases=aliases,
        compiler_params=pltpu.CompilerParams(dimension_semantics=("parallel",),
                                             vmem_limit_bytes=VMEM_LIMIT),
        name="premix_rope" if rope else "premix",
    )(*args)


def _attn_kernel(q_ref, k_ref, v_ref, lamp_ref, sg_ref, *rest, lam_init, has_cache):
    if has_cache:
        kc_ref, vc_ref, o_ref = rest
    else:
        (o_ref,) = rest
    tq = q_ref.shape[0]
    lp = lamp_ref[...]
    l1 = jnp.sum(lp[0:1] * lp[1:2], axis=-1, keepdims=True)
    l2 = jnp.sum(lp[2:3] * lp[3:4], axis=-1, keepdims=True)
    lam = jnp.exp(l1) - jnp.exp(l2) + lam_init
    first = lax.broadcasted_iota(jnp.int32, (1, LANES), 1) < D_SUB
    tsub = min(ATTN_SUB, tq)
    for h in range(N_HEADS):
        sl = slice(h * D_V, (h + 1) * D_V)
        kn = k_ref[:, sl]
        vn = v_ref[:, sl]
        if has_cache:
            ncache = kc_ref.shape[3]
            kct = kc_ref[0, 0, sl, :].astype(BF16)
            vct = vc_ref[0, 0, pl.ds(h, ncache, stride=N_HEADS), :].astype(BF16)
        for r0 in range(0, tq, tsub):
            qh = q_ref[r0:r0 + tsub, sl]
            zero = jnp.zeros_like(qh)
            qq = jnp.concatenate([jnp.where(first, qh, zero), jnp.where(first, zero, qh)], axis=0)
            if has_cache:
                sc = jnp.concatenate([_nn(qq, kct), _nt(qq, kn)], axis=1)
            else:
                sc = _nt(qq, kn)
            m = jnp.max(sc, axis=-1, keepdims=True)
            e = jnp.exp(sc - m)
            ssum = jnp.sum(e, axis=-1, keepdims=True)
            eb = e.astype(BF16)
            if has_cache:
                pv = _nn(eb[:, :ncache], vct) + _nn(eb[:, ncache:], vn)
            else:
                pv = _nn(eb, vn)
            on = pv * (1.0 / ssum)
            o = on[:tsub] - lam * on[tsub:]
            ms = jnp.mean(o * o, axis=-1, keepdims=True)
            o_ref[r0:r0 + tsub, sl] = (o * lax.rsqrt(ms + EPS) * sg_ref[:, sl]
                                       * (1.0 - lam_init)).astype(o_ref.dtype)


def _attention(q, k, v, lamp, sg, cache, layer, batch, seq_len, lam_init):
    n = q.shape[0]
    tq = min(ATTN_TILE, seq_len)
    nq = seq_len // tq
    has_cache = cache is not None
    in_specs = [
        pl.BlockSpec((tq, ATT_W), lambda b, i: (b * nq + i, 0)),
        pl.BlockSpec((seq_len, ATT_W), lambda b, i: (b, 0)),
        pl.BlockSpec((seq_len, ATT_W), lambda b, i: (b, 0)),
        pl.BlockSpec((4, D_SUB), lambda b, i: (0, 0)),
        pl.BlockSpec((1, ATT_W), lambda b, i: (0, 0)),
    ]
    args = [q, k, v, lamp, sg]
    if has_cache:
        past = cache[0].shape[3]
        in_specs += [pl.BlockSpec((1, 1, ATT_W, past), lambda b, i: (b, layer, 0, 0)),
                     pl.BlockSpec((1, 1, past * N_HEADS, D_V), lambda b, i: (b, layer, 0, 0))]
        args += list(cache)
    return pl.pallas_call(
        functools.partial(_attn_kernel, lam_init=lam_init, has_cache=has_cache),
        grid=(batch, nq),
        in_specs=in_specs,
        out_specs=pl.BlockSpec((tq, ATT_W), lambda b, i: (b * nq + i, 0)),
        out_shape=jax.ShapeDtypeStruct((n, ATT_W), BF16),
        compiler_params=pltpu.CompilerParams(dimension_semantics=("parallel", "parallel"),
                                             vmem_limit_bytes=VMEM_LIMIT),
        name="attn_cache" if has_cache else "attn",
    )(*args)


def _ssm_kernel(ut_ref, h0_ref, p1_ref, pb_ref, yt_ref, hf_ref, r_scr, hin_scr, **sizes):
    for gi in range(ut_ref.shape[0]):
        one = lambda ref, gi=gi: ref.at[pl.ds(gi, 1)]
        _ssm_group(one(ut_ref), one(h0_ref), one(p1_ref), one(pb_ref), one(yt_ref), one(hf_ref),
                   r_scr.at[gi], hin_scr.at[gi], **sizes)


def _ssm_group(ut_ref, h0_ref, p1_ref, pb_ref, yt_ref, hf_ref, r_scr, hin_scr, *,
               ctx_batch, ctx_chunks, lat_batch, lat_chunks):
    T = CHUNK
    W = SSM_CH * T
    lane = lax.broadcasted_iota(jnp.int32, (1, LANES), 1)
    lo = lane < SSM_P
    sgn_lr = jnp.where(lo, 1.0, -1.0).astype(F32)
    sgn_rl = -sgn_lr
    srow = lax.broadcasted_iota(jnp.int32, (T, LANES), 0).astype(F32)

    def swap(a):
        return pltpu.roll(a, SSM_P, 1)

    def tile_time(a):
        return jnp.concatenate([jnp.broadcast_to(a[i:i + 1, :], (SSM_CH, LANES)) for i in range(T)], axis=0)

    def rep_chan(a):
        return jnp.concatenate([a] * T, axis=0)

    def pair_pack(y):
        ys = swap(y)
        return jnp.where(lo, y, ys), jnp.where(lo, ys, y) * sgn_rl

    def cmul_const(x, yr2, yi2s):
        return x * yr2 + swap(x) * yi2s

    p1 = p1_ref[0]
    pb = pb_ref[0]

    def direction(d):
        lre2 = p1[3 * d + 0:3 * d + 1]
        lim2 = p1[3 * d + 1:3 * d + 2]
        ls2 = p1[3 * d + 2:3 * d + 3]
        bre2, bim2, cre2, cim2 = pb[4 * d + 0], pb[4 * d + 1], pb[4 * d + 2], pb[4 * d + 3]
        delta = jnp.exp(ls2)
        xr = lre2 * delta
        th = lim2 * delta

        def cpow_const(k):
            return jnp.exp(k * xr) * jnp.where(lo, jnp.cos(k * th), jnp.sin(k * th))

        mag1 = jnp.exp(xr)
        lbr2 = mag1 * jnp.cos(th)
        lbi2 = mag1 * jnp.sin(th)
        den = lre2 * lre2 + lim2 * lim2
        nr = lbr2 - 1.0
        cr2 = (nr * lre2 + lbi2 * lim2) / den
        ci2 = (lbi2 * lre2 - nr * lim2) / den
        bbr2 = rep_chan(cr2 * bre2 - ci2 * bim2)
        bbi2s = rep_chan((cr2 * bim2 + ci2 * bre2) * sgn_rl)
        ccr2 = rep_chan(cre2)
        cci2s = rep_chan(cim2 * sgn_rl)

        ang = srow * th
        cs = jnp.where(lo, jnp.cos(ang), jnp.sin(ang))
        wp = jnp.exp(srow * xr) * cs
        wn = jnp.exp(-(srow * xr)) * cs * sgn_lr

        def times(x, c1, c2s):
            return tile_time(x) * c1 + tile_time(swap(x)) * c2s

        lam_t = pair_pack(cpow_const(float(T)))
        if d == 0:
            left = times(wn, bbr2, bbi2s)
            bs = times(cmul_const(wn, *pair_pack(cpow_const(float(T - 1)))), bbr2, bbi2s)
            right = times(wp, ccr2, cci2s)
            cc = times(cmul_const(wp, *pair_pack(cpow_const(1.0))), ccr2, cci2s)
        else:
            left = times(wp, bbr2, bbi2s)
            bs = left
            right = times(wn, ccr2, cci2s)
            cc = times(cmul_const(wn, *lam_t), ccr2, cci2s)
        bbar = jnp.where(lo, bbr2, bbi2s)
        return left * sgn_lr, right, bs, cc * sgn_lr, lam_t, bbar

    lf, rf, bsf, ccf, lamt_f, bbar_f = direction(0)
    lb, rb, bsb, ccb, lamt_b, bbar_b = direction(1)

    s_idx = lax.broadcasted_iota(jnp.int32, (W, W), 0) // SSM_CH
    t_idx = lax.broadcasted_iota(jnp.int32, (W, W), 1) // SSM_CH
    a_t = (jnp.where(t_idx >= s_idx, _nt(lf.astype(BF16), rf.astype(BF16)), 0.0)
           + jnp.where(s_idx >= t_idx, _nt(lb.astype(BF16), rb.astype(BF16)), 0.0))
    w1 = jnp.concatenate([a_t, bsf, swap(bsf), bsb, swap(bsb)], axis=1).astype(BF16)
    r_scr[...] = _nn(ut_ref[0], w1)

    def chain(row0, nb, nchunks, col, lam_t, h, reverse):
        a1, a2s = lam_t
        hs = swap(h)
        order = range(nchunks - 1, -1, -1) if reverse else range(nchunks)
        hcol = slice(LANES, 2 * LANES) if reverse else slice(0, LANES)
        for c in order:
            rows = slice(row0 + c * nb, row0 + (c + 1) * nb)
            hin_scr[rows, hcol] = h
            s = r_scr[rows, col:col + LANES]
            ss = r_scr[rows, col + LANES:col + 2 * LANES]
            h, hs = h * a1 + hs * a2s + s, hs * a1 - h * a2s + ss
        return h

    zero = jnp.zeros((ctx_batch, LANES), F32)
    chain(0, ctx_batch, ctx_chunks, W, lamt_f, zero, False)
    chain(0, ctx_batch, ctx_chunks, W + 2 * LANES, lamt_b, zero, True)
    s_tiled = tile_time(srow)
    first_tok = jnp.where(s_tiled == 0.0, bbar_f, 0.0).astype(BF16)
    last_tok = jnp.where(s_tiled == float(T - 1), bbar_b, 0.0).astype(BF16)
    last_rows = (ctx_chunks - 1) * ctx_batch
    hf_ref[0, :, 0:LANES] = _nn(ut_ref[0, 0:ctx_batch, :], first_tok)
    hf_ref[0, :, LANES:2 * LANES] = _nn(ut_ref[0, last_rows:last_rows + ctx_batch, :], last_tok)
    lat0 = ctx_batch * ctx_chunks
    h0 = h0_ref[0]
    chain(lat0, lat_batch, lat_chunks, W, lamt_f, h0[:, 0:LANES], False)
    chain(lat0, lat_batch, lat_chunks, W + 2 * LANES, lamt_b, h0[:, LANES:2 * LANES], True)

    cc_cat = jnp.concatenate([ccf, ccb], axis=1).astype(BF16)
    yt_ref[0] = (r_scr[:, 0:W] + _nt(hin_scr[...].astype(BF16), cc_cat)).astype(yt_ref.dtype)


def _ssm(ut, h0, p1, pb, ctx_batch, ctx_chunks, lat_batch, lat_chunks):
    g, n, w = ut.shape
    gs = S5_GROUPS_PER_STEP
    return pl.pallas_call(
        functools.partial(_ssm_kernel, ctx_batch=ctx_batch, ctx_chunks=ctx_chunks,
                          lat_batch=lat_batch, lat_chunks=lat_chunks),
        grid=(g // gs,),
        in_specs=[
            pl.BlockSpec((gs, n, w), lambda i: (i, 0, 0)),
            pl.BlockSpec((gs, lat_batch, 2 * LANES), lambda i: (i, 0, 0)),
            pl.BlockSpec((gs, 8, LANES), lambda i: (i, 0, 0)),
            pl.BlockSpec((gs, 8, SSM_CH, LANES), lambda i: (i, 0, 0, 0)),
        ],
        out_specs=[
            pl.BlockSpec((gs, n, w), lambda i: (i, 0, 0)),
            pl.BlockSpec((gs, ctx_batch, 2 * LANES), lambda i: (i, 0, 0)),
        ],
        out_shape=[jax.ShapeDtypeStruct((g, n, w), BF16),
                   jax.ShapeDtypeStruct((g, ctx_batch, 2 * LANES), F32)],
        scratch_shapes=[pltpu.VMEM((gs, n, w + 4 * LANES), F32), pltpu.VMEM((gs, n, 2 * LANES), F32)],
        compiler_params=pltpu.CompilerParams(dimension_semantics=("parallel",),
                                             vmem_limit_bytes=VMEM_LIMIT),
        name="s5_scan",
    )(ut, h0, p1, pb)


def _tail_kernel(x_ref, oatt_ref, y_ref, u_ref, mod_ref, d_ref, wglu_ref, bglu_ref, wout_ref, g2_ref,
                 wup_ref, cw_ref, cb_ref, wd_ref, o_ref, x1_scr, h2_scr, *, seq_len):
    rows = x_ref.shape[0]
    tf = FFN_SLAB
    mod = mod_ref[0]
    gt1 = mod[2:3]
    sh2 = mod[3:4]
    sc2 = mod[4:5]
    gt2 = mod[5:6]
    for sb in range(rows // MIX_SUB):
        rs = slice(sb * MIX_SUB, (sb + 1) * MIX_SUB)
        y = y_ref[rs, :].astype(F32) + d_ref[...] * u_ref[rs, :]
        z = jax.nn.gelu(y)
        gate = _sigmoid(_nn(z.astype(BF16), wglu_ref[0]) + bglu_ref[...])
        o_ssm = (z * gate).astype(BF16)
        mix = _nn(oatt_ref[rs, :], wout_ref[0, 0:ATT_W, :]) + _nn(o_ssm, wout_ref[0, ATT_W:, :])
        x1 = x_ref[rs, :] + gt1 * mix
        x1_scr[rs, :] = x1
        ms = jnp.mean(x1 * x1, axis=-1, keepdims=True)
        h2 = x1 * lax.rsqrt(ms + EPS) * g2_ref[...]
        h2_scr[rs, :] = (h2 * (1.0 + sc2) + sh2).astype(BF16)

    row8 = lax.broadcasted_iota(jnp.int32, (8, 1), 0)
    h2 = h2_scr[...]

    def conv(up, cw, cb):
        prev = pltpu.roll(up, 1, 0)
        nxt = pltpu.roll(up, rows - 1, 0)
        pp, nn = [], []
        for r in range(0, rows, seq_len):
            e = r + seq_len
            pp += [jnp.where(row8 == 0, 0.0, prev[r:r + 8]), prev[r + 8:e]]
            nn += [nxt[r:e - 8], jnp.where(row8 == 7, 0.0, nxt[e - 8:e])]
        prev = jnp.concatenate(pp, axis=0)
        nxt = jnp.concatenate(nn, axis=0)
        return prev * cw[0:1] + up * cw[1:2] + nxt * cw[2:3] + cb

    acts = []
    for j in range(D_FF // tf):
        cv = slice(j * tf, (j + 1) * tf)
        cg = slice(D_FF + j * tf, D_FF + (j + 1) * tf)
        val = conv(_nn(h2, wup_ref[0, :, cv]), cw_ref[:, cv], cb_ref[:, cv])
        gate = conv(_nn(h2, wup_ref[0, :, cg]), cw_ref[:, cg], cb_ref[:, cg])
        acts.append((gate * _sigmoid(gate) * val).astype(BF16))
    act = jnp.concatenate(acts, axis=1)
    o_ref[...] = x1_scr[...] + gt2 * _nn(act, wd_ref[0])


def _tail(x, oatt, y, u, mods_l, d, wglu, bglu, wout, g2, wup, cw, cb, wdown, layer, seq_len, mod_base):
    n = x.shape[0]
    rows = FFN_ROWS
    per_seq = seq_len // rows if seq_len >= rows else 1
    if mod_base == 0:
        mod_map = lambda i: (0, 0, 0)
    else:
        mod_map = lambda i: (mod_base + i // per_seq, 0, 0)
    row = lambda i: (i, 0)
    const = lambda i: (0, 0)
    lay = lambda i: (layer, 0, 0)
    resident = dict(pipeline_mode=pl.Buffered(1))
    return pl.pallas_call(
        functools.partial(_tail_kernel, seq_len=seq_len),
        grid=(n // rows,),
        in_specs=[
            pl.BlockSpec((rows, D_MODEL), row),
            pl.BlockSpec((rows, ATT_W), row),
            pl.BlockSpec((rows, SSM_W), row),
            pl.BlockSpec((rows, SSM_W), row),
            pl.BlockSpec((1, N_MOD, D_MODEL), mod_map),
            pl.BlockSpec((1, SSM_W), const),
            pl.BlockSpec((1, SSM_W, SSM_W), lay, **resident),
            pl.BlockSpec((1, SSM_W), const),
            pl.BlockSpec((1, D_MODEL, D_MODEL), lay, **resident),
            pl.BlockSpec((1, D_MODEL), const),
            pl.BlockSpec((1, D_MODEL, 2 * D_FF), lay, **resident),
            pl.BlockSpec((3, 2 * D_FF), const),
            pl.BlockSpec((1, 2 * D_FF), const),
            pl.BlockSpec((1, D_FF, D_MODEL), lay, **resident),
        ],
        out_specs=pl.BlockSpec((rows, D_MODEL), row),
        out_shape=jax.ShapeDtypeStruct((n, D_MODEL), F32),
        scratch_shapes=[pltpu.VMEM((rows, D_MODEL), F32), pltpu.VMEM((rows, D_MODEL), BF16)],
        compiler_params=pltpu.CompilerParams(dimension_semantics=("parallel",),
                                             vmem_limit_bytes=VMEM_LIMIT),
        name="layer_tail",
    )(x, oatt, y, u, mods_l, d, wglu, bglu, wout, g2, wup, cw, cb, wdown)


def _rope_tables(n_tokens):
    pos = np.arange(n_tokens)
    row = (pos // GRID_W).astype(np.float64)
    col = (pos % GRID_W).astype(np.float64)
    nf = ROPE_AXIS // 2
    inv_freq = 1.0 / (ROPE_THETA ** (np.arange(nf, dtype=np.float64) / nf))
    lane = np.arange(LANES)
    freq = inv_freq[lane % nf]
    is_col = (lane % D_SUB) >= ROPE_AXIS
    ang = np.where(is_col[None, :], col[:, None], row[:, None]) * freq[None, :]
    second = (lane % ROPE_AXIS) >= nf
    cos = np.cos(ang)
    sin = np.sin(ang)
    sina = np.where(second[None, :], 0.0, -sin)
    sinb = np.where(second[None, :], sin, 0.0)
    return tuple(jnp.asarray(t, dtype=F32) for t in (cos, sina, sinb))


TIME_BLOCKS = CHUNK // GROUPS_PER_BLOCK


def _chunk_layout_passes(ctx_batch, ctx_len, lat_batch, lat_len):
    return ((0, ctx_batch, ctx_len, 0), (1, lat_batch, lat_len, ctx_batch * (ctx_len // CHUNK)))


def _row_permutation(batch):
    n = batch * CHUNK
    p = np.zeros((n, n), np.float32)
    b, s = np.meshgrid(np.arange(batch), np.arange(CHUNK), indexing="ij")
    p[(s * batch + b).ravel(), (b * CHUNK + s).ravel()] = 1.0
    return p


def _lane_permutation():
    n = GROUPS_PER_BLOCK * LANES
    q = np.zeros((n, n), np.float32)
    t, g, ch = np.meshgrid(np.arange(GROUPS_PER_BLOCK), np.arange(GROUPS_PER_BLOCK), np.arange(SSM_CH),
                           indexing="ij")
    q[(t * LANES + g * SSM_CH + ch).ravel(), (g * LANES + t * SSM_CH + ch).ravel()] = 1.0
    return q


def _to_chunks_kernel(uc_ref, ul_ref, pc_ref, pl_ref, q_ref, o_ref, v_scr, *, passes):
    for idx, batch, seq_len, row0 in passes:
        u_ref = (uc_ref, ul_ref)[idx]
        p_ref = (pc_ref, pl_ref)[idx]
        for c in range(seq_len // CHUNK):
            x = jnp.concatenate([u_ref[b * seq_len + c * CHUNK:b * seq_len + (c + 1) * CHUNK, :]
                                 for b in range(batch)], axis=0)
            r = _nn(p_ref[...], x.astype(BF16))
            rows = slice(row0 + c * batch, row0 + (c + 1) * batch)
            for s in range(CHUNK):
                j, t = divmod(s, GROUPS_PER_BLOCK)
                v_scr[j, rows, t * LANES:(t + 1) * LANES] = r[s * batch:(s + 1) * batch]
    for j in range(TIME_BLOCKS):
        w = _nn(v_scr[j].astype(BF16), q_ref[...])
        for g in range(GROUPS_PER_BLOCK):
            o_ref[g, :, j * LANES:(j + 1) * LANES] = w[:, g * LANES:(g + 1) * LANES].astype(o_ref.dtype)


def _from_chunks_kernel(y_ref, pc_ref, pl_ref, q_ref, oc_ref, ol_ref, v_scr, *, passes):
    for j in range(TIME_BLOCKS):
        yj = jnp.concatenate([y_ref[g, :, j * LANES:(j + 1) * LANES] for g in range(GROUPS_PER_BLOCK)], axis=1)
        v_scr[j] = _nn(yj, q_ref[...])
    for idx, batch, seq_len, row0 in passes:
        o_ref = (oc_ref, ol_ref)[idx]
        p_ref = (pc_ref, pl_ref)[idx]
        for c in range(seq_len // CHUNK):
            rows = slice(row0 + c * batch, row0 + (c + 1) * batch)
            x = jnp.concatenate([v_scr[s // GROUPS_PER_BLOCK, rows,
                                       (s % GROUPS_PER_BLOCK) * LANES:(s % GROUPS_PER_BLOCK + 1) * LANES]
                                 for s in range(CHUNK)], axis=0)
            r = _nn(p_ref[...], x.astype(BF16))
            for b in range(batch):
                o_ref[b * seq_len + c * CHUNK:b * seq_len + (c + 1) * CHUNK, :] = (
                    r[b * CHUNK:(b + 1) * CHUNK].astype(o_ref.dtype))


def _to_chunks(uc, ul, ctx_batch, ctx_len, lat_batch, lat_len):
    passes = _chunk_layout_passes(ctx_batch, ctx_len, lat_batch, lat_len)
    rows = ctx_batch * (ctx_len // CHUNK) + lat_batch * (lat_len // CHUNK)
    pc = jnp.asarray(_row_permutation(ctx_batch), dtype=BF16)
    pn = jnp.asarray(_row_permutation(lat_batch), dtype=BF16)
    q = jnp.asarray(_lane_permutation(), dtype=BF16)
    const = lambda j: (0, 0)
    return pl.pallas_call(
        functools.partial(_to_chunks_kernel, passes=passes),
        grid=(SSM_GROUPS // GROUPS_PER_BLOCK,),
        in_specs=[pl.BlockSpec((uc.shape[0], LANES), lambda j: (0, j)),
                  pl.BlockSpec((ul.shape[0], LANES), lambda j: (0, j)),
                  pl.BlockSpec(pc.shape, const), pl.BlockSpec(pn.shape, const), pl.BlockSpec(q.shape, const)],
        out_specs=pl.BlockSpec((GROUPS_PER_BLOCK, rows, SSM_CH * CHUNK), lambda j: (j, 0, 0)),
        out_shape=jax.ShapeDtypeStruct((SSM_GROUPS, rows, SSM_CH * CHUNK), BF16),
        scratch_shapes=[pltpu.VMEM((TIME_BLOCKS, rows, GROUPS_PER_BLOCK * LANES), F32)],
        compiler_params=pltpu.CompilerParams(dimension_semantics=("parallel",),
                                             vmem_limit_bytes=VMEM_LIMIT),
        name="to_chunks",
    )(uc, ul, pc, pn, q)


def _from_chunks(yt, ctx_batch, ctx_len, lat_batch, lat_len):
    passes = _chunk_layout_passes(ctx_batch, ctx_len, lat_batch, lat_len)
    rows = yt.shape[1]
    n_ctx = ctx_batch * ctx_len
    n_lat = lat_batch * lat_len
    pc = jnp.asarray(_row_permutation(ctx_batch).T, dtype=BF16)
    pn = jnp.asarray(_row_permutation(lat_batch).T, dtype=BF16)
    q = jnp.asarray(_lane_permutation().T, dtype=BF16)
    const = lambda j: (0, 0)
    return pl.pallas_call(
        functools.partial(_from_chunks_kernel, passes=passes),
        grid=(SSM_GROUPS // GROUPS_PER_BLOCK,),
        in_specs=[pl.BlockSpec((GROUPS_PER_BLOCK, rows, SSM_CH * CHUNK), lambda j: (j, 0, 0)),
                  pl.BlockSpec(pc.shape, const), pl.BlockSpec(pn.shape, const), pl.BlockSpec(q.shape, const)],
        out_specs=[pl.BlockSpec((n_ctx, LANES), lambda j: (0, j)),
                   pl.BlockSpec((n_lat, LANES), lambda j: (0, j))],
        out_shape=[jax.ShapeDtypeStruct((n_ctx, SSM_W), BF16), jax.ShapeDtypeStruct((n_lat, SSM_W), BF16)],
        scratch_shapes=[pltpu.VMEM((TIME_BLOCKS, rows, GROUPS_PER_BLOCK * LANES), F32)],
        compiler_params=pltpu.CompilerParams(dimension_semantics=("parallel",),
                                             vmem_limit_bytes=VMEM_LIMIT),
        name="from_chunks",
    )(yt, pc, pn, q)


def _dup_lanes(a):
    return jnp.concatenate([a, a], axis=-1)


def kernel(x_prompt, x_sample, cache_k, cache_v, state_ssm, c, c_ctx, w_mod, b_mod, g_norm1, w_in, q_norm, k_norm, lambda_q1, lambda_k1, lambda_q2, lambda_k2, subln_g, ssm_lambda_re, ssm_lambda_im, ssm_log_step, ssm_b_re, ssm_b_im, ssm_c_re, ssm_c_im, ssm_d, w_glu, b_glu, w_out, g_norm2, w_up, conv_w, conv_b, w_down):
    nb_ctx, len_ctx, _ = x_prompt.shape
    nb_lat, len_lat, _ = x_sample.shape
    past = cache_k.shape[2]

    n_cond = 16
    cond = jnp.zeros((n_cond, D_MODEL), F32).at[0].set(c_ctx).at[1:1 + nb_lat].set(c)
    mods = _modulation(cond, w_mod, b_mod).reshape(DEPTH, n_cond, N_MOD, D_MODEL)

    gidx = np.arange(ATT_W) // D_SUB
    gmat = jnp.asarray(np.where(gidx[:, None] == gidx[None, :], 1.0 / D_SUB, 0.0), dtype=BF16)
    cache_kt = cache_k.transpose(0, 1, 3, 4, 5, 2).reshape(nb_lat, DEPTH, ATT_W, past)
    cache_vr = cache_v.reshape(nb_lat, DEPTH, past * N_HEADS, D_V)
    rope_tabs = _rope_tables(len_lat)
    win_b = w_in.astype(BF16)
    wout_b = w_out.astype(BF16)
    wglu_b = w_glu.astype(BF16)
    wup_b = w_up.astype(BF16)
    wdown_b = w_down.astype(BF16)

    xp = x_prompt.reshape(nb_ctx * len_ctx, D_MODEL)
    xs = x_sample.reshape(nb_lat * len_lat, D_MODEL)
    ss = []
    kv_out = None
    for l in range(DEPTH):
        lam_init = 0.8 - 0.6 * math.exp(-0.3 * l)
        mods_l = mods[l]
        g1 = g_norm1[l].reshape(1, D_MODEL)
        g2 = g_norm2[l].reshape(1, D_MODEL)
        qg = jnp.tile(q_norm[l], ATT_W // D_SUB).reshape(1, ATT_W)
        kg = jnp.tile(k_norm[l], ATT_W // D_SUB).reshape(1, ATT_W)
        sg = jnp.tile(subln_g[l], N_HEADS).reshape(1, ATT_W)
        lamp = jnp.stack([lambda_q1[l], lambda_k1[l], lambda_q2[l], lambda_k2[l]])
        d = ssm_d[l].reshape(1, SSM_W)
        bglu = b_glu[l].reshape(1, SSM_W)
        cb = conv_b[l].reshape(1, 2 * D_FF)
        cw = conv_w[l]

        zeros = jnp.zeros((SSM_GROUPS, SSM_P), F32)
        step = jnp.broadcast_to(ssm_log_step[l][:, :, None], (2, SSM_GROUPS, SSM_P))
        p1 = _dup_lanes(jnp.stack([ssm_lambda_re[l, 0], ssm_lambda_im[l, 0], step[0],
                                   ssm_lambda_re[l, 1], ssm_lambda_im[l, 1], step[1], zeros, zeros], axis=1))
        bt = lambda a: a.transpose(0, 2, 1)
        pb = _dup_lanes(jnp.stack([bt(ssm_b_re[l, 0]), bt(ssm_b_im[l, 0]), ssm_c_re[l, 0], ssm_c_im[l, 0],
                                   bt(ssm_b_re[l, 1]), bt(ssm_b_im[l, 1]), ssm_c_re[l, 1], ssm_c_im[l, 1]], axis=1))
        h0 = state_ssm[:, l].transpose(2, 0, 1, 4, 3).reshape(SSM_GROUPS, nb_lat, 4 * SSM_P)

        qc, kc_new, vc_new, uc, kt, vt = _premix(xp, mods_l, g1, win_b, l, gmat, qg, kg, None, len_ctx, 0,
                                                 carry=kv_out)
        ql, kl, vl, ul = _premix(xs, mods_l, g1, win_b, l, gmat, qg, kg, rope_tabs, len_lat, 1)

        oc = _attention(qc, kc_new, vc_new, lamp, sg, None, l, nb_ctx, len_ctx, lam_init)
        ol = _attention(ql, kl, vl, lamp, sg, (cache_kt, cache_vr), l, nb_lat, len_lat, lam_init)

        ut = _to_chunks(uc, ul, nb_ctx, len_ctx, nb_lat, len_lat)
        yt, hfin = _ssm(ut, h0, p1, pb, nb_ctx, len_ctx // CHUNK, nb_lat, len_lat // CHUNK)
        yc, yl = _from_chunks(yt, nb_ctx, len_ctx, nb_lat, len_lat)

        xp = _tail(xp, oc, yc, uc, mods_l, d, wglu_b, bglu, wout_b, g2, wup_b, cw, cb, wdown_b, l, len_ctx, 0)
        xs = _tail(xs, ol, yl, ul, mods_l, d, wglu_b, bglu, wout_b, g2, wup_b, cw, cb, wdown_b, l, len_lat, 1)

        kv_out = (kt, vt)
        ss.append(hfin.reshape(SSM_GROUPS, nb_ctx, 2, 2, SSM_P).transpose(1, 2, 0, 4, 3))

    new_k = kv_out[0].reshape(nb_ctx, DEPTH, N_HEADS, 2, D_SUB, len_ctx).transpose(0, 1, 5, 2, 3, 4)
    new_v = kv_out[1].reshape(nb_ctx, DEPTH, len_ctx, N_HEADS, D_V)
    return (xp.reshape(nb_ctx, len_ctx, D_MODEL), xs.reshape(nb_lat, len_lat, D_MODEL),
            new_k, new_v, jnp.stack(ss, axis=1))
```

```python
import functools
import math

import numpy as np
import jax
import jax.numpy as jnp
from jax import lax
from jax.experimental import pallas as pl
from jax.experimental.pallas import tpu as pltpu

F32 = jnp.float32
BF16 = jnp.bfloat16

D_MODEL = 1024
DEPTH = 2
GRID_W = 64
ATT_W = 512
SSM_W = 512
N_HEADS = 4
D_V = 128
D_SUB = 64
ROPE_AXIS = 32
ROPE_THETA = 10000.0
SSM_CH = 16
SSM_GROUPS = 32
SSM_P = 64
D_FF = 2048
N_MOD = 6
EPS = 1e-6

LANES = 128
GROUPS_PER_BLOCK = LANES // SSM_CH
CHUNK = 32
S5_GROUPS_PER_STEP = 4
ATTN_TILE = 1024
ATTN_SUB = 256
MIX_TILE = 512
MIX_SUB = 256
FFN_ROWS = 1024
FFN_SLAB = 512
MOD_COLS = 1536
VMEM_LIMIT = 56 * 1024 * 1024


def _nt(a, b):
    return lax.dot_general(a, b, (((1,), (1,)), ((), ())), preferred_element_type=F32)


def _nn(a, b):
    return jnp.dot(a, b, preferred_element_type=F32)


def _sigmoid(x):
    return 1.0 / (1.0 + jnp.exp(-x))


def _mod_kernel(c_ref, w_ref, b_ref, o_ref):
    c = c_ref[...]
    a = (c * _sigmoid(c)).astype(BF16)
    o_ref[0] = _nn(a, w_ref[0].astype(BF16)) + b_ref[0]


def _modulation(cond, w_mod, b_mod):
    nb = cond.shape[0]
    ncol = N_MOD * D_MODEL
    return pl.pallas_call(
        _mod_kernel,
        grid=(DEPTH, ncol // MOD_COLS),
        in_specs=[
            pl.BlockSpec((nb, D_MODEL), lambda l, j: (0, 0)),
            pl.BlockSpec((1, D_MODEL, MOD_COLS), lambda l, j: (l, 0, j)),
            pl.BlockSpec((1, 1, MOD_COLS), lambda l, j: (l, 0, j)),
        ],
        out_specs=pl.BlockSpec((1, nb, MOD_COLS), lambda l, j: (l, 0, j)),
        out_shape=jax.ShapeDtypeStruct((DEPTH, nb, ncol), F32),
        compiler_params=pltpu.CompilerParams(vmem_limit_bytes=VMEM_LIMIT),
        name="adaln_mod",
    )(cond, w_mod, b_mod.reshape(DEPTH, 1, ncol))


def _premix_kernel(x_ref, mod_ref, g1_ref, win_ref, gmat_ref, qg_ref, kg_ref, *rest, rope):
    kt_ref = vt_ref = None
    if rope:
        cos_ref, sina_ref, sinb_ref, q_ref, k_ref, v_ref, u_ref = rest
    else:
        q_ref, k_ref, v_ref, u_ref, kt_ref, vt_ref = rest[-6:]
    mod = mod_ref[0]
    sh1 = mod[0:1]
    sc1 = mod[1:2]

    def head_norm(t, g):
        msq = _nn((t * t).astype(BF16), gmat_ref[...])
        return t * lax.rsqrt(msq + EPS) * g

    def rotate(t, rs):
        outs = []
        for j in range(ATT_W // LANES):
            s = t[:, j * LANES:(j + 1) * LANES]
            outs.append(s * cos_ref[rs, :] + pltpu.roll(s, LANES - 16, 1) * sina_ref[rs, :]
                        + pltpu.roll(s, 16, 1) * sinb_ref[rs, :])
        return jnp.concatenate(outs, axis=1)

    for sb in range(x_ref.shape[0] // MIX_SUB):
        rs = slice(sb * MIX_SUB, (sb + 1) * MIX_SUB)
        x = x_ref[rs, :]
        ms = jnp.mean(x * x, axis=-1, keepdims=True)
        h = x * lax.rsqrt(ms + EPS) * g1_ref[...]
        h = h * (1.0 + sc1) + sh1
        qkvu = _nn(h.astype(BF16), win_ref[0])
        q = head_norm(qkvu[:, 0:ATT_W], qg_ref[...])
        k = head_norm(qkvu[:, ATT_W:2 * ATT_W], kg_ref[...])
        if rope:
            q = rotate(q, rs)
            k = rotate(k, rs)
        q_ref[rs, :] = (q * (D_SUB ** -0.5)).astype(q_ref.dtype)
        k_ref[rs, :] = k.astype(k_ref.dtype)
        v = qkvu[:, 2 * ATT_W:3 * ATT_W]
        if kt_ref is not None:
            kt = k.T
            for slot in range(kt_ref.shape[1]):
                kt_ref[sb, slot] = kt
                for hd in range(N_HEADS):
                    vt_ref[sb, slot, pl.ds(hd, MIX_SUB, stride=N_HEADS), :] = v[:, hd * D_V:(hd + 1) * D_V]
        v_ref[rs, :] = v.astype(v_ref.dtype)
        u_ref[rs, :] = qkvu[:, 3 * ATT_W:].astype(u_ref.dtype)


def _premix(x, mods_l, g1, win, layer, gmat, qg, kg, rope_tabs, seq_len, mod_base, carry=None):
    n = x.shape[0]
    tm = MIX_TILE
    per_seq = seq_len // tm
    rope = rope_tabs is not None
    assert rope or seq_len == MIX_SUB
    if mod_base == 0:
        mod_map = lambda i: (0, 0, 0)
    else:
        mod_map = lambda i: (mod_base + i // per_seq, 0, 0)
    row = lambda i: (i, 0)
    const = lambda i: (0, 0)
    in_specs = [
        pl.BlockSpec((tm, D_MODEL), row),
        pl.BlockSpec((1, N_MOD, D_MODEL), mod_map),
        pl.BlockSpec((1, D_MODEL), const),
        pl.BlockSpec((1, D_MODEL, 4 * ATT_W), lambda i: (layer, 0, 0)),
        pl.BlockSpec((ATT_W, ATT_W), const),
        pl.BlockSpec((1, ATT_W), const),
        pl.BlockSpec((1, ATT_W), const),
    ]
    args = [x, mods_l, g1, win, gmat, qg, kg]
    if rope:
        tab = lambda i: (i % per_seq, 0)
        in_specs += [pl.BlockSpec((tm, LANES), tab)] * 3
        args += list(rope_tabs)
    out = pl.BlockSpec((tm, ATT_W), row)
    out_specs = [out, out, out, out]
    out_shape = [jax.ShapeDtypeStruct((n, ATT_W), BF16),
                 jax.ShapeDtypeStruct((n, ATT_W), BF16),
                 jax.ShapeDtypeStruct((n, ATT_W), BF16),
                 jax.ShapeDtypeStruct((n, SSM_W), F32)]
    aliases = {}
    if not rope:
        nseq = tm // seq_len
        slots, first = (DEPTH, 0) if carry is None else (1, layer)
        out_specs.append(pl.BlockSpec((nseq, slots, ATT_W, seq_len), lambda i: (i, first, 0, 0)))
        out_shape.append(jax.ShapeDtypeStruct((n // seq_len, DEPTH, ATT_W, seq_len), F32))
        out_specs.append(pl.BlockSpec((nseq, slots, seq_len * N_HEADS, D_V), lambda i: (i, first, 0, 0)))
        out_shape.append(jax.ShapeDtypeStruct((n // seq_len, DEPTH, seq_len * N_HEADS, D_V), F32))
        if carry is not None:
            aliases = {len(args): 4, len(args) + 1: 5}
            in_specs += [pl.BlockSpec(memory_space=pl.ANY)] * 2
            args += list(carry)
    return pl.pallas_call(
        functools.partial(_premix_kernel, rope=rope),
        grid=(n // tm,),
        in_specs=in_specs,
        out_specs=out_specs,
        out_shape=out_shape,
        input_output_aliases=aliases,
        compiler_params=pltpu.CompilerParams(dimension_semantics=("parallel",),
                                             vmem_limit_bytes=VMEM_LIMIT),
        name="premix_rope" if rope else "premix",
    )(*args)


def _attn_kernel(q_ref, k_ref, v_ref, lamp_ref, sg_ref, *rest, lam_init, has_cache):
    if has_cache:
        kc_ref, vc_ref, o_ref = rest
    else:
        (o_ref,) = rest
    tq = q_ref.shape[0]
    lp = lamp_ref[...]
    l1 = jnp.sum(lp[0:1] * lp[1:2], axis=-1, keepdims=True)
    l2 = jnp.sum(lp[2:3] * lp[3:4], axis=-1, keepdims=True)
    lam = jnp.exp(l1) - jnp.exp(l2) + lam_init
    first = lax.broadcasted_iota(jnp.int32, (1, LANES), 1) < D_SUB
    tsub = min(ATTN_SUB, tq)
    for h in range(N_HEADS):
        sl = slice(h * D_V, (h + 1) * D_V)
        kn = k_ref[:, sl]
        vn = v_ref[:, sl]
        if has_cache:
            ncache = kc_ref.shape[3]
            kct = kc_ref[0, 0, sl, :].astype(BF16)
            vct = vc_ref[0, 0, pl.ds(h, ncache, stride=N_HEADS), :].astype(BF16)
        for r0 in range(0, tq, tsub):
            qh = q_ref[r0:r0 + tsub, sl]
            zero = jnp.zeros_like(qh)
            qq = jnp.concatenate([jnp.where(first, qh, zero), jnp.where(first, zero, qh)], axis=0)
            if has_cache:
                sc = jnp.concatenate([_nn(qq, kct), _nt(qq, kn)], axis=1)
            else:
                sc = _nt(qq, kn)
            m = jnp.max(sc, axis=-1, keepdims=True)
            e = jnp.exp(sc - m)
            ssum = jnp.sum(e, axis=-1, keepdims=True)
            eb = e.astype(BF16)
            if has_cache:
                pv = _nn(eb[:, :ncache], vct) + _nn(eb[:, ncache:], vn)
            else:
                pv = _nn(eb, vn)
            on = pv * (1.0 / ssum)
            o = on[:tsub] - lam * on[tsub:]
            ms = jnp.mean(o * o, axis=-1, keepdims=True)
            o_ref[r0:r0 + tsub, sl] = (o * lax.rsqrt(ms + EPS) * sg_ref[:, sl]
                                       * (1.0 - lam_init)).astype(o_ref.dtype)


def _attention(q, k, v, lamp, sg, cache, layer, batch, seq_len, lam_init):
    n = q.shape[0]
    tq = min(ATTN_TILE, seq_len)
    nq = seq_len // tq
    has_cache = cache is not None
    in_specs = [
        pl.BlockSpec((tq, ATT_W), lambda b, i: (b * nq + i, 0)),
        pl.BlockSpec((seq_len, ATT_W), lambda b, i: (b, 0)),
        pl.BlockSpec((seq_len, ATT_W), lambda b, i: (b, 0)),
        pl.BlockSpec((4, D_SUB), lambda b, i: (0, 0)),
        pl.BlockSpec((1, ATT_W), lambda b, i: (0, 0)),
    ]
    args = [q, k, v, lamp, sg]
    if has_cache:
        past = cache[0].shape[3]
        in_specs += [pl.BlockSpec((1, 1, ATT_W, past), lambda b, i: (b, layer, 0, 0)),
                     pl.BlockSpec((1, 1, past * N_HEADS, D_V), lambda b, i: (b, layer, 0, 0))]
        args += list(cache)
    return pl.pallas_call(
        functools.partial(_attn_kernel, lam_init=lam_init, has_cache=has_cache),
        grid=(batch, nq),
        in_specs=in_specs,
        out_specs=pl.BlockSpec((tq, ATT_W), lambda b, i: (b * nq + i, 0)),
        out_shape=jax.ShapeDtypeStruct((n, ATT_W), BF16),
        compiler_params=pltpu.CompilerParams(dimension_semantics=("parallel", "parallel"),
                                             vmem_limit_bytes=VMEM_LIMIT),
        name="attn_cache" if has_cache else "attn",
    )(*args)


def _ssm_kernel(ut_ref, h0_ref, p1_ref, pb_ref, yt_ref, hf_ref, r_scr, hin_scr, **sizes):
    for gi in range(ut_ref.shape[0]):
        one = lambda ref, gi=gi: ref.at[pl.ds(gi, 1)]
        _ssm_group(one(ut_ref), one(h0_ref), one(p1_ref), one(pb_ref), one(yt_ref), one(hf_ref),
                   r_scr.at[gi], hin_scr.at[gi], **sizes)


def _ssm_group(ut_ref, h0_ref, p1_ref, pb_ref, yt_ref, hf_ref, r_scr, hin_scr, *,
               ctx_batch, ctx_chunks, lat_batch, lat_chunks):
    T = CHUNK
    W = SSM_CH * T
    lane = lax.broadcasted_iota(jnp.int32, (1, LANES), 1)
    lo = lane < SSM_P
    sgn_lr = jnp.where(lo, 1.0, -1.0).astype(F32)
    sgn_rl = -sgn_lr
    srow = lax.broadcasted_iota(jnp.int32, (T, LANES), 0).astype(F32)

    def swap(a):
        return pltpu.roll(a, SSM_P, 1)

    def tile_time(a):
        return jnp.concatenate([jnp.broadcast_to(a[i:i + 1, :], (SSM_CH, LANES)) for i in range(T)], axis=0)

    def rep_chan(a):
        return jnp.concatenate([a] * T, axis=0)

    def pair_pack(y):
        ys = swap(y)
        return jnp.where(lo, y, ys), jnp.where(lo, ys, y) * sgn_rl

    def cmul_const(x, yr2, yi2s):
        return x * yr2 + swap(x) * yi2s

    p1 = p1_ref[0]
    pb = pb_ref[0]

    def direction(d):
        lre2 = p1[3 * d + 0:3 * d + 1]
        lim2 = p1[3 * d + 1:3 * d + 2]
        ls2 = p1[3 * d + 2:3 * d + 3]
        bre2, bim2, cre2, cim2 = pb[4 * d + 0], pb[4 * d + 1], pb[4 * d + 2], pb[4 * d + 3]
        delta = jnp.exp(ls2)
        xr = lre2 * delta
        th = lim2 * delta

        def cpow_const(k):
            return jnp.exp(k * xr) * jnp.where(lo, jnp.cos(k * th), jnp.sin(k * th))

        mag1 = jnp.exp(xr)
        lbr2 = mag1 * jnp.cos(th)
        lbi2 = mag1 * jnp.sin(th)
        den = lre2 * lre2 + lim2 * lim2
        nr = lbr2 - 1.0
        cr2 = (nr * lre2 + lbi2 * lim2) / den
        ci2 = (lbi2 * lre2 - nr * lim2) / den
        bbr16 = cr2 * bre2 - ci2 * bim2
        bbi16s = (cr2 * bim2 + ci2 * bre2) * sgn_rl
        bbr2 = rep_chan(bbr16)
        bbi2s = rep_chan(bbi16s)
        ccr2 = rep_chan(cre2)
        cci2s = rep_chan(cim2 * sgn_rl)

        def cpow_rows(k):
            ang = k * th
            return jnp.exp(k * xr) * jnp.where(lo, jnp.cos(ang), jnp.sin(ang))

        wp = cpow_rows(srow)
        wrev = cpow_rows(float(T - 1) - srow)

        def times(x, c1, c2s):
            return tile_time(x) * c1 + tile_time(swap(x)) * c2s

        lam_1 = pair_pack(cpow_const(1.0))
        lam_t = pair_pack(cpow_const(float(T)))
        if d == 0:
            clag = times(wp, ccr2, cci2s)
            bs = times(wrev, bbr2, bbi2s)
            cc = times(cmul_const(wp, *lam_1), ccr2, cci2s)
        else:
            clag = times(wrev, ccr2, cci2s)
            bs = times(wp, bbr2, bbi2s)
            cc = times(cmul_const(wrev, *lam_1), ccr2, cci2s)
        kt = _nt(jnp.where(lo, bbr16, -bbi16s).astype(BF16), clag.astype(BF16))
        bbar = jnp.where(lo, bbr2, bbi2s)
        return kt, bs, cc * sgn_lr, lam_t, bbar

    kt_f, bsf, ccf, lamt_f, bbar_f = direction(0)
    kt_b, bsb, ccb, lamt_b, bbar_b = direction(1)

    sub = LANES // SSM_CH
    right_f = [kt_f if b == 0 else pltpu.roll(kt_f, b * SSM_CH, 1) for b in range(sub)]
    left_b = [kt_b if b == 0 else pltpu.roll(kt_b, W - b * SSM_CH, 1) for b in range(sub)]

    def ring(x, lanes):
        return x if lanes % W == 0 else jnp.concatenate([x[:, W - lanes % W:], x[:, :W - lanes % W]], axis=1)

    blocks = []
    for s in range(T):
        a, b = divmod(s, sub)
        a2, b2 = divmod(T - 1 - s, sub)
        fwd = ring(right_f[b], a * LANES)
        bwd = ring(left_b[b2], W - a2 * LANES)
        first, end = s * SSM_CH, (s + 1) * SSM_CH
        cols = []
        for v in range(W // LANES):
            lo_l, hi_l = v * LANES, (v + 1) * LANES
            fv, bv = fwd[:, lo_l:hi_l], bwd[:, lo_l:hi_l]
            parts = []
            if lo_l >= first:
                parts.append(fv)
            elif hi_l > first:
                parts.append(jnp.where(lane >= first - lo_l, fv, 0.0))
            if hi_l <= end:
                parts.append(bv)
            elif lo_l < end:
                parts.append(jnp.where(lane < end - lo_l, bv, 0.0))
            cols.append(parts[0] if len(parts) == 1 else parts[0] + parts[1])
        blocks.append(jnp.concatenate(cols, axis=1))
    a_t = jnp.concatenate(blocks, axis=0)
    w1 = jnp.concatenate([a_t, bsf, swap(bsf), bsb, swap(bsb)], axis=1).astype(BF16)
    r_scr[...] = _nn(ut_ref[0], w1)

    def chain(row0, nb, nchunks, col, lam_t, h, reverse):
        a1, a2s = lam_t
        hs = swap(h)
        order = range(nchunks - 1, -1, -1) if reverse else range(nchunks)
        hcol = slice(LANES, 2 * LANES) if reverse else slice(0, LANES)
        for c in order:
            rows = slice(row0 + c * nb, row0 + (c + 1) * nb)
            hin_scr[rows, hcol] = h
            s = r_scr[rows, col:col + LANES]
            ss = r_scr[rows, col + LANES:col + 2 * LANES]
            h, hs = h * a1 + hs * a2s + s, hs * a1 - h * a2s + ss
        return h

    zero = jnp.zeros((ctx_batch, LANES), F32)
    chain(0, ctx_batch, ctx_chunks, W, lamt_f, zero, False)
    chain(0, ctx_batch, ctx_chunks, W + 2 * LANES, lamt_b, zero, True)
    s_tiled = tile_time(srow)
    first_tok = jnp.where(s_tiled == 0.0, bbar_f, 0.0).astype(BF16)
    last_tok = jnp.where(s_tiled == float(T - 1), bbar_b, 0.0).astype(BF16)
    last_rows = (ctx_chunks - 1) * ctx_batch
    hf_ref[0, :, 0:LANES] = _nn(ut_ref[0, 0:ctx_batch, :], first_tok)
    hf_ref[0, :, LANES:2 * LANES] = _nn(ut_ref[0, last_rows:last_rows + ctx_batch, :], last_tok)
    lat0 = ctx_batch * ctx_chunks
    h0 = h0_ref[0]
    chain(lat0, lat_batch, lat_chunks, W, lamt_f, h0[:, 0:LANES], False)
    chain(lat0, lat_batch, lat_chunks, W + 2 * LANES, lamt_b, h0[:, LANES:2 * LANES], True)

    cc_cat = jnp.concatenate([ccf, ccb], axis=1).astype(BF16)
    yt_ref[0] = (r_scr[:, 0:W] + _nt(hin_scr[...].astype(BF16), cc_cat)).astype(yt_ref.dtype)


def _ssm(ut, h0, p1, pb, ctx_batch, ctx_chunks, lat_batch, lat_chunks):
    g, n, w = ut.shape
    gs = S5_GROUPS_PER_STEP
    return pl.pallas_call(
        functools.partial(_ssm_kernel, ctx_batch=ctx_batch, ctx_chunks=ctx_chunks,
                          lat_batch=lat_batch, lat_chunks=lat_chunks),
        grid=(g // gs,),
        in_specs=[
            pl.BlockSpec((gs, n, w), lambda i: (i, 0, 0)),
            pl.BlockSpec((gs, lat_batch, 2 * LANES), lambda i: (i, 0, 0)),
            pl.BlockSpec((gs, 8, LANES), lambda i: (i, 0, 0)),
            pl.BlockSpec((gs, 8, SSM_CH, LANES), lambda i: (i, 0, 0, 0)),
        ],
        out_specs=[
            pl.BlockSpec((gs, n, w), lambda i: (i, 0, 0)),
            pl.BlockSpec((gs, ctx_batch, 2 * LANES), lambda i: (i, 0, 0)),
        ],
        out_shape=[jax.ShapeDtypeStruct((g, n, w), BF16),
                   jax.ShapeDtypeStruct((g, ctx_batch, 2 * LANES), F32)],
        scratch_shapes=[pltpu.VMEM((gs, n, w + 4 * LANES), F32), pltpu.VMEM((gs, n, 2 * LANES), F32)],
        compiler_params=pltpu.CompilerParams(dimension_semantics=("parallel",),
                                             vmem_limit_bytes=VMEM_LIMIT),
        name="s5_scan",
    )(ut, h0, p1, pb)


def _tail_kernel(x_ref, oatt_ref, y_ref, u_ref, mod_ref, d_ref, wglu_ref, bglu_ref, wout_ref, g2_ref,
                 wup_ref, cw_ref, cb_ref, wd_ref, o_ref, x1_scr, h2_scr, *, seq_len):
    rows = x_ref.shape[0]
    tf = FFN_SLAB
    mod = mod_ref[0]
    gt1 = mod[2:3]
    sh2 = mod[3:4]
    sc2 = mod[4:5]
    gt2 = mod[5:6]
    for sb in range(rows // MIX_SUB):
        rs = slice(sb * MIX_SUB, (sb + 1) * MIX_SUB)
        y = y_ref[rs, :].astype(F32) + d_ref[...] * u_ref[rs, :]
        z = jax.nn.gelu(y)
        gate = _sigmoid(_nn(z.astype(BF16), wglu_ref[0]) + bglu_ref[...])
        o_ssm = (z * gate).astype(BF16)
        mix = _nn(oatt_ref[rs, :], wout_ref[0, 0:ATT_W, :]) + _nn(o_ssm, wout_ref[0, ATT_W:, :])
        x1 = x_ref[rs, :] + gt1 * mix
        x1_scr[rs, :] = x1
        ms = jnp.mean(x1 * x1, axis=-1, keepdims=True)
        h2 = x1 * lax.rsqrt(ms + EPS) * g2_ref[...]
        h2_scr[rs, :] = (h2 * (1.0 + sc2) + sh2).astype(BF16)

    row8 = lax.broadcasted_iota(jnp.int32, (8, 1), 0)
    h2 = h2_scr[...]

    def conv(up, cw, cb):
        prev = pltpu.roll(up, 1, 0)
        nxt = pltpu.roll(up, rows - 1, 0)
        pp, nn = [], []
        for r in range(0, rows, seq_len):
            e = r + seq_len
            pp += [jnp.where(row8 == 0, 0.0, prev[r:r + 8]), prev[r + 8:e]]
            nn += [nxt[r:e - 8], jnp.where(row8 == 7, 0.0, nxt[e - 8:e])]
        prev = jnp.concatenate(pp, axis=0)
        nxt = jnp.concatenate(nn, axis=0)
        return prev * cw[0:1] + up * cw[1:2] + nxt * cw[2:3] + cb

    acts = []
    for j in range(D_FF // tf):
        cv = slice(j * tf, (j + 1) * tf)
        cg = slice(D_FF + j * tf, D_FF + (j + 1) * tf)
        val = conv(_nn(h2, wup_ref[0, :, cv]), cw_ref[:, cv], cb_ref[:, cv])
        gate = conv(_nn(h2, wup_ref[0, :, cg]), cw_ref[:, cg], cb_ref[:, cg])
        acts.append((gate * _sigmoid(gate) * val).astype(BF16))
    act = jnp.concatenate(acts, axis=1)
    o_ref[...] = x1_scr[...] + gt2 * _nn(act, wd_ref[0])


def _tail(x, oatt, y, u, mods_l, d, wglu, bglu, wout, g2, wup, cw, cb, wdown, layer, seq_len, mod_base):
    n = x.shape[0]
    rows = FFN_ROWS
    per_seq = seq_len // rows if seq_len >= rows else 1
    if mod_base == 0:
        mod_map = lambda i: (0, 0, 0)
    else:
        mod_map = lambda i: (mod_base + i // per_seq, 0, 0)
    row = lambda i: (i, 0)
    const = lambda i: (0, 0)
    lay = lambda i: (layer, 0, 0)
    resident = dict(pipeline_mode=pl.Buffered(1))
    return pl.pallas_call(
        functools.partial(_tail_kernel, seq_len=seq_len),
        grid=(n // rows,),
        in_specs=[
            pl.BlockSpec((rows, D_MODEL), row),
            pl.BlockSpec((rows, ATT_W), row),
            pl.BlockSpec((rows, SSM_W), row),
            pl.BlockSpec((rows, SSM_W), row),
            pl.BlockSpec((1, N_MOD, D_MODEL), mod_map),
            pl.BlockSpec((1, SSM_W), const),
            pl.BlockSpec((1, SSM_W, SSM_W), lay, **resident),
            pl.BlockSpec((1, SSM_W), const),
            pl.BlockSpec((1, D_MODEL, D_MODEL), lay, **resident),
            pl.BlockSpec((1, D_MODEL), const),
            pl.BlockSpec((1, D_MODEL, 2 * D_FF), lay, **resident),
            pl.BlockSpec((3, 2 * D_FF), const),
            pl.BlockSpec((1, 2 * D_FF), const),
            pl.BlockSpec((1, D_FF, D_MODEL), lay, **resident),
        ],
        out_specs=pl.BlockSpec((rows, D_MODEL), row),
        out_shape=jax.ShapeDtypeStruct((n, D_MODEL), F32),
        scratch_shapes=[pltpu.VMEM((rows, D_MODEL), F32), pltpu.VMEM((rows, D_MODEL), BF16)],
        compiler_params=pltpu.CompilerParams(dimension_semantics=("parallel",),
                                             vmem_limit_bytes=VMEM_LIMIT),
        name="layer_tail",
    )(x, oatt, y, u, mods_l, d, wglu, bglu, wout, g2, wup, cw, cb, wdown)


def _rope_tables(n_tokens):
    pos = np.arange(n_tokens)
    row = (pos // GRID_W).astype(np.float64)
    col = (pos % GRID_W).astype(np.float64)
    nf = ROPE_AXIS // 2
    inv_freq = 1.0 / (ROPE_THETA ** (np.arange(nf, dtype=np.float64) / nf))
    lane = np.arange(LANES)
    freq = inv_freq[lane % nf]
    is_col = (lane % D_SUB) >= ROPE_AXIS
    ang = np.where(is_col[None, :], col[:, None], row[:, None]) * freq[None, :]
    second = (lane % ROPE_AXIS) >= nf
    cos = np.cos(ang)
    sin = np.sin(ang)
    sina = np.where(second[None, :], 0.0, -sin)
    sinb = np.where(second[None, :], sin, 0.0)
    return tuple(jnp.asarray(t, dtype=F32) for t in (cos, sina, sinb))


TIME_BLOCKS = CHUNK // GROUPS_PER_BLOCK
CHUNK_PAIR = 2


def _chunk_layout_passes(ctx_batch, ctx_len, lat_batch, lat_len):
    return ((0, ctx_batch, ctx_len, 0), (1, lat_batch, lat_len, ctx_batch * (ctx_len // CHUNK)))


def _row_permutation(batch):
    n = batch * CHUNK
    p = np.zeros((n, n), np.float32)
    b, s = np.meshgrid(np.arange(batch), np.arange(CHUNK), indexing="ij")
    p[(s * batch + b).ravel(), (b * CHUNK + s).ravel()] = 1.0
    return p


def _lane_permutation():
    n = GROUPS_PER_BLOCK * LANES
    q = np.zeros((n, n), np.float32)
    t, g, ch = np.meshgrid(np.arange(GROUPS_PER_BLOCK), np.arange(GROUPS_PER_BLOCK), np.arange(SSM_CH),
                           indexing="ij")
    q[(t * LANES + g * SSM_CH + ch).ravel(), (g * LANES + t * SSM_CH + ch).ravel()] = 1.0
    return q


def _to_chunks_kernel(uc_ref, ul_ref, pc_ref, pl_ref, q_ref, o_ref, v_scr, *, passes):
    for idx, batch, seq_len, row0 in passes:
        u_ref = (uc_ref, ul_ref)[idx]
        p_ref = (pc_ref, pl_ref)[idx]
        for c0 in range(0, seq_len // CHUNK, CHUNK_PAIR):
            x = jnp.concatenate(
                [jnp.concatenate([u_ref[b * seq_len + c * CHUNK:b * seq_len + (c + 1) * CHUNK, :]
                                  for b in range(batch)], axis=0) for c in range(c0, c0 + CHUNK_PAIR)], axis=1)
            r2 = _nn(p_ref[...], x.astype(BF16))
            for k in range(CHUNK_PAIR):
                c = c0 + k
                r = r2[:, k * LANES:(k + 1) * LANES]
                rows = slice(row0 + c * batch, row0 + (c + 1) * batch)
                for s in range(CHUNK):
                    j, t = divmod(s, GROUPS_PER_BLOCK)
                    v_scr[j, rows, t * LANES:(t + 1) * LANES] = r[s * batch:(s + 1) * batch]
    for j in range(TIME_BLOCKS):
        w = _nn(v_scr[j].astype(BF16), q_ref[...])
        for g in range(GROUPS_PER_BLOCK):
            o_ref[g, :, j * LANES:(j + 1) * LANES] = w[:, g * LANES:(g + 1) * LANES].astype(o_ref.dtype)


def _from_chunks_kernel(y_ref, pc_ref, pl_ref, q_ref, oc_ref, ol_ref, v_scr, *, passes):
    for j in range(TIME_BLOCKS):
        yj = jnp.concatenate([y_ref[g, :, j * LANES:(j + 1) * LANES] for g in range(GROUPS_PER_BLOCK)], axis=1)
        v_scr[j] = _nn(yj, q_ref[...])
    for idx, batch, seq_len, row0 in passes:
        o_ref = (oc_ref, ol_ref)[idx]
        p_ref = (pc_ref, pl_ref)[idx]
        for c0 in range(0, seq_len // CHUNK, CHUNK_PAIR):
            xs = []
            for c in range(c0, c0 + CHUNK_PAIR):
                rows = slice(row0 + c * batch, row0 + (c + 1) * batch)
                xs.append(jnp.concatenate(
                    [v_scr[s // GROUPS_PER_BLOCK, rows,
                           (s % GROUPS_PER_BLOCK) * LANES:(s % GROUPS_PER_BLOCK + 1) * LANES]
                     for s in range(CHUNK)], axis=0))
            r2 = _nn(p_ref[...], jnp.concatenate(xs, axis=1).astype(BF16))
            for k in range(CHUNK_PAIR):
                c = c0 + k
                for b in range(batch):
                    o_ref[b * seq_len + c * CHUNK:b * seq_len + (c + 1) * CHUNK, :] = (
                        r2[b * CHUNK:(b + 1) * CHUNK, k * LANES:(k + 1) * LANES].astype(o_ref.dtype))


def _to_chunks(uc, ul, ctx_batch, ctx_len, lat_batch, lat_len):
    passes = _chunk_layout_passes(ctx_batch, ctx_len, lat_batch, lat_len)
    rows = ctx_batch * (ctx_len // CHUNK) + lat_batch * (lat_len // CHUNK)
    pc = jnp.asarray(_row_permutation(ctx_batch), dtype=BF16)
    pn = jnp.asarray(_row_permutation(lat_batch), dtype=BF16)
    q = jnp.asarray(_lane_permutation(), dtype=BF16)
    const = lambda j: (0, 0)
    return pl.pallas_call(
        functools.partial(_to_chunks_kernel, passes=passes),
        grid=(SSM_GROUPS // GROUPS_PER_BLOCK,),
        in_specs=[pl.BlockSpec((uc.shape[0], LANES), lambda j: (0, j)),
                  pl.BlockSpec((ul.shape[0], LANES), lambda j: (0, j)),
                  pl.BlockSpec(pc.shape, const), pl.BlockSpec(pn.shape, const), pl.BlockSpec(q.shape, const)],
        out_specs=pl.BlockSpec((GROUPS_PER_BLOCK, rows, SSM_CH * CHUNK), lambda j: (j, 0, 0)),
        out_shape=jax.ShapeDtypeStruct((SSM_GROUPS, rows, SSM_CH * CHUNK), BF16),
        scratch_shapes=[pltpu.VMEM((TIME_BLOCKS, rows, GROUPS_PER_BLOCK * LANES), F32)],
        compiler_params=pltpu.CompilerParams(dimension_semantics=("parallel",),
                                             vmem_limit_bytes=VMEM_LIMIT),
        name="to_chunks",
    )(uc, ul, pc, pn, q)


def _from_chunks(yt, ctx_batch, ctx_len, lat_batch, lat_len):
    passes = _chunk_layout_passes(ctx_batch, ctx_len, lat_batch, lat_len)
    rows = yt.shape[1]
    n_ctx = ctx_batch * ctx_len
    n_lat = lat_batch * lat_len
    pc = jnp.asarray(_row_permutation(ctx_batch).T, dtype=BF16)
    pn = jnp.asarray(_row_permutation(lat_batch).T, dtype=BF16)
    q = jnp.asarray(_lane_permutation().T, dtype=BF16)
    const = lambda j: (0, 0)
    return pl.pallas_call(
        functools.partial(_from_chunks_kernel, passes=passes),
        grid=(SSM_GROUPS // GROUPS_PER_BLOCK,),
        in_specs=[pl.BlockSpec((GROUPS_PER_BLOCK, rows, SSM_CH * CHUNK), lambda j: (j, 0, 0)),
                  pl.BlockSpec(pc.shape, const), pl.BlockSpec(pn.shape, const), pl.BlockSpec(q.shape, const)],
        out_specs=[pl.BlockSpec((n_ctx, LANES), lambda j: (0, j)),
                   pl.BlockSpec((n_lat, LANES), lambda j: (0, j))],
        out_shape=[jax.ShapeDtypeStruct((n_ctx, SSM_W), BF16), jax.ShapeDtypeStruct((n_lat, SSM_W), BF16)],
        scratch_shapes=[pltpu.VMEM((TIME_BLOCKS, rows, GROUPS_PER_BLOCK * LANES), F32)],
        compiler_params=pltpu.CompilerParams(dimension_semantics=("parallel",),
                                             vmem_limit_bytes=VMEM_LIMIT),
        name="from_chunks",
    )(yt, pc, pn, q)


def _dup_lanes(a):
    return jnp.concatenate([a, a], axis=-1)


def kernel(x_prompt, x_sample, cache_k, cache_v, state_ssm, c, c_ctx, w_mod, b_mod, g_norm1, w_in, q_norm, k_norm, lambda_q1, lambda_k1, lambda_q2, lambda_k2, subln_g, ssm_lambda_re, ssm_lambda_im, ssm_log_step, ssm_b_re, ssm_b_im, ssm_c_re, ssm_c_im, ssm_d, w_glu, b_glu, w_out, g_norm2, w_up, conv_w, conv_b, w_down):
    nb_ctx, len_ctx, _ = x_prompt.shape
    nb_lat, len_lat, _ = x_sample.shape
    past = cache_k.shape[2]

    n_cond = 16
    cond = jnp.zeros((n_cond, D_MODEL), F32).at[0].set(c_ctx).at[1:1 + nb_lat].set(c)
    mods = _modulation(cond, w_mod, b_mod).reshape(DEPTH, n_cond, N_MOD, D_MODEL)

    gidx = np.arange(ATT_W) // D_SUB
    gmat = jnp.asarray(np.where(gidx[:, None] == gidx[None, :], 1.0 / D_SUB, 0.0), dtype=BF16)
    cache_kt = cache_k.transpose(0, 1, 3, 4, 5, 2).reshape(nb_lat, DEPTH, ATT_W, past)
    cache_vr = cache_v.reshape(nb_lat, DEPTH, past * N_HEADS, D_V)
    rope_tabs = _rope_tables(len_lat)
    win_b = w_in.astype(BF16)
    wout_b = w_out.astype(BF16)
    wglu_b = w_glu.astype(BF16)
    wup_b = w_up.astype(BF16)
    wdown_b = w_down.astype(BF16)

    xp = x_prompt.reshape(nb_ctx * len_ctx, D_MODEL)
    xs = x_sample.reshape(nb_lat * len_lat, D_MODEL)
    ss = []
    kv_out = None
    for l in range(DEPTH):
        lam_init = 0.8 - 0.6 * math.exp(-0.3 * l)
        mods_l = mods[l]
        g1 = g_norm1[l].reshape(1, D_MODEL)
        g2 = g_norm2[l].reshape(1, D_MODEL)
        qg = jnp.tile(q_norm[l], ATT_W // D_SUB).reshape(1, ATT_W)
        kg = jnp.tile(k_norm[l], ATT_W // D_SUB).reshape(1, ATT_W)
        sg = jnp.tile(subln_g[l], N_HEADS).reshape(1, ATT_W)
        lamp = jnp.stack([lambda_q1[l], lambda_k1[l], lambda_q2[l], lambda_k2[l]])
        d = ssm_d[l].reshape(1, SSM_W)
        bglu = b_glu[l].reshape(1, SSM_W)
        cb = conv_b[l].reshape(1, 2 * D_FF)
        cw = conv_w[l]

        zeros = jnp.zeros((SSM_GROUPS, SSM_P), F32)
        step = jnp.broadcast_to(ssm_log_step[l][:, :, None], (2, SSM_GROUPS, SSM_P))
        p1 = _dup_lanes(jnp.stack([ssm_lambda_re[l, 0], ssm_lambda_im[l, 0], step[0],
                                   ssm_lambda_re[l, 1], ssm_lambda_im[l, 1], step[1], zeros, zeros], axis=1))
        bt = lambda a: a.transpose(0, 2, 1)
        pb = _dup_lanes(jnp.stack([bt(ssm_b_re[l, 0]), bt(ssm_b_im[l, 0]), ssm_c_re[l, 0], ssm_c_im[l, 0],
                                   bt(ssm_b_re[l, 1]), bt(ssm_b_im[l, 1]), ssm_c_re[l, 1], ssm_c_im[l, 1]], axis=1))
        h0 = state_ssm[:, l].transpose(2, 0, 1, 4, 3).reshape(SSM_GROUPS, nb_lat, 4 * SSM_P)

        qc, kc_new, vc_new, uc, kt, vt = _premix(xp, mods_l, g1, win_b, l, gmat, qg, kg, None, len_ctx, 0,
                                                 carry=kv_out)
        ql, kl, vl, ul = _premix(xs, mods_l, g1, win_b, l, gmat, qg, kg, rope_tabs, len_lat, 1)

        oc = _attention(qc, kc_new, vc_new, lamp, sg, None, l, nb_ctx, len_ctx, lam_init)
        ol = _attention(ql, kl, vl, lamp, sg, (cache_kt, cache_vr), l, nb_lat, len_lat, lam_init)

        ut = _to_chunks(uc, ul, nb_ctx, len_ctx, nb_lat, len_lat)
        yt, hfin = _ssm(ut, h0, p1, pb, nb_ctx, len_ctx // CHUNK, nb_lat, len_lat // CHUNK)
        yc, yl = _from_chunks(yt, nb_ctx, len_ctx, nb_lat, len_lat)

        xp = _tail(xp, oc, yc, uc, mods_l, d, wglu_b, bglu, wout_b, g2, wup_b, cw, cb, wdown_b, l, len_ctx, 0)
        xs = _tail(xs, ol, yl, ul, mods_l, d, wglu_b, bglu, wout_b, g2, wup_b, cw, cb, wdown_b, l, len_lat, 1)

        kv_out = (kt, vt)
        ss.append(hfin.reshape(SSM_GROUPS, nb_ctx, 2, 2, SSM_P).transpose(1, 2, 0, 4, 3))

    new_k = kv_out[0].reshape(nb_ctx, DEPTH, N_HEADS, 2, D_SUB, len_ctx).transpose(0, 1, 5, 2, 3, 4)
    new_v = kv_out[1].reshape(nb_ctx, DEPTH, len_ctx, N_HEADS, D_V)
    return (xp.reshape(nb_ctx, len_ctx, D_MODEL), xs.reshape(nb_lat, len_lat, D_MODEL),
            new_k, new_v, jnp.stack(ss, axis=1))
```

```python
import functools
import math

import numpy as np
import jax
import jax.numpy as jnp
from jax import lax
from jax.experimental import pallas as pl
from jax.experimental.pallas import tpu as pltpu

F32 = jnp.float32
BF16 = jnp.bfloat16

D_MODEL = 1024
DEPTH = 2
GRID_W = 64
ATT_W = 512
SSM_W = 512
N_HEADS = 4
D_V = 128
D_SUB = 64
ROPE_AXIS = 32
ROPE_THETA = 10000.0
SSM_CH = 16
SSM_GROUPS = 32
SSM_P = 64
D_FF = 2048
N_MOD = 6
EPS = 1e-6

LANES = 128
MXU_TILE = 256
GROUPS_PER_BLOCK = LANES // SSM_CH
CHUNK = 32
S5_GROUPS_PER_STEP = 4
TIME_BLOCKS = CHUNK // GROUPS_PER_BLOCK
CHUNK_PAIR = MXU_TILE // LANES
ATTN_TILE = 1024
ATTN_SUB = 256
MIX_TILE = 512
MIX_SUB = 256
FFN_ROWS = 1024
FFN_SLAB = 512
MOD_COLS = 1536
VMEM_LIMIT = 56 * 1024 * 1024


def _nt(a, b):
    return lax.dot_general(a, b, (((1,), (1,)), ((), ())), preferred_element_type=F32)


def _nn(a, b):
    return jnp.dot(a, b, preferred_element_type=F32)


def _sigmoid(x):
    return 1.0 / (1.0 + jnp.exp(-x))


def _mod_kernel(c_ref, w_ref, b_ref, o_ref):
    c = c_ref[...]
    a = (c * _sigmoid(c)).astype(BF16)
    o_ref[0] = _nn(a, w_ref[0].astype(BF16)) + b_ref[0]


def _modulation(cond, w_mod, b_mod):
    nb = cond.shape[0]
    ncol = N_MOD * D_MODEL
    return pl.pallas_call(
        _mod_kernel,
        grid=(DEPTH, ncol // MOD_COLS),
        in_specs=[
            pl.BlockSpec((nb, D_MODEL), lambda l, j: (0, 0)),
            pl.BlockSpec((1, D_MODEL, MOD_COLS), lambda l, j: (l, 0, j)),
            pl.BlockSpec((1, 1, MOD_COLS), lambda l, j: (l, 0, j)),
        ],
        out_specs=pl.BlockSpec((1, nb, MOD_COLS), lambda l, j: (l, 0, j)),
        out_shape=jax.ShapeDtypeStruct((DEPTH, nb, ncol), F32),
        compiler_params=pltpu.CompilerParams(vmem_limit_bytes=VMEM_LIMIT),
        name="adaln_mod",
    )(cond, w_mod, b_mod.reshape(DEPTH, 1, ncol))


def _premix_kernel(x_ref, mod_ref, g1_ref, win_ref, gmat_ref, qg_ref, kg_ref, *rest, rope):
    kt_ref = vt_ref = None
    if rope:
        cos_ref, sina_ref, sinb_ref, q_ref, k_ref, v_ref, u_ref = rest
    else:
        q_ref, k_ref, v_ref, u_ref, kt_ref, vt_ref = rest[-6:]
    mod = mod_ref[0]
    sh1 = mod[0:1]
    sc1 = mod[1:2]

    def head_norm(t, g):
        sq = (t * t).astype(BF16)
        msq = jnp.concatenate([_nn(sq[:, c:c + MXU_TILE], gmat_ref[...])
                               for c in range(0, ATT_W, MXU_TILE)], axis=1)
        return t * lax.rsqrt(msq + EPS) * g

    def rotate(t, rs):
        outs = []
        for j in range(ATT_W // LANES):
            s = t[:, j * LANES:(j + 1) * LANES]
            outs.append(s * cos_ref[rs, :] + pltpu.roll(s, LANES - 16, 1) * sina_ref[rs, :]
                        + pltpu.roll(s, 16, 1) * sinb_ref[rs, :])
        return jnp.concatenate(outs, axis=1)

    for sb in range(x_ref.shape[0] // MIX_SUB):
        rs = slice(sb * MIX_SUB, (sb + 1) * MIX_SUB)
        x = x_ref[rs, :]
        ms = jnp.mean(x * x, axis=-1, keepdims=True)
        h = x * lax.rsqrt(ms + EPS) * g1_ref[...]
        h = h * (1.0 + sc1) + sh1
        qkvu = _nn(h.astype(BF16), win_ref[0])
        q = head_norm(qkvu[:, 0:ATT_W], qg_ref[...])
        k = head_norm(qkvu[:, ATT_W:2 * ATT_W], kg_ref[...])
        if rope:
            q = rotate(q, rs)
            k = rotate(k, rs)
        q_ref[rs, :] = (q * (D_SUB ** -0.5)).astype(q_ref.dtype)
        k_ref[rs, :] = k.astype(k_ref.dtype)
        v = qkvu[:, 2 * ATT_W:3 * ATT_W]
        if kt_ref is not None:
            kt = k.T
            for slot in range(kt_ref.shape[1]):
                kt_ref[sb, slot] = kt
                for hd in range(N_HEADS):
                    vt_ref[sb, slot, pl.ds(hd, MIX_SUB, stride=N_HEADS), :] = v[:, hd * D_V:(hd + 1) * D_V]
        v_ref[rs, :] = v.astype(v_ref.dtype)
        u_ref[rs, :] = qkvu[:, 3 * ATT_W:].astype(u_ref.dtype)


def _premix(x, mods_l, g1, win, layer, gmat, qg, kg, rope_tabs, seq_len, mod_base, carry=None):
    n = x.shape[0]
    tm = MIX_TILE
    per_seq = seq_len // tm
    rope = rope_tabs is not None
    assert rope or seq_len == MIX_SUB
    if mod_base == 0:
        mod_map = lambda i: (0, 0, 0)
    else:
        mod_map = lambda i: (mod_base + i // per_seq, 0, 0)
    row = lambda i: (i, 0)
    const = lambda i: (0, 0)
    in_specs = [
        pl.BlockSpec((tm, D_MODEL), row),
        pl.BlockSpec((1, N_MOD, D_MODEL), mod_map),
        pl.BlockSpec((1, D_MODEL), const),
        pl.BlockSpec((1, D_MODEL, 4 * ATT_W), lambda i: (layer, 0, 0)),
        pl.BlockSpec((MXU_TILE, MXU_TILE), const),
        pl.BlockSpec((1, ATT_W), const),
        pl.BlockSpec((1, ATT_W), const),
    ]
    args = [x, mods_l, g1, win, gmat, qg, kg]
    if rope:
        tab = lambda i: (i % per_seq, 0)
        in_specs += [pl.BlockSpec((tm, LANES), tab)] * 3
        args += list(rope_tabs)
    out = pl.BlockSpec((tm, ATT_W), row)
    out_specs = [out, out, out, out]
    out_shape = [jax.ShapeDtypeStruct((n, ATT_W), BF16),
                 jax.ShapeDtypeStruct((n, ATT_W), BF16),
                 jax.ShapeDtypeStruct((n, ATT_W), BF16),
                 jax.ShapeDtypeStruct((n, SSM_W), F32)]
    aliases = {}
    if not rope:
        nseq = tm // seq_len
        slots, first = (DEPTH, 0) if carry is None else (1, layer)
        out_specs.append(pl.BlockSpec((nseq, slots, ATT_W, seq_len), lambda i: (i, first, 0, 0)))
        out_shape.append(jax.ShapeDtypeStruct((n // seq_len, DEPTH, ATT_W, seq_len), F32))
        out_specs.append(pl.BlockSpec((nseq, slots, seq_len * N_HEADS, D_V), lambda i: (i, first, 0, 0)))
        out_shape.append(jax.ShapeDtypeStruct((n // seq_len, DEPTH, seq_len * N_HEADS, D_V), F32))
        if carry is not None:
            aliases = {len(args): 4, len(args) + 1: 5}
            in_specs += [pl.BlockSpec(memory_space=pl.ANY)] * 2
            args += list(carry)
    return pl.pallas_call(
        functools.partial(_premix_kernel, rope=rope),
        grid=(n // tm,),
        in_specs=in_specs,
        out_specs=out_specs,
        out_shape=out_shape,
        input_output_aliases=aliases,
        compiler_params=pltpu.CompilerParams(dimension_semantics=("parallel",),
                                             vmem_limit_bytes=VMEM_LIMIT),
        name="premix_rope" if rope else "premix",
    )(*args)


def _attn_kernel(q_ref, k_ref, v_ref, lamp_ref, sg_ref, *rest, lam_init, has_cache):
    if has_cache:
        kc_ref, vc_ref, o_ref = rest
    else:
        (o_ref,) = rest
    tq = q_ref.shape[0]
    lp = lamp_ref[...]
    l1 = jnp.sum(lp[0:1] * lp[1:2], axis=-1, keepdims=True)
    l2 = jnp.sum(lp[2:3] * lp[3:4], axis=-1, keepdims=True)
    lam = jnp.exp(l1) - jnp.exp(l2) + lam_init
    first = lax.broadcasted_iota(jnp.int32, (1, LANES), 1) < D_SUB
    tsub = min(ATTN_SUB, tq)
    for h in range(N_HEADS):
        sl = slice(h * D_V, (h + 1) * D_V)
        kn = k_ref[:, sl]
        vn = v_ref[:, sl]
        if has_cache:
            ncache = kc_ref.shape[3]
            kct = kc_ref[0, 0, sl, :].astype(BF16)
            vct = vc_ref[0, 0, pl.ds(h, ncache, stride=N_HEADS), :].astype(BF16)
        for r0 in range(0, tq, tsub):
            qh = q_ref[r0:r0 + tsub, sl]
            zero = jnp.zeros_like(qh)
            qq = jnp.concatenate([jnp.where(first, qh, zero), jnp.where(first, zero, qh)], axis=0)
            if has_cache:
                sc = jnp.concatenate([_nn(qq, kct), _nt(qq, kn)], axis=1)
            else:
                sc = _nt(qq, kn)
            m = jnp.max(sc, axis=-1, keepdims=True)
            e = jnp.exp(sc - m)
            ssum = jnp.sum(e, axis=-1, keepdims=True)
            eb = e.astype(BF16)
            if has_cache:
                pv = _nn(eb[:, :ncache], vct) + _nn(eb[:, ncache:], vn)
            else:
                pv = _nn(eb, vn)
            on = pv * (1.0 / ssum)
            o = on[:tsub] - lam * on[tsub:]
            ms = jnp.mean(o * o, axis=-1, keepdims=True)
            o_ref[r0:r0 + tsub, sl] = (o * lax.rsqrt(ms + EPS) * sg_ref[:, sl]
                                       * (1.0 - lam_init)).astype(o_ref.dtype)


def _attention(q, k, v, lamp, sg, cache, layer, batch, seq_len, lam_init):
    n = q.shape[0]
    tq = min(ATTN_TILE, seq_len)
    nq = seq_len // tq
    has_cache = cache is not None
    in_specs = [
        pl.BlockSpec((tq, ATT_W), lambda b, i: (b * nq + i, 0)),
        pl.BlockSpec((seq_len, ATT_W), lambda b, i: (b, 0)),
        pl.BlockSpec((seq_len, ATT_W), lambda b, i: (b, 0)),
        pl.BlockSpec((4, D_SUB), lambda b, i: (0, 0)),
        pl.BlockSpec((1, ATT_W), lambda b, i: (0, 0)),
    ]
    args = [q, k, v, lamp, sg]
    if has_cache:
        past = cache[0].shape[3]
        in_specs += [pl.BlockSpec((1, 1, ATT_W, past), lambda b, i: (b, layer, 0, 0)),
                     pl.BlockSpec((1, 1, past * N_HEADS, D_V), lambda b, i: (b, layer, 0, 0))]
        args += list(cache)
    return pl.pallas_call(
        functools.partial(_attn_kernel, lam_init=lam_init, has_cache=has_cache),
        grid=(batch, nq),
        in_specs=in_specs,
        out_specs=pl.BlockSpec((tq, ATT_W), lambda b, i: (b * nq + i, 0)),
        out_shape=jax.ShapeDtypeStruct((n, ATT_W), BF16),
        compiler_params=pltpu.CompilerParams(dimension_semantics=("parallel", "parallel"),
                                             vmem_limit_bytes=VMEM_LIMIT),
        name="attn_cache" if has_cache else "attn",
    )(*args)


def _ssm_kernel(ut_ref, h0_ref, p1_ref, pb_ref, yt_ref, hf_ref, r_scr, hin_scr, **sizes):
    for gi in range(ut_ref.shape[0]):
        one = lambda ref, gi=gi: ref.at[pl.ds(gi, 1)]
        _ssm_group(one(ut_ref), one(h0_ref), one(p1_ref), one(pb_ref), one(yt_ref), one(hf_ref),
                   r_scr.at[gi], hin_scr.at[gi], **sizes)


def _ssm_group(ut_ref, h0_ref, p1_ref, pb_ref, yt_ref, hf_ref, r_scr, hin_scr, *,
               ctx_batch, ctx_chunks, lat_batch, lat_chunks):
    T = CHUNK
    W = SSM_CH * T
    lane = lax.broadcasted_iota(jnp.int32, (1, LANES), 1)
    lo = lane < SSM_P
    sgn_lr = jnp.where(lo, 1.0, -1.0).astype(F32)
    sgn_rl = -sgn_lr
    srow = lax.broadcasted_iota(jnp.int32, (T, LANES), 0).astype(F32)

    def swap(a):
        return pltpu.roll(a, SSM_P, 1)

    def tile_time(a):
        return jnp.concatenate([jnp.broadcast_to(a[i:i + 1, :], (SSM_CH, LANES)) for i in range(T)], axis=0)

    def rep_chan(a):
        return jnp.concatenate([a] * T, axis=0)

    def pair_pack(y):
        ys = swap(y)
        return jnp.where(lo, y, ys), jnp.where(lo, ys, y) * sgn_rl

    def cmul_const(x, yr2, yi2s):
        return x * yr2 + swap(x) * yi2s

    p1 = p1_ref[0]
    pb = pb_ref[0]

    def direction(d):
        lre2 = p1[3 * d + 0:3 * d + 1]
        lim2 = p1[3 * d + 1:3 * d + 2]
        ls2 = p1[3 * d + 2:3 * d + 3]
        bre2, bim2, cre2, cim2 = pb[4 * d + 0], pb[4 * d + 1], pb[4 * d + 2], pb[4 * d + 3]
        delta = jnp.exp(ls2)
        xr = lre2 * delta
        th = lim2 * delta

        def cpow_const(k):
            return jnp.exp(k * xr) * jnp.where(lo, jnp.cos(k * th), jnp.sin(k * th))

        mag1 = jnp.exp(xr)
        lbr2 = mag1 * jnp.cos(th)
        lbi2 = mag1 * jnp.sin(th)
        den = lre2 * lre2 + lim2 * lim2
        nr = lbr2 - 1.0
        cr2 = (nr * lre2 + lbi2 * lim2) / den
        ci2 = (lbi2 * lre2 - nr * lim2) / den
        bbr16 = cr2 * bre2 - ci2 * bim2
        bbi16s = (cr2 * bim2 + ci2 * bre2) * sgn_rl
        bbr2 = rep_chan(bbr16)
        bbi2s = rep_chan(bbi16s)
        ccr2 = rep_chan(cre2)
        cci2s = rep_chan(cim2 * sgn_rl)

        def cpow_rows(k):
            ang = k * th
            return jnp.exp(k * xr) * jnp.where(lo, jnp.cos(ang), jnp.sin(ang))

        wp = cpow_rows(srow)
        wrev = cpow_rows(float(T - 1) - srow)

        def times(x, c1, c2s):
            return tile_time(x) * c1 + tile_time(swap(x)) * c2s

        lam_1 = pair_pack(cpow_const(1.0))
        lam_t = pair_pack(cpow_const(float(T)))
        if d == 0:
            clag = times(wp, ccr2, cci2s)
            bs = times(wrev, bbr2, bbi2s)
            cc = times(cmul_const(wp, *lam_1), ccr2, cci2s)
        else:
            clag = times(wrev, ccr2, cci2s)
            bs = times(wp, bbr2, bbi2s)
            cc = times(cmul_const(wrev, *lam_1), ccr2, cci2s)
        kt = _nt(jnp.where(lo, bbr16, -bbi16s).astype(BF16), clag.astype(BF16))
        bbar = jnp.where(lo, bbr2, bbi2s)
        return kt, bs, cc * sgn_lr, lam_t, bbar

    kt_f, bsf, ccf, lamt_f, bbar_f = direction(0)
    kt_b, bsb, ccb, lamt_b, bbar_b = direction(1)

    sub = LANES // SSM_CH
    right_f = [kt_f if b == 0 else pltpu.roll(kt_f, b * SSM_CH, 1) for b in range(sub)]
    left_b = [kt_b if b == 0 else pltpu.roll(kt_b, W - b * SSM_CH, 1) for b in range(sub)]

    def ring(x, lanes):
        return x if lanes % W == 0 else jnp.concatenate([x[:, W - lanes % W:], x[:, :W - lanes % W]], axis=1)

    blocks = []
    for s in range(T):
        a, b = divmod(s, sub)
        a2, b2 = divmod(T - 1 - s, sub)
        fwd = ring(right_f[b], a * LANES)
        bwd = ring(left_b[b2], W - a2 * LANES)
        first, end = s * SSM_CH, (s + 1) * SSM_CH
        cols = []
        for v in range(W // LANES):
            lo_l, hi_l = v * LANES, (v + 1) * LANES
            fv, bv = fwd[:, lo_l:hi_l], bwd[:, lo_l:hi_l]
            parts = []
            if lo_l >= first:
                parts.append(fv)
            elif hi_l > first:
                parts.append(jnp.where(lane >= first - lo_l, fv, 0.0))
            if hi_l <= end:
                parts.append(bv)
            elif lo_l < end:
                parts.append(jnp.where(lane < end - lo_l, bv, 0.0))
            cols.append(parts[0] if len(parts) == 1 else parts[0] + parts[1])
        blocks.append(jnp.concatenate(cols, axis=1))
    a_t = jnp.concatenate(blocks, axis=0)
    w1 = jnp.concatenate([a_t, bsf, swap(bsf), bsb, swap(bsb)], axis=1).astype(BF16)
    r_scr[...] = _nn(ut_ref[0], w1)

    def chain(row0, nb, nchunks, col, lam_t, h, reverse):
        a1, a2s = lam_t
        hs = swap(h)
        order = range(nchunks - 1, -1, -1) if reverse else range(nchunks)
        hcol = slice(LANES, 2 * LANES) if reverse else slice(0, LANES)
        for c in order:
            rows = slice(row0 + c * nb, row0 + (c + 1) * nb)
            hin_scr[rows, hcol] = h
            s = r_scr[rows, col:col + LANES]
            ss = r_scr[rows, col + LANES:col + 2 * LANES]
            h, hs = h * a1 + hs * a2s + s, hs * a1 - h * a2s + ss
        return h

    zero = jnp.zeros((ctx_batch, LANES), F32)
    chain(0, ctx_batch, ctx_chunks, W, lamt_f, zero, False)
    chain(0, ctx_batch, ctx_chunks, W + 2 * LANES, lamt_b, zero, True)
    s_tiled = tile_time(srow)
    first_tok = jnp.where(s_tiled == 0.0, bbar_f, 0.0).astype(BF16)
    last_tok = jnp.where(s_tiled == float(T - 1), bbar_b, 0.0).astype(BF16)
    last_rows = (ctx_chunks - 1) * ctx_batch
    hf_ref[0, :, 0:LANES] = _nn(ut_ref[0, 0:ctx_batch, :], first_tok)
    hf_ref[0, :, LANES:2 * LANES] = _nn(ut_ref[0, last_rows:last_rows + ctx_batch, :], last_tok)
    lat0 = ctx_batch * ctx_chunks
    h0 = h0_ref[0]
    chain(lat0, lat_batch, lat_chunks, W, lamt_f, h0[:, 0:LANES], False)
    chain(lat0, lat_batch, lat_chunks, W + 2 * LANES, lamt_b, h0[:, LANES:2 * LANES], True)

    cc_cat = jnp.concatenate([ccf, ccb], axis=1).astype(BF16)
    yt_ref[0] = (r_scr[:, 0:W] + _nt(hin_scr[...].astype(BF16), cc_cat)).astype(yt_ref.dtype)


def _ssm(ut, h0, p1, pb, ctx_batch, ctx_chunks, lat_batch, lat_chunks):
    g, n, w = ut.shape
    gs = S5_GROUPS_PER_STEP
    return pl.pallas_call(
        functools.partial(_ssm_kernel, ctx_batch=ctx_batch, ctx_chunks=ctx_chunks,
                          lat_batch=lat_batch, lat_chunks=lat_chunks),
        grid=(g // gs,),
        in_specs=[
            pl.BlockSpec((gs, n, w), lambda i: (i, 0, 0)),
            pl.BlockSpec((gs, lat_batch, 2 * LANES), lambda i: (i, 0, 0)),
            pl.BlockSpec((gs, 8, LANES), lambda i: (i, 0, 0)),
            pl.BlockSpec((gs, 8, SSM_CH, LANES), lambda i: (i, 0, 0, 0)),
        ],
        out_specs=[
            pl.BlockSpec((gs, n, w), lambda i: (i, 0, 0)),
            pl.BlockSpec((gs, ctx_batch, 2 * LANES), lambda i: (i, 0, 0)),
        ],
        out_shape=[jax.ShapeDtypeStruct((g, n, w), BF16),
                   jax.ShapeDtypeStruct((g, ctx_batch, 2 * LANES), F32)],
        scratch_shapes=[pltpu.VMEM((gs, n, w + 4 * LANES), F32), pltpu.VMEM((gs, n, 2 * LANES), F32)],
        compiler_params=pltpu.CompilerParams(dimension_semantics=("parallel",),
                                             vmem_limit_bytes=VMEM_LIMIT),
        name="s5_scan",
    )(ut, h0, p1, pb)


def _tail_kernel(x_ref, oatt_ref, y_ref, u_ref, mod_ref, d_ref, wglu_ref, bglu_ref, wout_ref, g2_ref,
                 wup_ref, cw_ref, cb_ref, wd_ref, o_ref, x1_scr, h2_scr, *, seq_len):
    rows = x_ref.shape[0]
    tf = FFN_SLAB
    mod = mod_ref[0]
    gt1 = mod[2:3]
    sh2 = mod[3:4]
    sc2 = mod[4:5]
    gt2 = mod[5:6]
    for sb in range(rows // MIX_SUB):
        rs = slice(sb * MIX_SUB, (sb + 1) * MIX_SUB)
        y = y_ref[rs, :].astype(F32) + d_ref[...] * u_ref[rs, :]
        z = jax.nn.gelu(y)
        gate = _sigmoid(_nn(z.astype(BF16), wglu_ref[0]) + bglu_ref[...])
        o_ssm = (z * gate).astype(BF16)
        mix = _nn(oatt_ref[rs, :], wout_ref[0, 0:ATT_W, :]) + _nn(o_ssm, wout_ref[0, ATT_W:, :])
        x1 = x_ref[rs, :] + gt1 * mix
        x1_scr[rs, :] = x1
        ms = jnp.mean(x1 * x1, axis=-1, keepdims=True)
        h2 = x1 * lax.rsqrt(ms + EPS) * g2_ref[...]
        h2_scr[rs, :] = (h2 * (1.0 + sc2) + sh2).astype(BF16)

    row8 = lax.broadcasted_iota(jnp.int32, (8, 1), 0)
    h2 = h2_scr[...]

    def conv(up, cw, cb):
        prev = pltpu.roll(up, 1, 0)
        nxt = pltpu.roll(up, rows - 1, 0)
        pp, nn = [], []
        for r in range(0, rows, seq_len):
            e = r + seq_len
            pp += [jnp.where(row8 == 0, 0.0, prev[r:r + 8]), prev[r + 8:e]]
            nn += [nxt[r:e - 8], jnp.where(row8 == 7, 0.0, nxt[e - 8:e])]
        prev = jnp.concatenate(pp, axis=0)
        nxt = jnp.concatenate(nn, axis=0)
        return prev * cw[0:1] + up * cw[1:2] + nxt * cw[2:3] + cb

    acts = []
    for j in range(D_FF // tf):
        cv = slice(j * tf, (j + 1) * tf)
        cg = slice(D_FF + j * tf, D_FF + (j + 1) * tf)
        val = conv(_nn(h2, wup_ref[0, :, cv]), cw_ref[:, cv], cb_ref[:, cv])
        gate = conv(_nn(h2, wup_ref[0, :, cg]), cw_ref[:, cg], cb_ref[:, cg])
        acts.append((gate * _sigmoid(gate) * val).astype(BF16))
    act = jnp.concatenate(acts, axis=1)
    o_ref[...] = x1_scr[...] + gt2 * _nn(act, wd_ref[0])


def _tail(x, oatt, y, u, mods_l, d, wglu, bglu, wout, g2, wup, cw, cb, wdown, layer, seq_len, mod_base):
    n = x.shape[0]
    rows = FFN_ROWS
    per_seq = seq_len // rows if seq_len >= rows else 1
    if mod_base == 0:
        mod_map = lambda i: (0, 0, 0)
    else:
        mod_map = lambda i: (mod_base + i // per_seq, 0, 0)
    row = lambda i: (i, 0)
    const = lambda i: (0, 0)
    lay = lambda i: (layer, 0, 0)
    resident = dict(pipeline_mode=pl.Buffered(1))
    return pl.pallas_call(
        functools.partial(_tail_kernel, seq_len=seq_len),
        grid=(n // rows,),
        in_specs=[
            pl.BlockSpec((rows, D_MODEL), row),
            pl.BlockSpec((rows, ATT_W), row),
            pl.BlockSpec((rows, SSM_W), row),
            pl.BlockSpec((rows, SSM_W), row),
            pl.BlockSpec((1, N_MOD, D_MODEL), mod_map),
            pl.BlockSpec((1, SSM_W), const),
            pl.BlockSpec((1, SSM_W, SSM_W), lay, **resident),
            pl.BlockSpec((1, SSM_W), const),
            pl.BlockSpec((1, D_MODEL, D_MODEL), lay, **resident),
            pl.BlockSpec((1, D_MODEL), const),
            pl.BlockSpec((1, D_MODEL, 2 * D_FF), lay, **resident),
            pl.BlockSpec((3, 2 * D_FF), const),
            pl.BlockSpec((1, 2 * D_FF), const),
            pl.BlockSpec((1, D_FF, D_MODEL), lay, **resident),
        ],
        out_specs=pl.BlockSpec((rows, D_MODEL), row),
        out_shape=jax.ShapeDtypeStruct((n, D_MODEL), F32),
        scratch_shapes=[pltpu.VMEM((rows, D_MODEL), F32), pltpu.VMEM((rows, D_MODEL), BF16)],
        compiler_params=pltpu.CompilerParams(dimension_semantics=("parallel",),
                                             vmem_limit_bytes=VMEM_LIMIT),
        name="layer_tail",
    )(x, oatt, y, u, mods_l, d, wglu, bglu, wout, g2, wup, cw, cb, wdown)


def _rope_tables(n_tokens):
    pos = np.arange(n_tokens)
    row = (pos // GRID_W).astype(np.float64)
    col = (pos % GRID_W).astype(np.float64)
    nf = ROPE_AXIS // 2
    inv_freq = 1.0 / (ROPE_THETA ** (np.arange(nf, dtype=np.float64) / nf))
    lane = np.arange(LANES)
    freq = inv_freq[lane % nf]
    is_col = (lane % D_SUB) >= ROPE_AXIS
    ang = np.where(is_col[None, :], col[:, None], row[:, None]) * freq[None, :]
    second = (lane % ROPE_AXIS) >= nf
    cos = np.cos(ang)
    sin = np.sin(ang)
    sina = np.where(second[None, :], 0.0, -sin)
    sinb = np.where(second[None, :], sin, 0.0)
    return tuple(jnp.asarray(t, dtype=F32) for t in (cos, sina, sinb))


def _chunk_layout_passes(ctx_batch, ctx_len, lat_batch, lat_len):
    return ((0, ctx_batch, ctx_len, 0), (1, lat_batch, lat_len, ctx_batch * (ctx_len // CHUNK)))


def _row_permutation(batch):
    n = batch * CHUNK
    p = np.zeros((n, n), np.float32)
    b, s = np.meshgrid(np.arange(batch), np.arange(CHUNK), indexing="ij")
    p[(s * batch + b).ravel(), (b * CHUNK + s).ravel()] = 1.0
    return p


def _lane_permutation():
    n = GROUPS_PER_BLOCK * LANES
    q = np.zeros((n, n), np.float32)
    t, g, ch = np.meshgrid(np.arange(GROUPS_PER_BLOCK), np.arange(GROUPS_PER_BLOCK), np.arange(SSM_CH),
                           indexing="ij")
    q[(t * LANES + g * SSM_CH + ch).ravel(), (g * LANES + t * SSM_CH + ch).ravel()] = 1.0
    return q


def _to_chunks_kernel(uc_ref, ul_ref, pc_ref, pl_ref, q_ref, o_ref, v_scr, *, passes):
    for idx, batch, seq_len, row0 in passes:
        u_ref = (uc_ref, ul_ref)[idx]
        p_ref = (pc_ref, pl_ref)[idx]
        for c0 in range(0, seq_len // CHUNK, CHUNK_PAIR):
            x = jnp.concatenate(
                [jnp.concatenate([u_ref[b * seq_len + c * CHUNK:b * seq_len + (c + 1) * CHUNK, :]
                                  for b in range(batch)], axis=0) for c in range(c0, c0 + CHUNK_PAIR)], axis=1)
            r2 = _nn(p_ref[...], x.astype(BF16))
            for k in range(CHUNK_PAIR):
                c = c0 + k
                r = r2[:, k * LANES:(k + 1) * LANES]
                rows = slice(row0 + c * batch, row0 + (c + 1) * batch)
                for s in range(CHUNK):
                    j, t = divmod(s, GROUPS_PER_BLOCK)
                    v_scr[j, rows, t * LANES:(t + 1) * LANES] = r[s * batch:(s + 1) * batch]
    for j in range(TIME_BLOCKS):
        w = _nn(v_scr[j].astype(BF16), q_ref[...])
        for g in range(GROUPS_PER_BLOCK):
            o_ref[g, :, j * LANES:(j + 1) * LANES] = w[:, g * LANES:(g + 1) * LANES].astype(o_ref.dtype)


def _from_chunks_kernel(y_ref, pc_ref, pl_ref, q_ref, oc_ref, ol_ref, v_scr, *, passes):
    for j in range(TIME_BLOCKS):
        yj = jnp.concatenate([y_ref[g, :, j * LANES:(j + 1) * LANES] for g in range(GROUPS_PER_BLOCK)], axis=1)
        v_scr[j] = _nn(yj, q_ref[...])
    for idx, batch, seq_len, row0 in passes:
        o_ref = (oc_ref, ol_ref)[idx]
        p_ref = (pc_ref, pl_ref)[idx]
        for c0 in range(0, seq_len // CHUNK, CHUNK_PAIR):
            xs = []
            for c in range(c0, c0 + CHUNK_PAIR):
                rows = slice(row0 + c * batch, row0 + (c + 1) * batch)
                xs.append(jnp.concatenate(
                    [v_scr[s // GROUPS_PER_BLOCK, rows,
                           (s % GROUPS_PER_BLOCK) * LANES:(s % GROUPS_PER_BLOCK + 1) * LANES]
                     for s in range(CHUNK)], axis=0))
            r2 = _nn(p_ref[...], jnp.concatenate(xs, axis=1).astype(BF16))
            for k in range(CHUNK_PAIR):
                c = c0 + k
                for b in range(batch):
                    o_ref[b * seq_len + c * CHUNK:b * seq_len + (c + 1) * CHUNK, :] = (
                        r2[b * CHUNK:(b + 1) * CHUNK, k * LANES:(k + 1) * LANES].astype(o_ref.dtype))


def _to_chunks(uc, ul, ctx_batch, ctx_len, lat_batch, lat_len):
    passes = _chunk_layout_passes(ctx_batch, ctx_len, lat_batch, lat_len)
    rows = ctx_batch * (ctx_len // CHUNK) + lat_batch * (lat_len // CHUNK)
    pc = jnp.asarray(_row_permutation(ctx_batch), dtype=BF16)
    pn = jnp.asarray(_row_permutation(lat_batch), dtype=BF16)
    q = jnp.asarray(_lane_permutation(), dtype=BF16)
    const = lambda j: (0, 0)
    return pl.pallas_call(
        functools.partial(_to_chunks_kernel, passes=passes),
        grid=(SSM_GROUPS // GROUPS_PER_BLOCK,),
        in_specs=[pl.BlockSpec((uc.shape[0], LANES), lambda j: (0, j)),
                  pl.BlockSpec((ul.shape[0], LANES), lambda j: (0, j)),
                  pl.BlockSpec(pc.shape, const), pl.BlockSpec(pn.shape, const), pl.BlockSpec(q.shape, const)],
        out_specs=pl.BlockSpec((GROUPS_PER_BLOCK, rows, SSM_CH * CHUNK), lambda j: (j, 0, 0)),
        out_shape=jax.ShapeDtypeStruct((SSM_GROUPS, rows, SSM_CH * CHUNK), BF16),
        scratch_shapes=[pltpu.VMEM((TIME_BLOCKS, rows, GROUPS_PER_BLOCK * LANES), F32)],
        compiler_params=pltpu.CompilerParams(dimension_semantics=("parallel",),
                                             vmem_limit_bytes=VMEM_LIMIT),
        name="to_chunks",
    )(uc, ul, pc, pn, q)


def _from_chunks(yt, ctx_batch, ctx_len, lat_batch, lat_len):
    passes = _chunk_layout_passes(ctx_batch, ctx_len, lat_batch, lat_len)
    rows = yt.shape[1]
    n_ctx = ctx_batch * ctx_len
    n_lat = lat_batch * lat_len
    pc = jnp.asarray(_row_permutation(ctx_batch).T, dtype=BF16)
    pn = jnp.asarray(_row_permutation(lat_batch).T, dtype=BF16)
    q = jnp.asarray(_lane_permutation().T, dtype=BF16)
    const = lambda j: (0, 0)
    return pl.pallas_call(
        functools.partial(_from_chunks_kernel, passes=passes),
        grid=(SSM_GROUPS // GROUPS_PER_BLOCK,),
        in_specs=[pl.BlockSpec((GROUPS_PER_BLOCK, rows, SSM_CH * CHUNK), lambda j: (j, 0, 0)),
                  pl.BlockSpec(pc.shape, const), pl.BlockSpec(pn.shape, const), pl.BlockSpec(q.shape, const)],
        out_specs=[pl.BlockSpec((n_ctx, LANES), lambda j: (0, j)),
                   pl.BlockSpec((n_lat, LANES), lambda j: (0, j))],
        out_shape=[jax.ShapeDtypeStruct((n_ctx, SSM_W), BF16), jax.ShapeDtypeStruct((n_lat, SSM_W), BF16)],
        scratch_shapes=[pltpu.VMEM((TIME_BLOCKS, rows, GROUPS_PER_BLOCK * LANES), F32)],
        compiler_params=pltpu.CompilerParams(dimension_semantics=("parallel",),
                                             vmem_limit_bytes=VMEM_LIMIT),
        name="from_chunks",
    )(yt, pc, pn, q)


def _dup_lanes(a):
    return jnp.concatenate([a, a], axis=-1)


def kernel(x_prompt, x_sample, cache_k, cache_v, state_ssm, c, c_ctx, w_mod, b_mod, g_norm1, w_in, q_norm, k_norm, lambda_q1, lambda_k1, lambda_q2, lambda_k2, subln_g, ssm_lambda_re, ssm_lambda_im, ssm_log_step, ssm_b_re, ssm_b_im, ssm_c_re, ssm_c_im, ssm_d, w_glu, b_glu, w_out, g_norm2, w_up, conv_w, conv_b, w_down):
    nb_ctx, len_ctx, _ = x_prompt.shape
    nb_lat, len_lat, _ = x_sample.shape
    past = cache_k.shape[2]

    n_cond = 16
    cond = jnp.zeros((n_cond, D_MODEL), F32).at[0].set(c_ctx).at[1:1 + nb_lat].set(c)
    mods = _modulation(cond, w_mod, b_mod).reshape(DEPTH, n_cond, N_MOD, D_MODEL)

    gidx = np.arange(MXU_TILE) // D_SUB
    gmat = jnp.asarray(np.where(gidx[:, None] == gidx[None, :], 1.0 / D_SUB, 0.0), dtype=BF16)
    cache_kt = cache_k.transpose(0, 1, 3, 4, 5, 2).reshape(nb_lat, DEPTH, ATT_W, past)
    cache_vr = cache_v.reshape(nb_lat, DEPTH, past * N_HEADS, D_V)
    rope_tabs = _rope_tables(len_lat)
    win_b = w_in.astype(BF16)
    wout_b = w_out.astype(BF16)
    wglu_b = w_glu.astype(BF16)
    wup_b = w_up.astype(BF16)
    wdown_b = w_down.astype(BF16)

    xp = x_prompt.reshape(nb_ctx * len_ctx, D_MODEL)
    xs = x_sample.reshape(nb_lat * len_lat, D_MODEL)
    ss = []
    kv_out = None
    for l in range(DEPTH):
        lam_init = 0.8 - 0.6 * math.exp(-0.3 * l)
        mods_l = mods[l]
        g1 = g_norm1[l].reshape(1, D_MODEL)
        g2 = g_norm2[l].reshape(1, D_MODEL)
        qg = jnp.tile(q_norm[l], ATT_W // D_SUB).reshape(1, ATT_W)
        kg = jnp.tile(k_norm[l], ATT_W // D_SUB).reshape(1, ATT_W)
        sg = jnp.tile(subln_g[l], N_HEADS).reshape(1, ATT_W)
        lamp = jnp.stack([lambda_q1[l], lambda_k1[l], lambda_q2[l], lambda_k2[l]])
        d = ssm_d[l].reshape(1, SSM_W)
        bglu = b_glu[l].reshape(1, SSM_W)
        cb = conv_b[l].reshape(1, 2 * D_FF)
        cw = conv_w[l]

        zeros = jnp.zeros((SSM_GROUPS, SSM_P), F32)
        step = jnp.broadcast_to(ssm_log_step[l][:, :, None], (2, SSM_GROUPS, SSM_P))
        p1 = _dup_lanes(jnp.stack([ssm_lambda_re[l, 0], ssm_lambda_im[l, 0], step[0],
                                   ssm_lambda_re[l, 1], ssm_lambda_im[l, 1], step[1], zeros, zeros], axis=1))
        bt = lambda a: a.transpose(0, 2, 1)
        pb = _dup_lanes(jnp.stack([bt(ssm_b_re[l, 0]), bt(ssm_b_im[l, 0]), ssm_c_re[l, 0], ssm_c_im[l, 0],
                                   bt(ssm_b_re[l, 1]), bt(ssm_b_im[l, 1]), ssm_c_re[l, 1], ssm_c_im[l, 1]], axis=1))
        h0 = state_ssm[:, l].transpose(2, 0, 1, 4, 3).reshape(SSM_GROUPS, nb_lat, 4 * SSM_P)

        qc, kc_new, vc_new, uc, kt, vt = _premix(xp, mods_l, g1, win_b, l, gmat, qg, kg, None, len_ctx, 0,
                                                 carry=kv_out)
        ql, kl, vl, ul = _premix(xs, mods_l, g1, win_b, l, gmat, qg, kg, rope_tabs, len_lat, 1)

        oc = _attention(qc, kc_new, vc_new, lamp, sg, None, l, nb_ctx, len_ctx, lam_init)
        ol = _attention(ql, kl, vl, lamp, sg, (cache_kt, cache_vr), l, nb_lat, len_lat, lam_init)

        ut = _to_chunks(uc, ul, nb_ctx, len_ctx, nb_lat, len_lat)
        yt, hfin = _ssm(ut, h0, p1, pb, nb_ctx, len_ctx // CHUNK, nb_lat, len_lat // CHUNK)
        yc, yl = _from_chunks(yt, nb_ctx, len_ctx, nb_lat, len_lat)

        xp = _tail(xp, oc, yc, uc, mods_l, d, wglu_b, bglu, wout_b, g2, wup_b, cw, cb, wdown_b, l, len_ctx, 0)
        xs = _tail(xs, ol, yl, ul, mods_l, d, wglu_b, bglu, wout_b, g2, wup_b, cw, cb, wdown_b, l, len_lat, 1)

        kv_out = (kt, vt)
        ss.append(hfin.reshape(SSM_GROUPS, nb_ctx, 2, 2, SSM_P).transpose(1, 2, 0, 4, 3))

    new_k = kv_out[0].reshape(nb_ctx, DEPTH, N_HEADS, 2, D_SUB, len_ctx).transpose(0, 1, 5, 2, 3, 4)
    new_v = kv_out[1].reshape(nb_ctx, DEPTH, len_ctx, N_HEADS, D_V)
    return (xp.reshape(nb_ctx, len_ctx, D_MODEL), xs.reshape(nb_lat, len_lat, D_MODEL),
            new_k, new_v, jnp.stack(ss, axis=1))
```

```python
import functools
import math

import numpy as np
import jax
import jax.numpy as jnp
from jax import lax
from jax.experimental import pallas as pl
from jax.experimental.pallas import tpu as pltpu

F32 = jnp.float32
BF16 = jnp.bfloat16

D_MODEL = 1024
DEPTH = 2
GRID_W = 64
ATT_W = 512
SSM_W = 512
N_HEADS = 4
D_V = 128
D_SUB = 64
ROPE_AXIS = 32
ROPE_THETA = 10000.0
SSM_CH = 16
SSM_GROUPS = 32
SSM_P = 64
D_FF = 2048
N_MOD = 6
EPS = 1e-6

LANES = 128
MXU_TILE = 256
GROUPS_PER_BLOCK = LANES // SSM_CH
CHUNK = 32
S5_GROUPS_PER_STEP = 4
TIME_BLOCKS = CHUNK // GROUPS_PER_BLOCK
CHUNK_PAIR = MXU_TILE // LANES
ATTN_TILE = 1024
ATTN_SUB = 256
MIX_TILE = 512
MIX_SUB = 256
FFN_ROWS = 1024
FFN_SLAB = 512
MOD_COLS = 1536
VMEM_LIMIT = 56 * 1024 * 1024


def _nt(a, b):
    return lax.dot_general(a, b, (((1,), (1,)), ((), ())), preferred_element_type=F32)


def _nn(a, b):
    return jnp.dot(a, b, preferred_element_type=F32)


def _sigmoid(x):
    return 1.0 / (1.0 + jnp.exp(-x))


def _mod_kernel(c_ref, w_ref, b_ref, o_ref):
    c = c_ref[...]
    a = (c * _sigmoid(c)).astype(BF16)
    o_ref[0] = _nn(a, w_ref[0].astype(BF16)) + b_ref[0]


def _modulation(cond, w_mod, b_mod):
    nb = cond.shape[0]
    ncol = N_MOD * D_MODEL
    return pl.pallas_call(
        _mod_kernel,
        grid=(DEPTH, ncol // MOD_COLS),
        in_specs=[
            pl.BlockSpec((nb, D_MODEL), lambda l, j: (0, 0)),
            pl.BlockSpec((1, D_MODEL, MOD_COLS), lambda l, j: (l, 0, j)),
            pl.BlockSpec((1, 1, MOD_COLS), lambda l, j: (l, 0, j)),
        ],
        out_specs=pl.BlockSpec((1, nb, MOD_COLS), lambda l, j: (l, 0, j)),
        out_shape=jax.ShapeDtypeStruct((DEPTH, nb, ncol), F32),
        compiler_params=pltpu.CompilerParams(vmem_limit_bytes=VMEM_LIMIT),
        name="adaln_mod",
    )(cond, w_mod, b_mod.reshape(DEPTH, 1, ncol))


def _premix_kernel(x_ref, mod_ref, g1_ref, win_ref, gmat_ref, qg_ref, kg_ref, *rest, rope):
    kt_ref = vt_ref = None
    if rope:
        cos_ref, sina_ref, sinb_ref, q_ref, k_ref, v_ref, u_ref = rest
    else:
        q_ref, k_ref, v_ref, u_ref, kt_ref, vt_ref = rest[-6:]
    mod = mod_ref[0]
    sh1 = mod[0:1]
    sc1 = mod[1:2]

    def head_norm(t, g):
        sq = (t * t).astype(BF16)
        msq = jnp.concatenate([_nn(sq[:, c:c + MXU_TILE], gmat_ref[...])
                               for c in range(0, ATT_W, MXU_TILE)], axis=1)
        return t * lax.rsqrt(msq + EPS) * g

    def rotate(t, rs):
        outs = []
        for j in range(ATT_W // LANES):
            s = t[:, j * LANES:(j + 1) * LANES]
            outs.append(s * cos_ref[rs, :] + pltpu.roll(s, LANES - 16, 1) * sina_ref[rs, :]
                        + pltpu.roll(s, 16, 1) * sinb_ref[rs, :])
        return jnp.concatenate(outs, axis=1)

    for sb in range(x_ref.shape[0] // MIX_SUB):
        rs = slice(sb * MIX_SUB, (sb + 1) * MIX_SUB)
        x = x_ref[rs, :]
        ms = jnp.mean(x * x, axis=-1, keepdims=True)
        h = x * lax.rsqrt(ms + EPS) * g1_ref[...]
        h = h * (1.0 + sc1) + sh1
        qkvu = _nn(h.astype(BF16), win_ref[0])
        q = head_norm(qkvu[:, 0:ATT_W], qg_ref[...])
        k = head_norm(qkvu[:, ATT_W:2 * ATT_W], kg_ref[...])
        if rope:
            q = rotate(q, rs)
            k = rotate(k, rs)
        q_ref[rs, :] = (q * (D_SUB ** -0.5)).astype(q_ref.dtype)
        k_ref[rs, :] = k.astype(k_ref.dtype)
        v = qkvu[:, 2 * ATT_W:3 * ATT_W]
        if kt_ref is not None:
            kt = k.T
            for slot in range(kt_ref.shape[1]):
                kt_ref[sb, slot] = kt
                for hd in range(N_HEADS):
                    vt_ref[sb, slot, pl.ds(hd, MIX_SUB, stride=N_HEADS), :] = v[:, hd * D_V:(hd + 1) * D_V]
        v_ref[rs, :] = v.astype(v_ref.dtype)
        u_ref[rs, :] = qkvu[:, 3 * ATT_W:].astype(u_ref.dtype)


def _premix(x, mods_l, g1, win, layer, gmat, qg, kg, rope_tabs, seq_len, mod_base, carry=None):
    n = x.shape[0]
    tm = MIX_TILE
    per_seq = seq_len // tm
    rope = rope_tabs is not None
    assert rope or seq_len == MIX_SUB
    if mod_base == 0:
        mod_map = lambda i: (0, 0, 0)
    else:
        mod_map = lambda i: (mod_base + i // per_seq, 0, 0)
    row = lambda i: (i, 0)
    const = lambda i: (0, 0)
    in_specs = [
        pl.BlockSpec((tm, D_MODEL), row),
        pl.BlockSpec((1, N_MOD, D_MODEL), mod_map),
        pl.BlockSpec((1, D_MODEL), const),
        pl.BlockSpec((1, D_MODEL, 4 * ATT_W), lambda i: (layer, 0, 0)),
        pl.BlockSpec((MXU_TILE, MXU_TILE), const),
        pl.BlockSpec((1, ATT_W), const),
        pl.BlockSpec((1, ATT_W), const),
    ]
    args = [x, mods_l, g1, win, gmat, qg, kg]
    if rope:
        tab = lambda i: (i % per_seq, 0)
        in_specs += [pl.BlockSpec((tm, LANES), tab)] * 3
        args += list(rope_tabs)
    out = pl.BlockSpec((tm, ATT_W), row)
    out_specs = [out, out, out, out]
    out_shape = [jax.ShapeDtypeStruct((n, ATT_W), BF16),
                 jax.ShapeDtypeStruct((n, ATT_W), BF16),
                 jax.ShapeDtypeStruct((n, ATT_W), BF16),
                 jax.ShapeDtypeStruct((n, SSM_W), F32)]
    aliases = {}
    if not rope:
        nseq = tm // seq_len
        slots, first = (DEPTH, 0) if carry is None else (1, layer)
        out_specs.append(pl.BlockSpec((nseq, slots, ATT_W, seq_len), lambda i: (i, first, 0, 0)))
        out_shape.append(jax.ShapeDtypeStruct((n // seq_len, DEPTH, ATT_W, seq_len), F32))
        out_specs.append(pl.BlockSpec((nseq, slots, seq_len * N_HEADS, D_V), lambda i: (i, first, 0, 0)))
        out_shape.append(jax.ShapeDtypeStruct((n // seq_len, DEPTH, seq_len * N_HEADS, D_V), F32))
        if carry is not None:
            aliases = {len(args): 4, len(args) + 1: 5}
            in_specs += [pl.BlockSpec(memory_space=pl.ANY)] * 2
            args += list(carry)
    return pl.pallas_call(
        functools.partial(_premix_kernel, rope=rope),
        grid=(n // tm,),
        in_specs=in_specs,
        out_specs=out_specs,
        out_shape=out_shape,
        input_output_aliases=aliases,
        compiler_params=pltpu.CompilerParams(dimension_semantics=("parallel",),
                                             vmem_limit_bytes=VMEM_LIMIT),
        name="premix_rope" if rope else "premix",
    )(*args)


def _attn_kernel(q_ref, k_ref, v_ref, lamp_ref, sg_ref, *rest, lam_init, has_cache):
    if has_cache:
        kc_ref, vc_ref, o_ref = rest
    else:
        (o_ref,) = rest
    tq = q_ref.shape[0]
    lp = lamp_ref[...]
    l1 = jnp.sum(lp[0:1] * lp[1:2], axis=-1, keepdims=True)
    l2 = jnp.sum(lp[2:3] * lp[3:4], axis=-1, keepdims=True)
    lam = jnp.exp(l1) - jnp.exp(l2) + lam_init
    first = lax.broadcasted_iota(jnp.int32, (1, LANES), 1) < D_SUB
    tsub = min(ATTN_SUB, tq)
    for h in range(N_HEADS):
        sl = slice(h * D_V, (h + 1) * D_V)
        kn = k_ref[:, sl]
        vn = v_ref[:, sl]
        if has_cache:
            ncache = kc_ref.shape[3]
            kct = kc_ref[0, 0, sl, :].astype(BF16)
            vct = vc_ref[0, 0, pl.ds(h, ncache, stride=N_HEADS), :].astype(BF16)
        for r0 in range(0, tq, tsub):
            qh = q_ref[r0:r0 + tsub, sl]
            zero = jnp.zeros_like(qh)
            qq = jnp.concatenate([jnp.where(first, qh, zero), jnp.where(first, zero, qh)], axis=0)
            if has_cache:
                sc = jnp.concatenate([_nn(qq, kct), _nt(qq, kn)], axis=1)
            else:
                sc = _nt(qq, kn)
            m = jnp.max(sc, axis=-1, keepdims=True)
            e = jnp.exp(sc - m)
            ssum = jnp.sum(e, axis=-1, keepdims=True)
            eb = e.astype(BF16)
            if has_cache:
                pv = _nn(eb[:, :ncache], vct) + _nn(eb[:, ncache:], vn)
            else:
                pv = _nn(eb, vn)
            on = pv * (1.0 / ssum)
            o = on[:tsub] - lam * on[tsub:]
            ms = jnp.mean(o * o, axis=-1, keepdims=True)
            o_ref[r0:r0 + tsub, sl] = (o * lax.rsqrt(ms + EPS) * sg_ref[:, sl]
                                       * (1.0 - lam_init)).astype(o_ref.dtype)


def _attention(q, k, v, lamp, sg, cache, layer, batch, seq_len, lam_init):
    n = q.shape[0]
    tq = min(ATTN_TILE, seq_len)
    nq = seq_len // tq
    has_cache = cache is not None
    in_specs = [
        pl.BlockSpec((tq, ATT_W), lambda b, i: (b * nq + i, 0)),
        pl.BlockSpec((seq_len, ATT_W), lambda b, i: (b, 0)),
        pl.BlockSpec((seq_len, ATT_W), lambda b, i: (b, 0)),
        pl.BlockSpec((4, D_SUB), lambda b, i: (0, 0)),
        pl.BlockSpec((1, ATT_W), lambda b, i: (0, 0)),
    ]
    args = [q, k, v, lamp, sg]
    if has_cache:
        past = cache[0].shape[3]
        in_specs += [pl.BlockSpec((1, 1, ATT_W, past), lambda b, i: (b, layer, 0, 0)),
                     pl.BlockSpec((1, 1, past * N_HEADS, D_V), lambda b, i: (b, layer, 0, 0))]
        args += list(cache)
    return pl.pallas_call(
        functools.partial(_attn_kernel, lam_init=lam_init, has_cache=has_cache),
        grid=(batch, nq),
        in_specs=in_specs,
        out_specs=pl.BlockSpec((tq, ATT_W), lambda b, i: (b * nq + i, 0)),
        out_shape=jax.ShapeDtypeStruct((n, ATT_W), BF16),
        compiler_params=pltpu.CompilerParams(dimension_semantics=("parallel", "parallel"),
                                             vmem_limit_bytes=VMEM_LIMIT),
        name="attn_cache" if has_cache else "attn",
    )(*args)


def _ssm_kernel(ut_ref, h0_ref, p1_ref, pb_ref, yt_ref, hf_ref, r_scr, hin_scr, **sizes):
    for gi in range(ut_ref.shape[0]):
        one = lambda ref, gi=gi: ref.at[pl.ds(gi, 1)]
        _ssm_group(one(ut_ref), one(h0_ref), one(p1_ref), one(pb_ref), one(yt_ref), one(hf_ref),
                   r_scr.at[gi], hin_scr.at[gi], **sizes)


def _ssm_group(ut_ref, h0_ref, p1_ref, pb_ref, yt_ref, hf_ref, r_scr, hin_scr, *,
               ctx_batch, ctx_chunks, lat_batch, lat_chunks):
    T = CHUNK
    W = SSM_CH * T
    lane = lax.broadcasted_iota(jnp.int32, (1, LANES), 1)
    lo = lane < SSM_P
    sgn_lr = jnp.where(lo, 1.0, -1.0).astype(F32)
    sgn_rl = -sgn_lr
    sidx = lax.broadcasted_iota(jnp.int32, (T, LANES), 0)
    srow = sidx.astype(F32)

    def swap(a):
        return pltpu.roll(a, SSM_P, 1)

    def tile_time(a):
        return jnp.concatenate([jnp.broadcast_to(a[i:i + 1, :], (SSM_CH, LANES)) for i in range(T)], axis=0)

    def rep_chan(a):
        return jnp.concatenate([a] * T, axis=0)

    def pair_pack(y):
        ys = swap(y)
        return jnp.where(lo, y, ys), jnp.where(lo, ys, y) * sgn_rl

    def cmul_const(x, yr2, yi2s):
        return x * yr2 + swap(x) * yi2s

    p1 = p1_ref[0]
    pb = pb_ref[0]

    def direction(d):
        lre2 = p1[3 * d + 0:3 * d + 1]
        lim2 = p1[3 * d + 1:3 * d + 2]
        ls2 = p1[3 * d + 2:3 * d + 3]
        bre2, bim2, cre2, cim2 = pb[4 * d + 0], pb[4 * d + 1], pb[4 * d + 2], pb[4 * d + 3]
        delta = jnp.exp(ls2)
        xr = lre2 * delta
        th = lim2 * delta

        mag1 = jnp.exp(xr)
        lbr2 = mag1 * jnp.cos(th)
        lbi2 = mag1 * jnp.sin(th)
        den = lre2 * lre2 + lim2 * lim2
        nr = lbr2 - 1.0
        cr2 = (nr * lre2 + lbi2 * lim2) / den
        ci2 = (lbi2 * lre2 - nr * lim2) / den
        bbr16 = cr2 * bre2 - ci2 * bim2
        bbi16s = (cr2 * bim2 + ci2 * bre2) * sgn_rl
        bbr2 = rep_chan(bbr16)
        bbi2s = rep_chan(bbi16s)
        ccr2 = rep_chan(cre2)
        cci2s = rep_chan(cim2 * sgn_rl)

        pows = [pair_pack(jnp.where(lo, lbr2, lbi2))]
        for _ in range(T.bit_length() - 1):
            pows.append(pair_pack(cmul_const(jnp.where(lo, pows[-1][0], pows[-1][1]), *pows[-1])))
        one = jnp.where(lo, 1.0, 0.0) + jnp.zeros((T, LANES), F32)
        wp = wrev = one
        for j, pw in enumerate(pows[:-1]):
            bit = ((sidx >> j) & 1) == 1
            wp = jnp.where(bit, cmul_const(wp, *pw), wp)
            wrev = jnp.where(bit, wrev, cmul_const(wrev, *pw))

        def times(x, c1, c2s):
            return tile_time(x) * c1 + tile_time(swap(x)) * c2s

        lam_1 = pows[0]
        lam_t = pows[-1]
        if d == 0:
            clag = times(wp, ccr2, cci2s)
            bs = times(wrev, bbr2, bbi2s)
            cc = times(cmul_const(wp, *lam_1), ccr2, cci2s)
        else:
            clag = times(wrev, ccr2, cci2s)
            bs = times(wp, bbr2, bbi2s)
            cc = times(cmul_const(wrev, *lam_1), ccr2, cci2s)
        kt = _nt(jnp.where(lo, bbr16, -bbi16s).astype(BF16), clag.astype(BF16))
        bbar = jnp.where(lo, bbr2, bbi2s)
        return kt, bs, cc * sgn_lr, lam_t, bbar

    kt_f, bsf, ccf, lamt_f, bbar_f = direction(0)
    kt_b, bsb, ccb, lamt_b, bbar_b = direction(1)

    sub = LANES // SSM_CH
    right_f = [kt_f if b == 0 else pltpu.roll(kt_f, b * SSM_CH, 1) for b in range(sub)]
    left_b = [kt_b if b == 0 else pltpu.roll(kt_b, W - b * SSM_CH, 1) for b in range(sub)]

    def ring(x, lanes):
        return x if lanes % W == 0 else jnp.concatenate([x[:, W - lanes % W:], x[:, :W - lanes % W]], axis=1)

    blocks = []
    for s in range(T):
        a, b = divmod(s, sub)
        a2, b2 = divmod(T - 1 - s, sub)
        fwd = ring(right_f[b], a * LANES)
        bwd = ring(left_b[b2], W - a2 * LANES)
        first, end = s * SSM_CH, (s + 1) * SSM_CH
        cols = []
        for v in range(W // LANES):
            lo_l, hi_l = v * LANES, (v + 1) * LANES
            fv, bv = fwd[:, lo_l:hi_l], bwd[:, lo_l:hi_l]
            parts = []
            if lo_l >= first:
                parts.append(fv)
            elif hi_l > first:
                parts.append(jnp.where(lane >= first - lo_l, fv, 0.0))
            if hi_l <= end:
                parts.append(bv)
            elif lo_l < end:
                parts.append(jnp.where(lane < end - lo_l, bv, 0.0))
            cols.append(parts[0] if len(parts) == 1 else parts[0] + parts[1])
        blocks.append(jnp.concatenate(cols, axis=1))
    a_t = jnp.concatenate(blocks, axis=0)
    w1 = jnp.concatenate([a_t, bsf, swap(bsf), bsb, swap(bsb)], axis=1).astype(BF16)
    r_scr[...] = _nn(ut_ref[0], w1)

    def chain(row0, nb, nchunks, col, lam_t, h, reverse):
        a1, a2s = lam_t
        hs = swap(h)
        order = range(nchunks - 1, -1, -1) if reverse else range(nchunks)
        hcol = slice(LANES, 2 * LANES) if reverse else slice(0, LANES)
        for c in order:
            rows = slice(row0 + c * nb, row0 + (c + 1) * nb)
            hin_scr[rows, hcol] = h
            s = r_scr[rows, col:col + LANES]
            ss = r_scr[rows, col + LANES:col + 2 * LANES]
            h, hs = h * a1 + hs * a2s + s, hs * a1 - h * a2s + ss
        return h

    zero = jnp.zeros((ctx_batch, LANES), F32)
    chain(0, ctx_batch, ctx_chunks, W, lamt_f, zero, False)
    chain(0, ctx_batch, ctx_chunks, W + 2 * LANES, lamt_b, zero, True)
    s_tiled = tile_time(srow)
    first_tok = jnp.where(s_tiled == 0.0, bbar_f, 0.0).astype(BF16)
    last_tok = jnp.where(s_tiled == float(T - 1), bbar_b, 0.0).astype(BF16)
    last_rows = (ctx_chunks - 1) * ctx_batch
    hf_ref[0, :, 0:LANES] = _nn(ut_ref[0, 0:ctx_batch, :], first_tok)
    hf_ref[0, :, LANES:2 * LANES] = _nn(ut_ref[0, last_rows:last_rows + ctx_batch, :], last_tok)
    lat0 = ctx_batch * ctx_chunks
    h0 = h0_ref[0]
    chain(lat0, lat_batch, lat_chunks, W, lamt_f, h0[:, 0:LANES], False)
    chain(lat0, lat_batch, lat_chunks, W + 2 * LANES, lamt_b, h0[:, LANES:2 * LANES], True)

    cc_cat = jnp.concatenate([ccf, ccb], axis=1).astype(BF16)
    yt_ref[0] = (r_scr[:, 0:W] + _nt(hin_scr[...].astype(BF16), cc_cat)).astype(yt_ref.dtype)


def _ssm(ut, h0, p1, pb, ctx_batch, ctx_chunks, lat_batch, lat_chunks):
    g, n, w = ut.shape
    gs = S5_GROUPS_PER_STEP
    return pl.pallas_call(
        functools.partial(_ssm_kernel, ctx_batch=ctx_batch, ctx_chunks=ctx_chunks,
                          lat_batch=lat_batch, lat_chunks=lat_chunks),
        grid=(g // gs,),
        in_specs=[
            pl.BlockSpec((gs, n, w), lambda i: (i, 0, 0)),
            pl.BlockSpec((gs, lat_batch, 2 * LANES), lambda i: (i, 0, 0)),
            pl.BlockSpec((gs, 8, LANES), lambda i: (i, 0, 0)),
            pl.BlockSpec((gs, 8, SSM_CH, LANES), lambda i: (i, 0, 0, 0)),
        ],
        out_specs=[
            pl.BlockSpec((gs, n, w), lambda i: (i, 0, 0)),
            pl.BlockSpec((gs, ctx_batch, 2 * LANES), lambda i: (i, 0, 0)),
        ],
        out_shape=[jax.ShapeDtypeStruct((g, n, w), BF16),
                   jax.ShapeDtypeStruct((g, ctx_batch, 2 * LANES), F32)],
        scratch_shapes=[pltpu.VMEM((gs, n, w + 4 * LANES), F32), pltpu.VMEM((gs, n, 2 * LANES), F32)],
        compiler_params=pltpu.CompilerParams(dimension_semantics=("parallel",),
                                             vmem_limit_bytes=VMEM_LIMIT),
        name="s5_scan",
    )(ut, h0, p1, pb)


def _tail_kernel(x_ref, oatt_ref, y_ref, u_ref, mod_ref, d_ref, wglu_ref, bglu_ref, wout_ref, g2_ref,
                 wup_ref, cw_ref, cb_ref, wd_ref, o_ref, x1_scr, h2_scr, *, seq_len):
    rows = x_ref.shape[0]
    tf = FFN_SLAB
    mod = mod_ref[0]
    gt1 = mod[2:3]
    sh2 = mod[3:4]
    sc2 = mod[4:5]
    gt2 = mod[5:6]
    for sb in range(rows // MIX_SUB):
        rs = slice(sb * MIX_SUB, (sb + 1) * MIX_SUB)
        y = y_ref[rs, :].astype(F32) + d_ref[...] * u_ref[rs, :]
        z = jax.nn.gelu(y)
        gate = _sigmoid(_nn(z.astype(BF16), wglu_ref[0]) + bglu_ref[...])
        o_ssm = (z * gate).astype(BF16)
        mix = _nn(oatt_ref[rs, :], wout_ref[0, 0:ATT_W, :]) + _nn(o_ssm, wout_ref[0, ATT_W:, :])
        x1 = x_ref[rs, :] + gt1 * mix
        x1_scr[rs, :] = x1
        ms = jnp.mean(x1 * x1, axis=-1, keepdims=True)
        h2 = x1 * lax.rsqrt(ms + EPS) * g2_ref[...]
        h2_scr[rs, :] = (h2 * (1.0 + sc2) + sh2).astype(BF16)

    row8 = lax.broadcasted_iota(jnp.int32, (8, 1), 0)
    h2 = h2_scr[...]

    def conv(up, cw, cb):
        prev = pltpu.roll(up, 1, 0)
        nxt = pltpu.roll(up, rows - 1, 0)
        pp, nn = [], []
        for r in range(0, rows, seq_len):
            e = r + seq_len
            pp += [jnp.where(row8 == 0, 0.0, prev[r:r + 8]), prev[r + 8:e]]
            nn += [nxt[r:e - 8], jnp.where(row8 == 7, 0.0, nxt[e - 8:e])]
        prev = jnp.concatenate(pp, axis=0)
        nxt = jnp.concatenate(nn, axis=0)
        return prev * cw[0:1] + up * cw[1:2] + nxt * cw[2:3] + cb

    acts = []
    for j in range(D_FF // tf):
        cv = slice(j * tf, (j + 1) * tf)
        cg = slice(D_FF + j * tf, D_FF + (j + 1) * tf)
        val = conv(_nn(h2, wup_ref[0, :, cv]), cw_ref[:, cv], cb_ref[:, cv])
        gate = conv(_nn(h2, wup_ref[0, :, cg]), cw_ref[:, cg], cb_ref[:, cg])
        acts.append((gate * _sigmoid(gate) * val).astype(BF16))
    act = jnp.concatenate(acts, axis=1)
    o_ref[...] = x1_scr[...] + gt2 * _nn(act, wd_ref[0])


def _tail(x, oatt, y, u, mods_l, d, wglu, bglu, wout, g2, wup, cw, cb, wdown, layer, seq_len, mod_base):
    n = x.shape[0]
    rows = FFN_ROWS
    per_seq = seq_len // rows if seq_len >= rows else 1
    if mod_base == 0:
        mod_map = lambda i: (0, 0, 0)
    else:
        mod_map = lambda i: (mod_base + i // per_seq, 0, 0)
    row = lambda i: (i, 0)
    const = lambda i: (0, 0)
    lay = lambda i: (layer, 0, 0)
    resident = dict(pipeline_mode=pl.Buffered(1))
    return pl.pallas_call(
        functools.partial(_tail_kernel, seq_len=seq_len),
        grid=(n // rows,),
        in_specs=[
            pl.BlockSpec((rows, D_MODEL), row),
            pl.BlockSpec((rows, ATT_W), row),
            pl.BlockSpec((rows, SSM_W), row),
            pl.BlockSpec((rows, SSM_W), row),
            pl.BlockSpec((1, N_MOD, D_MODEL), mod_map),
            pl.BlockSpec((1, SSM_W), const),
            pl.BlockSpec((1, SSM_W, SSM_W), lay, **resident),
            pl.BlockSpec((1, SSM_W), const),
            pl.BlockSpec((1, D_MODEL, D_MODEL), lay, **resident),
            pl.BlockSpec((1, D_MODEL), const),
            pl.BlockSpec((1, D_MODEL, 2 * D_FF), lay, **resident),
            pl.BlockSpec((3, 2 * D_FF), const),
            pl.BlockSpec((1, 2 * D_FF), const),
            pl.BlockSpec((1, D_FF, D_MODEL), lay, **resident),
        ],
        out_specs=pl.BlockSpec((rows, D_MODEL), row),
        out_shape=jax.ShapeDtypeStruct((n, D_MODEL), F32),
        scratch_shapes=[pltpu.VMEM((rows, D_MODEL), F32), pltpu.VMEM((rows, D_MODEL), BF16)],
        compiler_params=pltpu.CompilerParams(dimension_semantics=("parallel",),
                                             vmem_limit_bytes=VMEM_LIMIT),
        name="layer_tail",
    )(x, oatt, y, u, mods_l, d, wglu, bglu, wout, g2, wup, cw, cb, wdown)


def _rope_tables(n_tokens):
    pos = np.arange(n_tokens)
    row = (pos // GRID_W).astype(np.float64)
    col = (pos % GRID_W).astype(np.float64)
    nf = ROPE_AXIS // 2
    inv_freq = 1.0 / (ROPE_THETA ** (np.arange(nf, dtype=np.float64) / nf))
    lane = np.arange(LANES)
    freq = inv_freq[lane % nf]
    is_col = (lane % D_SUB) >= ROPE_AXIS
    ang = np.where(is_col[None, :], col[:, None], row[:, None]) * freq[None, :]
    second = (lane % ROPE_AXIS) >= nf
    cos = np.cos(ang)
    sin = np.sin(ang)
    sina = np.where(second[None, :], 0.0, -sin)
    sinb = np.where(second[None, :], sin, 0.0)
    return tuple(jnp.asarray(t, dtype=F32) for t in (cos, sina, sinb))


def _chunk_layout_passes(ctx_batch, ctx_len, lat_batch, lat_len):
    return ((0, ctx_batch, ctx_len, 0), (1, lat_batch, lat_len, ctx_batch * (ctx_len // CHUNK)))


def _row_permutation(batch):
    n = batch * CHUNK
    p = np.zeros((n, n), np.float32)
    b, s = np.meshgrid(np.arange(batch), np.arange(CHUNK), indexing="ij")
    p[(s * batch + b).ravel(), (b * CHUNK + s).ravel()] = 1.0
    return p


def _lane_permutation():
    n = GROUPS_PER_BLOCK * LANES
    q = np.zeros((n, n), np.float32)
    t, g, ch = np.meshgrid(np.arange(GROUPS_PER_BLOCK), np.arange(GROUPS_PER_BLOCK), np.arange(SSM_CH),
                           indexing="ij")
    q[(t * LANES + g * SSM_CH + ch).ravel(), (g * LANES + t * SSM_CH + ch).ravel()] = 1.0
    return q


def _to_chunks_kernel(uc_ref, ul_ref, pc_ref, pl_ref, q_ref, o_ref, v_scr, *, passes):
    for idx, batch, seq_len, row0 in passes:
        u_ref = (uc_ref, ul_ref)[idx]
        p_ref = (pc_ref, pl_ref)[idx]
        for c0 in range(0, seq_len // CHUNK, CHUNK_PAIR):
            x = jnp.concatenate(
                [jnp.concatenate([u_ref[b * seq_len + c * CHUNK:b * seq_len + (c + 1) * CHUNK, :]
                                  for b in range(batch)], axis=0) for c in range(c0, c0 + CHUNK_PAIR)], axis=1)
            r2 = _nn(p_ref[...], x.astype(BF16))
            for k in range(CHUNK_PAIR):
                c = c0 + k
                r = r2[:, k * LANES:(k + 1) * LANES]
                rows = slice(row0 + c * batch, row0 + (c + 1) * batch)
                for s in range(CHUNK):
                    j, t = divmod(s, GROUPS_PER_BLOCK)
                    v_scr[j, rows, t * LANES:(t + 1) * LANES] = r[s * batch:(s + 1) * batch]
    for j in range(TIME_BLOCKS):
        w = _nn(v_scr[j].astype(BF16), q_ref[...])
        for g in range(GROUPS_PER_BLOCK):
            o_ref[g, :, j * LANES:(j + 1) * LANES] = w[:, g * LANES:(g + 1) * LANES].astype(o_ref.dtype)


def _from_chunks_kernel(y_ref, pc_ref, pl_ref, q_ref, oc_ref, ol_ref, v_scr, *, passes):
    for j in range(TIME_BLOCKS):
        yj = jnp.concatenate([y_ref[g, :, j * LANES:(j + 1) * LANES] for g in range(GROUPS_PER_BLOCK)], axis=1)
        v_scr[j] = _nn(yj, q_ref[...])
    for idx, batch, seq_len, row0 in passes:
        o_ref = (oc_ref, ol_ref)[idx]
        p_ref = (pc_ref, pl_ref)[idx]
        for c0 in range(0, seq_len // CHUNK, CHUNK_PAIR):
            xs = []
            for c in range(c0, c0 + CHUNK_PAIR):
                rows = slice(row0 + c * batch, row0 + (c + 1) * batch)
                xs.append(jnp.concatenate(
                    [v_scr[s // GROUPS_PER_BLOCK, rows,
                           (s % GROUPS_PER_BLOCK) * LANES:(s % GROUPS_PER_BLOCK + 1) * LANES]
                     for s in range(CHUNK)], axis=0))
            r2 = _nn(p_ref[...], jnp.concatenate(xs, axis=1).astype(BF16))
            for k in range(CHUNK_PAIR):
                c = c0 + k
                for b in range(batch):
                    o_ref[b * seq_len + c * CHUNK:b * seq_len + (c + 1) * CHUNK, :] = (
                        r2[b * CHUNK:(b + 1) * CHUNK, k * LANES:(k + 1) * LANES].astype(o_ref.dtype))


def _to_chunks(uc, ul, ctx_batch, ctx_len, lat_batch, lat_len):
    passes = _chunk_layout_passes(ctx_batch, ctx_len, lat_batch, lat_len)
    rows = ctx_batch * (ctx_len // CHUNK) + lat_batch * (lat_len // CHUNK)
    pc = jnp.asarray(_row_permutation(ctx_batch), dtype=BF16)
    pn = jnp.asarray(_row_permutation(lat_batch), dtype=BF16)
    q = jnp.asarray(_lane_permutation(), dtype=BF16)
    const = lambda j: (0, 0)
    return pl.pallas_call(
        functools.partial(_to_chunks_kernel, passes=passes),
        grid=(SSM_GROUPS // GROUPS_PER_BLOCK,),
        in_specs=[pl.BlockSpec((uc.shape[0], LANES), lambda j: (0, j)),
                  pl.BlockSpec((ul.shape[0], LANES), lambda j: (0, j)),
                  pl.BlockSpec(pc.shape, const), pl.BlockSpec(pn.shape, const), pl.BlockSpec(q.shape, const)],
        out_specs=pl.BlockSpec((GROUPS_PER_BLOCK, rows, SSM_CH * CHUNK), lambda j: (j, 0, 0)),
        out_shape=jax.ShapeDtypeStruct((SSM_GROUPS, rows, SSM_CH * CHUNK), BF16),
        scratch_shapes=[pltpu.VMEM((TIME_BLOCKS, rows, GROUPS_PER_BLOCK * LANES), F32)],
        compiler_params=pltpu.CompilerParams(dimension_semantics=("parallel",),
                                             vmem_limit_bytes=VMEM_LIMIT),
        name="to_chunks",
    )(uc, ul, pc, pn, q)


def _from_chunks(yt, ctx_batch, ctx_len, lat_batch, lat_len):
    passes = _chunk_layout_passes(ctx_batch, ctx_len, lat_batch, lat_len)
    rows = yt.shape[1]
    n_ctx = ctx_batch * ctx_len
    n_lat = lat_batch * lat_len
    pc = jnp.asarray(_row_permutation(ctx_batch).T, dtype=BF16)
    pn = jnp.asarray(_row_permutation(lat_batch).T, dtype=BF16)
    q = jnp.asarray(_lane_permutation().T, dtype=BF16)
    const = lambda j: (0, 0)
    return pl.pallas_call(
        functools.partial(_from_chunks_kernel, passes=passes),
        grid=(SSM_GROUPS // GROUPS_PER_BLOCK,),
        in_specs=[pl.BlockSpec((GROUPS_PER_BLOCK, rows, SSM_CH * CHUNK), lambda j: (j, 0, 0)),
                  pl.BlockSpec(pc.shape, const), pl.BlockSpec(pn.shape, const), pl.BlockSpec(q.shape, const)],
        out_specs=[pl.BlockSpec((n_ctx, LANES), lambda j: (0, j)),
                   pl.BlockSpec((n_lat, LANES), lambda j: (0, j))],
        out_shape=[jax.ShapeDtypeStruct((n_ctx, SSM_W), BF16), jax.ShapeDtypeStruct((n_lat, SSM_W), BF16)],
        scratch_shapes=[pltpu.VMEM((TIME_BLOCKS, rows, GROUPS_PER_BLOCK * LANES), F32)],
        compiler_params=pltpu.CompilerParams(dimension_semantics=("parallel",),
                                             vmem_limit_bytes=VMEM_LIMIT),
        name="from_chunks",
    )(yt, pc, pn, q)


def _dup_lanes(a):
    return jnp.concatenate([a, a], axis=-1)


def kernel(x_prompt, x_sample, cache_k, cache_v, state_ssm, c, c_ctx, w_mod, b_mod, g_norm1, w_in, q_norm, k_norm, lambda_q1, lambda_k1, lambda_q2, lambda_k2, subln_g, ssm_lambda_re, ssm_lambda_im, ssm_log_step, ssm_b_re, ssm_b_im, ssm_c_re, ssm_c_im, ssm_d, w_glu, b_glu, w_out, g_norm2, w_up, conv_w, conv_b, w_down):
    nb_ctx, len_ctx, _ = x_prompt.shape
    nb_lat, len_lat, _ = x_sample.shape
    past = cache_k.shape[2]

    n_cond = 16
    cond = jnp.zeros((n_cond, D_MODEL), F32).at[0].set(c_ctx).at[1:1 + nb_lat].set(c)
    mods = _modulation(cond, w_mod, b_mod).reshape(DEPTH, n_cond, N_MOD, D_MODEL)

    gidx = np.arange(MXU_TILE) // D_SUB
    gmat = jnp.asarray(np.where(gidx[:, None] == gidx[None, :], 1.0 / D_SUB, 0.0), dtype=BF16)
    cache_kt = cache_k.transpose(0, 1, 3, 4, 5, 2).reshape(nb_lat, DEPTH, ATT_W, past)
    cache_vr = cache_v.reshape(nb_lat, DEPTH, past * N_HEADS, D_V)
    rope_tabs = _rope_tables(len_lat)
    win_b = w_in.astype(BF16)
    wout_b = w_out.astype(BF16)
    wglu_b = w_glu.astype(BF16)
    wup_b = w_up.astype(BF16)
    wdown_b = w_down.astype(BF16)

    xp = x_prompt.reshape(nb_ctx * len_ctx, D_MODEL)
    xs = x_sample.reshape(nb_lat * len_lat, D_MODEL)
    ss = []
    kv_out = None
    for l in range(DEPTH):
        lam_init = 0.8 - 0.6 * math.exp(-0.3 * l)
        mods_l = mods[l]
        g1 = g_norm1[l].reshape(1, D_MODEL)
        g2 = g_norm2[l].reshape(1, D_MODEL)
        qg = jnp.tile(q_norm[l], ATT_W // D_SUB).reshape(1, ATT_W)
        kg = jnp.tile(k_norm[l], ATT_W // D_SUB).reshape(1, ATT_W)
        sg = jnp.tile(subln_g[l], N_HEADS).reshape(1, ATT_W)
        lamp = jnp.stack([lambda_q1[l], lambda_k1[l], lambda_q2[l], lambda_k2[l]])
        d = ssm_d[l].reshape(1, SSM_W)
        bglu = b_glu[l].reshape(1, SSM_W)
        cb = conv_b[l].reshape(1, 2 * D_FF)
        cw = conv_w[l]

        zeros = jnp.zeros((SSM_GROUPS, SSM_P), F32)
        step = jnp.broadcast_to(ssm_log_step[l][:, :, None], (2, SSM_GROUPS, SSM_P))
        p1 = _dup_lanes(jnp.stack([ssm_lambda_re[l, 0], ssm_lambda_im[l, 0], step[0],
                                   ssm_lambda_re[l, 1], ssm_lambda_im[l, 1], step[1], zeros, zeros], axis=1))
        bt = lambda a: a.transpose(0, 2, 1)
        pb = _dup_lanes(jnp.stack([bt(ssm_b_re[l, 0]), bt(ssm_b_im[l, 0]), ssm_c_re[l, 0], ssm_c_im[l, 0],
                                   bt(ssm_b_re[l, 1]), bt(ssm_b_im[l, 1]), ssm_c_re[l, 1], ssm_c_im[l, 1]], axis=1))
        h0 = state_ssm[:, l].transpose(2, 0, 1, 4, 3).reshape(SSM_GROUPS, nb_lat, 4 * SSM_P)

        qc, kc_new, vc_new, uc, kt, vt = _premix(xp, mods_l, g1, win_b, l, gmat, qg, kg, None, len_ctx, 0,
                                                 carry=kv_out)
        ql, kl, vl, ul = _premix(xs, mods_l, g1, win_b, l, gmat, qg, kg, rope_tabs, len_lat, 1)

        oc = _attention(qc, kc_new, vc_new, lamp, sg, None, l, nb_ctx, len_ctx, lam_init)
        ol = _attention(ql, kl, vl, lamp, sg, (cache_kt, cache_vr), l, nb_lat, len_lat, lam_init)

        ut = _to_chunks(uc, ul, nb_ctx, len_ctx, nb_lat, len_lat)
        yt, hfin = _ssm(ut, h0, p1, pb, nb_ctx, len_ctx // CHUNK, nb_lat, len_lat // CHUNK)
        yc, yl = _from_chunks(yt, nb_ctx, len_ctx, nb_lat, len_lat)

        xp = _tail(xp, oc, yc, uc, mods_l, d, wglu_b, bglu, wout_b, g2, wup_b, cw, cb, wdown_b, l, len_ctx, 0)
        xs = _tail(xs, ol, yl, ul, mods_l, d, wglu_b, bglu, wout_b, g2, wup_b, cw, cb, wdown_b, l, len_lat, 1)

        kv_out = (kt, vt)
        ss.append(hfin.reshape(SSM_GROUPS, nb_ctx, 2, 2, SSM_P).transpose(1, 2, 0, 4, 3))

    new_k = kv_out[0].reshape(nb_ctx, DEPTH, N_HEADS, 2, D_SUB, len_ctx).transpose(0, 1, 5, 2, 3, 4)
    new_v = kv_out[1].reshape(nb_ctx, DEPTH, len_ctx, N_HEADS, D_V)
    return (xp.reshape(nb_ctx, len_ctx, D_MODEL), xs.reshape(nb_lat, len_lat, D_MODEL),
            new_k, new_v, jnp.stack(ss, axis=1))
```

```python
import functools
import math

import numpy as np
import jax
import jax.numpy as jnp
from jax import lax
from jax.experimental import pallas as pl
from jax.experimental.pallas import tpu as pltpu

F32 = jnp.float32
BF16 = jnp.bfloat16

D_MODEL = 1024
DEPTH = 2
GRID_W = 64
ATT_W = 512
SSM_W = 512
N_HEADS = 4
D_V = 128
D_SUB = 64
ROPE_AXIS = 32
ROPE_THETA = 10000.0
SSM_CH = 16
SSM_GROUPS = 32
SSM_P = 64
D_FF = 2048
N_MOD = 6
EPS = 1e-6

LANES = 128
MXU_TILE = 256
GROUPS_PER_BLOCK = LANES // SSM_CH
CHUNK = 32
S5_GROUPS_PER_STEP = 4
TIME_BLOCKS = CHUNK // GROUPS_PER_BLOCK
CHUNK_PAIR = MXU_TILE // LANES
ATTN_TILE = 1024
ATTN_SUB = 256
MIX_TILE = 512
MIX_SUB = 256
FFN_ROWS = 1024
FFN_SLAB = 512
MOD_COLS = 1536
VMEM_LIMIT = 56 * 1024 * 1024


def _nt(a, b):
    return lax.dot_general(a, b, (((1,), (1,)), ((), ())), preferred_element_type=F32)


def _nn(a, b):
    return jnp.dot(a, b, preferred_element_type=F32)


def _sigmoid(x):
    return 1.0 / (1.0 + jnp.exp(-x))


def _mod_kernel(c_ref, w_ref, b_ref, o_ref):
    c = c_ref[...]
    a = (c * _sigmoid(c)).astype(BF16)
    o_ref[0] = _nn(a, w_ref[0].astype(BF16)) + b_ref[0]


def _modulation(cond, w_mod, b_mod):
    nb = cond.shape[0]
    ncol = N_MOD * D_MODEL
    return pl.pallas_call(
        _mod_kernel,
        grid=(DEPTH, ncol // MOD_COLS),
        in_specs=[
            pl.BlockSpec((nb, D_MODEL), lambda l, j: (0, 0)),
            pl.BlockSpec((1, D_MODEL, MOD_COLS), lambda l, j: (l, 0, j)),
            pl.BlockSpec((1, 1, MOD_COLS), lambda l, j: (l, 0, j)),
        ],
        out_specs=pl.BlockSpec((1, nb, MOD_COLS), lambda l, j: (l, 0, j)),
        out_shape=jax.ShapeDtypeStruct((DEPTH, nb, ncol), F32),
        compiler_params=pltpu.CompilerParams(vmem_limit_bytes=VMEM_LIMIT),
        name="adaln_mod",
    )(cond, w_mod, b_mod.reshape(DEPTH, 1, ncol))


def _premix_kernel(x_ref, mod_ref, g1_ref, win_ref, gmat_ref, qg_ref, kg_ref, *rest, rope):
    kt_ref = vt_ref = None
    if rope:
        cos_ref, sina_ref, sinb_ref, q_ref, k_ref, v_ref, u_ref = rest
    else:
        q_ref, k_ref, v_ref, u_ref, kt_ref, vt_ref = rest[-6:]
    mod = mod_ref[0]
    sh1 = mod[0:1]
    sc1 = mod[1:2]

    def head_norm(t, g):
        sq = (t * t).astype(BF16)
        msq = jnp.concatenate([_nn(sq[:, c:c + MXU_TILE], gmat_ref[...])
                               for c in range(0, ATT_W, MXU_TILE)], axis=1)
        return t * lax.rsqrt(msq + EPS) * g

    def rotate(t, rs):
        outs = []
        for j in range(ATT_W // LANES):
            s = t[:, j * LANES:(j + 1) * LANES]
            outs.append(s * cos_ref[rs, :] + pltpu.roll(s, LANES - 16, 1) * sina_ref[rs, :]
                        + pltpu.roll(s, 16, 1) * sinb_ref[rs, :])
        return jnp.concatenate(outs, axis=1)

    for sb in range(x_ref.shape[0] // MIX_SUB):
        rs = slice(sb * MIX_SUB, (sb + 1) * MIX_SUB)
        x = x_ref[rs, :]
        ms = jnp.mean(x * x, axis=-1, keepdims=True)
        h = x * lax.rsqrt(ms + EPS) * g1_ref[...]
        h = h * (1.0 + sc1) + sh1
        qkvu = _nn(h.astype(BF16), win_ref[0])
        q = head_norm(qkvu[:, 0:ATT_W], qg_ref[...])
        k = head_norm(qkvu[:, ATT_W:2 * ATT_W], kg_ref[...])
        if rope:
            q = rotate(q, rs)
            k = rotate(k, rs)
        q_ref[rs, :] = (q * (D_SUB ** -0.5)).astype(q_ref.dtype)
        k_ref[rs, :] = k.astype(k_ref.dtype)
        v = qkvu[:, 2 * ATT_W:3 * ATT_W]
        if kt_ref is not None:
            kt = k.T
            for slot in range(kt_ref.shape[1]):
                kt_ref[sb, slot] = kt
                for hd in range(N_HEADS):
                    vt_ref[sb, slot, pl.ds(hd, MIX_SUB, stride=N_HEADS), :] = v[:, hd * D_V:(hd + 1) * D_V]
        v_ref[rs, :] = v.astype(v_ref.dtype)
        u_ref[rs, :] = qkvu[:, 3 * ATT_W:].astype(u_ref.dtype)


def _premix(x, mods_l, g1, win, layer, gmat, qg, kg, rope_tabs, seq_len, mod_base, carry=None):
    n = x.shape[0]
    tm = MIX_TILE
    per_seq = seq_len // tm
    rope = rope_tabs is not None
    assert rope or seq_len == MIX_SUB
    if mod_base == 0:
        mod_map = lambda i: (0, 0, 0)
    else:
        mod_map = lambda i: (mod_base + i // per_seq, 0, 0)
    row = lambda i: (i, 0)
    const = lambda i: (0, 0)
    in_specs = [
        pl.BlockSpec((tm, D_MODEL), row),
        pl.BlockSpec((1, N_MOD, D_MODEL), mod_map),
        pl.BlockSpec((1, D_MODEL), const),
        pl.BlockSpec((1, D_MODEL, 4 * ATT_W), lambda i: (layer, 0, 0)),
        pl.BlockSpec((MXU_TILE, MXU_TILE), const),
        pl.BlockSpec((1, ATT_W), const),
        pl.BlockSpec((1, ATT_W), const),
    ]
    args = [x, mods_l, g1, win, gmat, qg, kg]
    if rope:
        tab = lambda i: (i % per_seq, 0)
        in_specs += [pl.BlockSpec((tm, LANES), tab)] * 3
        args += list(rope_tabs)
    out = pl.BlockSpec((tm, ATT_W), row)
    out_specs = [out, out, out, out]
    out_shape = [jax.ShapeDtypeStruct((n, ATT_W), BF16),
                 jax.ShapeDtypeStruct((n, ATT_W), BF16),
                 jax.ShapeDtypeStruct((n, ATT_W), BF16),
                 jax.ShapeDtypeStruct((n, SSM_W), F32)]
    aliases = {}
    if not rope:
        nseq = tm // seq_len
        slots, first = (DEPTH, 0) if carry is None else (1, layer)
        out_specs.append(pl.BlockSpec((nseq, slots, ATT_W, seq_len), lambda i: (i, first, 0, 0)))
        out_shape.append(jax.ShapeDtypeStruct((n // seq_len, DEPTH, ATT_W, seq_len), F32))
        out_specs.append(pl.BlockSpec((nseq, slots, seq_len * N_HEADS, D_V), lambda i: (i, first, 0, 0)))
        out_shape.append(jax.ShapeDtypeStruct((n // seq_len, DEPTH, seq_len * N_HEADS, D_V), F32))
        if carry is not None:
            aliases = {len(args): 4, len(args) + 1: 5}
            in_specs += [pl.BlockSpec(memory_space=pl.ANY)] * 2
            args += list(carry)
    return pl.pallas_call(
        functools.partial(_premix_kernel, rope=rope),
        grid=(n // tm,),
        in_specs=in_specs,
        out_specs=out_specs,
        out_shape=out_shape,
        input_output_aliases=aliases,
        compiler_params=pltpu.CompilerParams(dimension_semantics=("parallel",),
                                             vmem_limit_bytes=VMEM_LIMIT),
        name="premix_rope" if rope else "premix",
    )(*args)


def _attn_kernel(q_ref, k_ref, v_ref, lamp_ref, sg_ref, *rest, lam_init, has_cache):
    if has_cache:
        kc_ref, vc_ref, o_ref = rest
    else:
        (o_ref,) = rest
    tq = q_ref.shape[0]
    lp = lamp_ref[...]
    l1 = jnp.sum(lp[0:1] * lp[1:2], axis=-1, keepdims=True)
    l2 = jnp.sum(lp[2:3] * lp[3:4], axis=-1, keepdims=True)
    lam = jnp.exp(l1) - jnp.exp(l2) + lam_init
    first = lax.broadcasted_iota(jnp.int32, (1, LANES), 1) < D_SUB
    tsub = min(ATTN_SUB, tq)
    for h in range(N_HEADS):
        sl = slice(h * D_V, (h + 1) * D_V)
        kn = k_ref[:, sl]
        vn = v_ref[:, sl]
        if has_cache:
            ncache = kc_ref.shape[3]
            kct = kc_ref[0, 0, sl, :].astype(BF16)
            vct = vc_ref[0, 0, pl.ds(h, ncache, stride=N_HEADS), :].astype(BF16)
        for r0 in range(0, tq, tsub):
            qh = q_ref[r0:r0 + tsub, sl]
            zero = jnp.zeros_like(qh)
            qq = jnp.concatenate([jnp.where(first, qh, zero), jnp.where(first, zero, qh)], axis=0)
            if has_cache:
                sc = jnp.concatenate([_nn(qq, kct), _nt(qq, kn)], axis=1)
            else:
                sc = _nt(qq, kn)
            m = jnp.max(sc, axis=-1, keepdims=True)
            e = jnp.exp(sc - m)
            ssum = jnp.sum(e, axis=-1, keepdims=True)
            eb = e.astype(BF16)
            if has_cache:
                pv = _nn(eb[:, :ncache], vct) + _nn(eb[:, ncache:], vn)
            else:
                pv = _nn(eb, vn)
            on = pv * (1.0 / ssum)
            o = on[:tsub] - lam * on[tsub:]
            ms = jnp.mean(o * o, axis=-1, keepdims=True)
            o_ref[r0:r0 + tsub, sl] = (o * lax.rsqrt(ms + EPS) * sg_ref[:, sl]
                                       * (1.0 - lam_init)).astype(o_ref.dtype)


def _attention(q, k, v, lamp, sg, cache, layer, batch, seq_len, lam_init):
    n = q.shape[0]
    tq = min(ATTN_TILE, seq_len)
    nq = seq_len // tq
    has_cache = cache is not None
    in_specs = [
        pl.BlockSpec((tq, ATT_W), lambda b, i: (b * nq + i, 0)),
        pl.BlockSpec((seq_len, ATT_W), lambda b, i: (b, 0)),
        pl.BlockSpec((seq_len, ATT_W), lambda b, i: (b, 0)),
        pl.BlockSpec((4, D_SUB), lambda b, i: (0, 0)),
        pl.BlockSpec((1, ATT_W), lambda b, i: (0, 0)),
    ]
    args = [q, k, v, lamp, sg]
    if has_cache:
        past = cache[0].shape[3]
        in_specs += [pl.BlockSpec((1, 1, ATT_W, past), lambda b, i: (b, layer, 0, 0)),
                     pl.BlockSpec((1, 1, past * N_HEADS, D_V), lambda b, i: (b, layer, 0, 0))]
        args += list(cache)
    return pl.pallas_call(
        functools.partial(_attn_kernel, lam_init=lam_init, has_cache=has_cache),
        grid=(batch, nq),
        in_specs=in_specs,
        out_specs=pl.BlockSpec((tq, ATT_W), lambda b, i: (b * nq + i, 0)),
        out_shape=jax.ShapeDtypeStruct((n, ATT_W), BF16),
        compiler_params=pltpu.CompilerParams(dimension_semantics=("parallel", "parallel"),
                                             vmem_limit_bytes=VMEM_LIMIT),
        name="attn_cache" if has_cache else "attn",
    )(*args)


def _ssm_kernel(ut_ref, h0_ref, p1_ref, pb_ref, yt_ref, hf_ref, r_scr, hin_scr, **sizes):
    for gi in range(ut_ref.shape[0]):
        one = lambda ref, gi=gi: ref.at[pl.ds(gi, 1)]
        _ssm_group(one(ut_ref), one(h0_ref), one(p1_ref), one(pb_ref), one(yt_ref), one(hf_ref),
                   r_scr.at[gi], hin_scr.at[gi], **sizes)


def _ssm_group(ut_ref, h0_ref, p1_ref, pb_ref, yt_ref, hf_ref, r_scr, hin_scr, *,
               ctx_batch, ctx_chunks, lat_batch, lat_chunks):
    T = CHUNK
    W = SSM_CH * T
    lane = lax.broadcasted_iota(jnp.int32, (1, LANES), 1)
    lo = lane < SSM_P
    sgn_lr = jnp.where(lo, 1.0, -1.0).astype(F32)
    sgn_rl = -sgn_lr
    sidx = lax.broadcasted_iota(jnp.int32, (T, LANES), 0)
    srow = sidx.astype(F32)

    def swap(a):
        return pltpu.roll(a, SSM_P, 1)

    def tile_time(a):
        return jnp.concatenate([jnp.broadcast_to(a[i:i + 1, :], (SSM_CH, LANES)) for i in range(T)], axis=0)

    def rep_chan(a):
        return jnp.concatenate([a] * T, axis=0)

    def pair_pack(y):
        ys = swap(y)
        return jnp.where(lo, y, ys), jnp.where(lo, ys, y) * sgn_rl

    def cmul_const(x, yr2, yi2s):
        return x * yr2 + swap(x) * yi2s

    p1 = p1_ref[0]
    pb = pb_ref[0]

    def direction(d):
        lre2 = p1[3 * d + 0:3 * d + 1]
        lim2 = p1[3 * d + 1:3 * d + 2]
        ls2 = p1[3 * d + 2:3 * d + 3]
        bre2, bim2, cre2, cim2 = pb[4 * d + 0], pb[4 * d + 1], pb[4 * d + 2], pb[4 * d + 3]
        delta = jnp.exp(ls2)
        xr = lre2 * delta
        th = lim2 * delta

        mag1 = jnp.exp(xr)
        lbr2 = mag1 * jnp.cos(th)
        lbi2 = mag1 * jnp.sin(th)
        den = lre2 * lre2 + lim2 * lim2
        nr = lbr2 - 1.0
        cr2 = (nr * lre2 + lbi2 * lim2) / den
        ci2 = (lbi2 * lre2 - nr * lim2) / den
        bbr16 = cr2 * bre2 - ci2 * bim2
        bbi16s = (cr2 * bim2 + ci2 * bre2) * sgn_rl
        bbr2 = rep_chan(bbr16)
        bbi2s = rep_chan(bbi16s)
        ccr2 = rep_chan(cre2)
        cci2s = rep_chan(cim2 * sgn_rl)

        pr, pi = [lbr2], [lbi2]
        for _ in range(T.bit_length() - 1):
            r, i = pr[-1], pi[-1]
            pr.append(r * r - i * i)
            pi.append(2.0 * r * i)
        wpr = wrr = jnp.ones((T, LANES), F32)
        wpi = wri = jnp.zeros((T, LANES), F32)
        for j in range(T.bit_length() - 1):
            bit = ((sidx >> j) & 1) == 1
            r, i = pr[j], pi[j]
            wpr, wpi = jnp.where(bit, wpr * r - wpi * i, wpr), jnp.where(bit, wpr * i + wpi * r, wpi)
            wrr, wri = jnp.where(bit, wrr, wrr * r - wri * i), jnp.where(bit, wri, wrr * i + wri * r)
        wp = jnp.where(lo, wpr, wpi)
        wrev = jnp.where(lo, wrr, wri)

        def times(x, c1, c2s):
            return tile_time(x) * c1 + tile_time(swap(x)) * c2s

        lam_1 = (pr[0], pi[0] * sgn_rl)
        lam_t = (pr[-1], pi[-1] * sgn_rl)
        if d == 0:
            clag = times(wp, ccr2, cci2s)
            bs = times(wrev, bbr2, bbi2s)
            cc = times(cmul_const(wp, *lam_1), ccr2, cci2s)
        else:
            clag = times(wrev, ccr2, cci2s)
            bs = times(wp, bbr2, bbi2s)
            cc = times(cmul_const(wrev, *lam_1), ccr2, cci2s)
        kt = _nt(jnp.where(lo, bbr16, -bbi16s).astype(BF16), clag.astype(BF16))
        bbar = jnp.where(lo, bbr2, bbi2s)
        return kt, bs, cc * sgn_lr, lam_t, bbar

    kt_f, bsf, ccf, lamt_f, bbar_f = direction(0)
    kt_b, bsb, ccb, lamt_b, bbar_b = direction(1)

    sub = LANES // SSM_CH
    right_f = [kt_f if b == 0 else pltpu.roll(kt_f, b * SSM_CH, 1) for b in range(sub)]
    left_b = [kt_b if b == 0 else pltpu.roll(kt_b, W - b * SSM_CH, 1) for b in range(sub)]

    def ring(x, lanes):
        return x if lanes % W == 0 else jnp.concatenate([x[:, W - lanes % W:], x[:, :W - lanes % W]], axis=1)

    blocks = []
    for s in range(T):
        a, b = divmod(s, sub)
        a2, b2 = divmod(T - 1 - s, sub)
        fwd = ring(right_f[b], a * LANES)
        bwd = ring(left_b[b2], W - a2 * LANES)
        first, end = s * SSM_CH, (s + 1) * SSM_CH
        cols = []
        for v in range(W // LANES):
            lo_l, hi_l = v * LANES, (v + 1) * LANES
            fv, bv = fwd[:, lo_l:hi_l], bwd[:, lo_l:hi_l]
            parts = []
            if lo_l >= first:
                parts.append(fv)
            elif hi_l > first:
                parts.append(jnp.where(lane >= first - lo_l, fv, 0.0))
            if hi_l <= end:
                parts.append(bv)
            elif lo_l < end:
                parts.append(jnp.where(lane < end - lo_l, bv, 0.0))
            cols.append(parts[0] if len(parts) == 1 else parts[0] + parts[1])
        blocks.append(jnp.concatenate(cols, axis=1))
    a_t = jnp.concatenate(blocks, axis=0)
    w1 = jnp.concatenate([a_t, bsf, swap(bsf), bsb, swap(bsb)], axis=1).astype(BF16)
    r_scr[...] = _nn(ut_ref[0], w1)

    def chain(row0, nb, nchunks, col, lam_t, h, reverse):
        a1, a2s = lam_t
        hs = swap(h)
        order = range(nchunks - 1, -1, -1) if reverse else range(nchunks)
        hcol = slice(LANES, 2 * LANES) if reverse else slice(0, LANES)
        for c in order:
            rows = slice(row0 + c * nb, row0 + (c + 1) * nb)
            hin_scr[rows, hcol] = h
            s = r_scr[rows, col:col + LANES]
            ss = r_scr[rows, col + LANES:col + 2 * LANES]
            h, hs = h * a1 + hs * a2s + s, hs * a1 - h * a2s + ss
        return h

    zero = jnp.zeros((ctx_batch, LANES), F32)
    chain(0, ctx_batch, ctx_chunks, W, lamt_f, zero, False)
    chain(0, ctx_batch, ctx_chunks, W + 2 * LANES, lamt_b, zero, True)
    s_tiled = tile_time(srow)
    first_tok = jnp.where(s_tiled == 0.0, bbar_f, 0.0).astype(BF16)
    last_tok = jnp.where(s_tiled == float(T - 1), bbar_b, 0.0).astype(BF16)
    last_rows = (ctx_chunks - 1) * ctx_batch
    hf_ref[0, :, 0:LANES] = _nn(ut_ref[0, 0:ctx_batch, :], first_tok)
    hf_ref[0, :, LANES:2 * LANES] = _nn(ut_ref[0, last_rows:last_rows + ctx_batch, :], last_tok)
    lat0 = ctx_batch * ctx_chunks
    h0 = h0_ref[0]
    chain(lat0, lat_batch, lat_chunks, W, lamt_f, h0[:, 0:LANES], False)
    chain(lat0, lat_batch, lat_chunks, W + 2 * LANES, lamt_b, h0[:, LANES:2 * LANES], True)

    cc_cat = jnp.concatenate([ccf, ccb], axis=1).astype(BF16)
    yt_ref[0] = (r_scr[:, 0:W] + _nt(hin_scr[...].astype(BF16), cc_cat)).astype(yt_ref.dtype)


def _ssm(ut, h0, p1, pb, ctx_batch, ctx_chunks, lat_batch, lat_chunks):
    g, n, w = ut.shape
    gs = S5_GROUPS_PER_STEP
    return pl.pallas_call(
        functools.partial(_ssm_kernel, ctx_batch=ctx_batch, ctx_chunks=ctx_chunks,
                          lat_batch=lat_batch, lat_chunks=lat_chunks),
        grid=(g // gs,),
        in_specs=[
            pl.BlockSpec((gs, n, w), lambda i: (i, 0, 0)),
            pl.BlockSpec((gs, lat_batch, 2 * LANES), lambda i: (i, 0, 0)),
            pl.BlockSpec((gs, 8, LANES), lambda i: (i, 0, 0)),
            pl.BlockSpec((gs, 8, SSM_CH, LANES), lambda i: (i, 0, 0, 0)),
        ],
        out_specs=[
            pl.BlockSpec((gs, n, w), lambda i: (i, 0, 0)),
            pl.BlockSpec((gs, ctx_batch, 2 * LANES), lambda i: (i, 0, 0)),
        ],
        out_shape=[jax.ShapeDtypeStruct((g, n, w), BF16),
                   jax.ShapeDtypeStruct((g, ctx_batch, 2 * LANES), F32)],
        scratch_shapes=[pltpu.VMEM((gs, n, w + 4 * LANES), F32), pltpu.VMEM((gs, n, 2 * LANES), F32)],
        compiler_params=pltpu.CompilerParams(dimension_semantics=("parallel",),
                                             vmem_limit_bytes=VMEM_LIMIT),
        name="s5_scan",
    )(ut, h0, p1, pb)


def _tail_kernel(x_ref, oatt_ref, y_ref, u_ref, mod_ref, d_ref, wglu_ref, bglu_ref, wout_ref, g2_ref,
                 wup_ref, cw_ref, cb_ref, wd_ref, o_ref, x1_scr, h2_scr, *, seq_len):
    rows = x_ref.shape[0]
    tf = FFN_SLAB
    mod = mod_ref[0]
    gt1 = mod[2:3]
    sh2 = mod[3:4]
    sc2 = mod[4:5]
    gt2 = mod[5:6]
    for sb in range(rows // MIX_SUB):
        rs = slice(sb * MIX_SUB, (sb + 1) * MIX_SUB)
        y = y_ref[rs, :].astype(F32) + d_ref[...] * u_ref[rs, :]
        z = jax.nn.gelu(y)
        gate = _sigmoid(_nn(z.astype(BF16), wglu_ref[0]) + bglu_ref[...])
        o_ssm = (z * gate).astype(BF16)
        mix = _nn(oatt_ref[rs, :], wout_ref[0, 0:ATT_W, :]) + _nn(o_ssm, wout_ref[0, ATT_W:, :])
        x1 = x_ref[rs, :] + gt1 * mix
        x1_scr[rs, :] = x1
        ms = jnp.mean(x1 * x1, axis=-1, keepdims=True)
        h2 = x1 * lax.rsqrt(ms + EPS) * g2_ref[...]
        h2_scr[rs, :] = (h2 * (1.0 + sc2) + sh2).astype(BF16)

    row8 = lax.broadcasted_iota(jnp.int32, (8, 1), 0)
    h2 = h2_scr[...]

    def conv(up, cw, cb):
        prev = pltpu.roll(up, 1, 0)
        nxt = pltpu.roll(up, rows - 1, 0)
        pp, nn = [], []
        for r in range(0, rows, seq_len):
            e = r + seq_len
            pp += [jnp.where(row8 == 0, 0.0, prev[r:r + 8]), prev[r + 8:e]]
            nn += [nxt[r:e - 8], jnp.where(row8 == 7, 0.0, nxt[e - 8:e])]
        prev = jnp.concatenate(pp, axis=0)
        nxt = jnp.concatenate(nn, axis=0)
        return prev * cw[0:1] + up * cw[1:2] + nxt * cw[2:3] + cb

    acts = []
    for j in range(D_FF // tf):
        cv = slice(j * tf, (j + 1) * tf)
        cg = slice(D_FF + j * tf, D_FF + (j + 1) * tf)
        val = conv(_nn(h2, wup_ref[0, :, cv]), cw_ref[:, cv], cb_ref[:, cv])
        gate = conv(_nn(h2, wup_ref[0, :, cg]), cw_ref[:, cg], cb_ref[:, cg])
        acts.append((gate * _sigmoid(gate) * val).astype(BF16))
    act = jnp.concatenate(acts, axis=1)
    o_ref[...] = x1_scr[...] + gt2 * _nn(act, wd_ref[0])


def _tail(x, oatt, y, u, mods_l, d, wglu, bglu, wout, g2, wup, cw, cb, wdown, layer, seq_len, mod_base):
    n = x.shape[0]
    rows = FFN_ROWS
    per_seq = seq_len // rows if seq_len >= rows else 1
    if mod_base == 0:
        mod_map = lambda i: (0, 0, 0)
    else:
        mod_map = lambda i: (mod_base + i // per_seq, 0, 0)
    row = lambda i: (i, 0)
    const = lambda i: (0, 0)
    lay = lambda i: (layer, 0, 0)
    resident = dict(pipeline_mode=pl.Buffered(1))
    return pl.pallas_call(
        functools.partial(_tail_kernel, seq_len=seq_len),
        grid=(n // rows,),
        in_specs=[
            pl.BlockSpec((rows, D_MODEL), row),
            pl.BlockSpec((rows, ATT_W), row),
            pl.BlockSpec((rows, SSM_W), row),
            pl.BlockSpec((rows, SSM_W), row),
            pl.BlockSpec((1, N_MOD, D_MODEL), mod_map),
            pl.BlockSpec((1, SSM_W), const),
            pl.BlockSpec((1, SSM_W, SSM_W), lay, **resident),
            pl.BlockSpec((1, SSM_W), const),
            pl.BlockSpec((1, D_MODEL, D_MODEL), lay, **resident),
            pl.BlockSpec((1, D_MODEL), const),
            pl.BlockSpec((1, D_MODEL, 2 * D_FF), lay, **resident),
            pl.BlockSpec((3, 2 * D_FF), const),
            pl.BlockSpec((1, 2 * D_FF), const),
            pl.BlockSpec((1, D_FF, D_MODEL), lay, **resident),
        ],
        out_specs=pl.BlockSpec((rows, D_MODEL), row),
        out_shape=jax.ShapeDtypeStruct((n, D_MODEL), F32),
        scratch_shapes=[pltpu.VMEM((rows, D_MODEL), F32), pltpu.VMEM((rows, D_MODEL), BF16)],
        compiler_params=pltpu.CompilerParams(dimension_semantics=("parallel",),
                                             vmem_limit_bytes=VMEM_LIMIT),
        name="layer_tail",
    )(x, oatt, y, u, mods_l, d, wglu, bglu, wout, g2, wup, cw, cb, wdown)


def _rope_tables(n_tokens):
    pos = np.arange(n_tokens)
    row = (pos // GRID_W).astype(np.float64)
    col = (pos % GRID_W).astype(np.float64)
    nf = ROPE_AXIS // 2
    inv_freq = 1.0 / (ROPE_THETA ** (np.arange(nf, dtype=np.float64) / nf))
    lane = np.arange(LANES)
    freq = inv_freq[lane % nf]
    is_col = (lane % D_SUB) >= ROPE_AXIS
    ang = np.where(is_col[None, :], col[:, None], row[:, None]) * freq[None, :]
    second = (lane % ROPE_AXIS) >= nf
    cos = np.cos(ang)
    sin = np.sin(ang)
    sina = np.where(second[None, :], 0.0, -sin)
    sinb = np.where(second[None, :], sin, 0.0)
    return tuple(jnp.asarray(t, dtype=F32) for t in (cos, sina, sinb))


def _chunk_layout_passes(ctx_batch, ctx_len, lat_batch, lat_len):
    return ((0, ctx_batch, ctx_len, 0), (1, lat_batch, lat_len, ctx_batch * (ctx_len // CHUNK)))


def _row_permutation(batch):
    n = batch * CHUNK
    p = np.zeros((n, n), np.float32)
    b, s = np.meshgrid(np.arange(batch), np.arange(CHUNK), indexing="ij")
    p[(s * batch + b).ravel(), (b * CHUNK + s).ravel()] = 1.0
    return p


def _lane_permutation():
    n = GROUPS_PER_BLOCK * LANES
    q = np.zeros((n, n), np.float32)
    t, g, ch = np.meshgrid(np.arange(GROUPS_PER_BLOCK), np.arange(GROUPS_PER_BLOCK), np.arange(SSM_CH),
                           indexing="ij")
    q[(t * LANES + g * SSM_CH + ch).ravel(), (g * LANES + t * SSM_CH + ch).ravel()] = 1.0
    return q


def _to_chunks_kernel(uc_ref, ul_ref, pc_ref, pl_ref, q_ref, o_ref, v_scr, *, passes):
    for idx, batch, seq_len, row0 in passes:
        u_ref = (uc_ref, ul_ref)[idx]
        p_ref = (pc_ref, pl_ref)[idx]
        for c0 in range(0, seq_len // CHUNK, CHUNK_PAIR):
            x = jnp.concatenate(
                [jnp.concatenate([u_ref[b * seq_len + c * CHUNK:b * seq_len + (c + 1) * CHUNK, :]
                                  for b in range(batch)], axis=0) for c in range(c0, c0 + CHUNK_PAIR)], axis=1)
            r2 = _nn(p_ref[...], x.astype(BF16))
            for k in range(CHUNK_PAIR):
                c = c0 + k
                r = r2[:, k * LANES:(k + 1) * LANES]
                rows = slice(row0 + c * batch, row0 + (c + 1) * batch)
                for s in range(CHUNK):
                    j, t = divmod(s, GROUPS_PER_BLOCK)
                    v_scr[j, rows, t * LANES:(t + 1) * LANES] = r[s * batch:(s + 1) * batch]
    for j in range(TIME_BLOCKS):
        w = _nn(v_scr[j].astype(BF16), q_ref[...])
        for g in range(GROUPS_PER_BLOCK):
            o_ref[g, :, j * LANES:(j + 1) * LANES] = w[:, g * LANES:(g + 1) * LANES].astype(o_ref.dtype)


def _from_chunks_kernel(y_ref, pc_ref, pl_ref, q_ref, oc_ref, ol_ref, v_scr, *, passes):
    for j in range(TIME_BLOCKS):
        yj = jnp.concatenate([y_ref[g, :, j * LANES:(j + 1) * LANES] for g in range(GROUPS_PER_BLOCK)], axis=1)
        v_scr[j] = _nn(yj, q_ref[...])
    for idx, batch, seq_len, row0 in passes:
        o_ref = (oc_ref, ol_ref)[idx]
        p_ref = (pc_ref, pl_ref)[idx]
        for c0 in range(0, seq_len // CHUNK, CHUNK_PAIR):
            xs = []
            for c in range(c0, c0 + CHUNK_PAIR):
                rows = slice(row0 + c * batch, row0 + (c + 1) * batch)
                xs.append(jnp.concatenate(
                    [v_scr[s // GROUPS_PER_BLOCK, rows,
                           (s % GROUPS_PER_BLOCK) * LANES:(s % GROUPS_PER_BLOCK + 1) * LANES]
                     for s in range(CHUNK)], axis=0))
            r2 = _nn(p_ref[...], jnp.concatenate(xs, axis=1).astype(BF16))
            for k in range(CHUNK_PAIR):
                c = c0 + k
                for b in range(batch):
                    o_ref[b * seq_len + c * CHUNK:b * seq_len + (c + 1) * CHUNK, :] = (
                        r2[b * CHUNK:(b + 1) * CHUNK, k * LANES:(k + 1) * LANES].astype(o_ref.dtype))


def _to_chunks(uc, ul, ctx_batch, ctx_len, lat_batch, lat_len):
    passes = _chunk_layout_passes(ctx_batch, ctx_len, lat_batch, lat_len)
    rows = ctx_batch * (ctx_len // CHUNK) + lat_batch * (lat_len // CHUNK)
    pc = jnp.asarray(_row_permutation(ctx_batch), dtype=BF16)
    pn = jnp.asarray(_row_permutation(lat_batch), dtype=BF16)
    q = jnp.asarray(_lane_permutation(), dtype=BF16)
    const = lambda j: (0, 0)
    return pl.pallas_call(
        functools.partial(_to_chunks_kernel, passes=passes),
        grid=(SSM_GROUPS // GROUPS_PER_BLOCK,),
        in_specs=[pl.BlockSpec((uc.shape[0], LANES), lambda j: (0, j)),
                  pl.BlockSpec((ul.shape[0], LANES), lambda j: (0, j)),
                  pl.BlockSpec(pc.shape, const), pl.BlockSpec(pn.shape, const), pl.BlockSpec(q.shape, const)],
        out_specs=pl.BlockSpec((GROUPS_PER_BLOCK, rows, SSM_CH * CHUNK), lambda j: (j, 0, 0)),
        out_shape=jax.ShapeDtypeStruct((SSM_GROUPS, rows, SSM_CH * CHUNK), BF16),
        scratch_shapes=[pltpu.VMEM((TIME_BLOCKS, rows, GROUPS_PER_BLOCK * LANES), F32)],
        compiler_params=pltpu.CompilerParams(dimension_semantics=("parallel",),
                                             vmem_limit_bytes=VMEM_LIMIT),
        name="to_chunks",
    )(uc, ul, pc, pn, q)


def _from_chunks(yt, ctx_batch, ctx_len, lat_batch, lat_len):
    passes = _chunk_layout_passes(ctx_batch, ctx_len, lat_batch, lat_len)
    rows = yt.shape[1]
    n_ctx = ctx_batch * ctx_len
    n_lat = lat_batch * lat_len
    pc = jnp.asarray(_row_permutation(ctx_batch).T, dtype=BF16)
    pn = jnp.asarray(_row_permutation(lat_batch).T, dtype=BF16)
    q = jnp.asarray(_lane_permutation().T, dtype=BF16)
    const = lambda j: (0, 0)
    return pl.pallas_call(
        functools.partial(_from_chunks_kernel, passes=passes),
        grid=(SSM_GROUPS // GROUPS_PER_BLOCK,),
        in_specs=[pl.BlockSpec((GROUPS_PER_BLOCK, rows, SSM_CH * CHUNK), lambda j: (j, 0, 0)),
                  pl.BlockSpec(pc.shape, const), pl.BlockSpec(pn.shape, const), pl.BlockSpec(q.shape, const)],
        out_specs=[pl.BlockSpec((n_ctx, LANES), lambda j: (0, j)),
                   pl.BlockSpec((n_lat, LANES), lambda j: (0, j))],
        out_shape=[jax.ShapeDtypeStruct((n_ctx, SSM_W), BF16), jax.ShapeDtypeStruct((n_lat, SSM_W), BF16)],
        scratch_shapes=[pltpu.VMEM((TIME_BLOCKS, rows, GROUPS_PER_BLOCK * LANES), F32)],
        compiler_params=pltpu.CompilerParams(dimension_semantics=("parallel",),
                                             vmem_limit_bytes=VMEM_LIMIT),
        name="from_chunks",
    )(yt, pc, pn, q)


def _dup_lanes(a):
    return jnp.concatenate([a, a], axis=-1)


def kernel(x_prompt, x_sample, cache_k, cache_v, state_ssm, c, c_ctx, w_mod, b_mod, g_norm1, w_in, q_norm, k_norm, lambda_q1, lambda_k1, lambda_q2, lambda_k2, subln_g, ssm_lambda_re, ssm_lambda_im, ssm_log_step, ssm_b_re, ssm_b_im, ssm_c_re, ssm_c_im, ssm_d, w_glu, b_glu, w_out, g_norm2, w_up, conv_w, conv_b, w_down):
    nb_ctx, len_ctx, _ = x_prompt.shape
    nb_lat, len_lat, _ = x_sample.shape
    past = cache_k.shape[2]

    n_cond = 16
    cond = jnp.zeros((n_cond, D_MODEL), F32).at[0].set(c_ctx).at[1:1 + nb_lat].set(c)
    mods = _modulation(cond, w_mod, b_mod).reshape(DEPTH, n_cond, N_MOD, D_MODEL)

    gidx = np.arange(MXU_TILE) // D_SUB
    gmat = jnp.asarray(np.where(gidx[:, None] == gidx[None, :], 1.0 / D_SUB, 0.0), dtype=BF16)
    cache_kt = cache_k.transpose(0, 1, 3, 4, 5, 2).reshape(nb_lat, DEPTH, ATT_W, past)
    cache_vr = cache_v.reshape(nb_lat, DEPTH, past * N_HEADS, D_V)
    rope_tabs = _rope_tables(len_lat)
    win_b = w_in.astype(BF16)
    wout_b = w_out.astype(BF16)
    wglu_b = w_glu.astype(BF16)
    wup_b = w_up.astype(BF16)
    wdown_b = w_down.astype(BF16)

    xp = x_prompt.reshape(nb_ctx * len_ctx, D_MODEL)
    xs = x_sample.reshape(nb_lat * len_lat, D_MODEL)
    ss = []
    kv_out = None
    for l in range(DEPTH):
        lam_init = 0.8 - 0.6 * math.exp(-0.3 * l)
        mods_l = mods[l]
        g1 = g_norm1[l].reshape(1, D_MODEL)
        g2 = g_norm2[l].reshape(1, D_MODEL)
        qg = jnp.tile(q_norm[l], ATT_W // D_SUB).reshape(1, ATT_W)
        kg = jnp.tile(k_norm[l], ATT_W // D_SUB).reshape(1, ATT_W)
        sg = jnp.tile(subln_g[l], N_HEADS).reshape(1, ATT_W)
        lamp = jnp.stack([lambda_q1[l], lambda_k1[l], lambda_q2[l], lambda_k2[l]])
        d = ssm_d[l].reshape(1, SSM_W)
        bglu = b_glu[l].reshape(1, SSM_W)
        cb = conv_b[l].reshape(1, 2 * D_FF)
        cw = conv_w[l]

        zeros = jnp.zeros((SSM_GROUPS, SSM_P), F32)
        step = jnp.broadcast_to(ssm_log_step[l][:, :, None], (2, SSM_GROUPS, SSM_P))
        p1 = _dup_lanes(jnp.stack([ssm_lambda_re[l, 0], ssm_lambda_im[l, 0], step[0],
                                   ssm_lambda_re[l, 1], ssm_lambda_im[l, 1], step[1], zeros, zeros], axis=1))
        bt = lambda a: a.transpose(0, 2, 1)
        pb = _dup_lanes(jnp.stack([bt(ssm_b_re[l, 0]), bt(ssm_b_im[l, 0]), ssm_c_re[l, 0], ssm_c_im[l, 0],
                                   bt(ssm_b_re[l, 1]), bt(ssm_b_im[l, 1]), ssm_c_re[l, 1], ssm_c_im[l, 1]], axis=1))
        h0 = state_ssm[:, l].transpose(2, 0, 1, 4, 3).reshape(SSM_GROUPS, nb_lat, 4 * SSM_P)

        qc, kc_new, vc_new, uc, kt, vt = _premix(xp, mods_l, g1, win_b, l, gmat, qg, kg, None, len_ctx, 0,
                                                 carry=kv_out)
        ql, kl, vl, ul = _premix(xs, mods_l, g1, win_b, l, gmat, qg, kg, rope_tabs, len_lat, 1)

        oc = _attention(qc, kc_new, vc_new, lamp, sg, None, l, nb_ctx, len_ctx, lam_init)
        ol = _attention(ql, kl, vl, lamp, sg, (cache_kt, cache_vr), l, nb_lat, len_lat, lam_init)

        ut = _to_chunks(uc, ul, nb_ctx, len_ctx, nb_lat, len_lat)
        yt, hfin = _ssm(ut, h0, p1, pb, nb_ctx, len_ctx // CHUNK, nb_lat, len_lat // CHUNK)
        yc, yl = _from_chunks(yt, nb_ctx, len_ctx, nb_lat, len_lat)

        xp = _tail(xp, oc, yc, uc, mods_l, d, wglu_b, bglu, wout_b, g2, wup_b, cw, cb, wdown_b, l, len_ctx, 0)
        xs = _tail(xs, ol, yl, ul, mods_l, d, wglu_b, bglu, wout_b, g2, wup_b, cw, cb, wdown_b, l, len_lat, 1)

        kv_out = (kt, vt)
        ss.append(hfin.reshape(SSM_GROUPS, nb_ctx, 2, 2, SSM_P).transpose(1, 2, 0, 4, 3))

    new_k = kv_out[0].reshape(nb_ctx, DEPTH, N_HEADS, 2, D_SUB, len_ctx).transpose(0, 1, 5, 2, 3, 4)
    new_v = kv_out[1].reshape(nb_ctx, DEPTH, len_ctx, N_HEADS, D_V)
    return (xp.reshape(nb_ctx, len_ctx, D_MODEL), xs.reshape(nb_lat, len_lat, D_MODEL),
            new_k, new_v, jnp.stack(ss, axis=1))
```

```python
import functools
import math

import numpy as np
import jax
import jax.numpy as jnp
from jax import lax
from jax.experimental import pallas as pl
from jax.experimental.pallas import tpu as pltpu

F32 = jnp.float32
BF16 = jnp.bfloat16

D_MODEL = 1024
DEPTH = 2
GRID_W = 64
ATT_W = 512
SSM_W = 512
N_HEADS = 4
D_V = 128
D_SUB = 64
ROPE_AXIS = 32
ROPE_THETA = 10000.0
SSM_CH = 16
SSM_GROUPS = 32
SSM_P = 64
D_FF = 2048
N_MOD = 6
EPS = 1e-6

LANES = 128
MXU_TILE = 256
GROUPS_PER_BLOCK = LANES // SSM_CH
CHUNK = 32
S5_GROUPS_PER_STEP = 4
TIME_BLOCKS = CHUNK // GROUPS_PER_BLOCK
CHUNK_PAIR = MXU_TILE // LANES
ATTN_TILE = 1024
ATTN_SUB = 256
MIX_TILE = 512
MIX_SUB = 256
FFN_ROWS = 1024
FFN_SLAB = 512
MOD_COLS = 1536
VMEM_LIMIT = 56 * 1024 * 1024


def _nt(a, b):
    return lax.dot_general(a, b, (((1,), (1,)), ((), ())), preferred_element_type=F32)


def _nn(a, b):
    return jnp.dot(a, b, preferred_element_type=F32)


def _sigmoid(x):
    return 1.0 / (1.0 + jnp.exp(-x))


def _mod_kernel(c_ref, w_ref, b_ref, o_ref):
    c = c_ref[...]
    a = (c * _sigmoid(c)).astype(BF16)
    o_ref[0] = _nn(a, w_ref[0].astype(BF16)) + b_ref[0]


def _modulation(cond, w_mod, b_mod):
    nb = cond.shape[0]
    ncol = N_MOD * D_MODEL
    return pl.pallas_call(
        _mod_kernel,
        grid=(DEPTH, ncol // MOD_COLS),
        in_specs=[
            pl.BlockSpec((nb, D_MODEL), lambda l, j: (0, 0)),
            pl.BlockSpec((1, D_MODEL, MOD_COLS), lambda l, j: (l, 0, j)),
            pl.BlockSpec((1, 1, MOD_COLS), lambda l, j: (l, 0, j)),
        ],
        out_specs=pl.BlockSpec((1, nb, MOD_COLS), lambda l, j: (l, 0, j)),
        out_shape=jax.ShapeDtypeStruct((DEPTH, nb, ncol), F32),
        compiler_params=pltpu.CompilerParams(vmem_limit_bytes=VMEM_LIMIT),
        name="adaln_mod",
    )(cond, w_mod, b_mod.reshape(DEPTH, 1, ncol))


def _premix_kernel(x_ref, mod_ref, g1_ref, win_ref, gmat_ref, qg_ref, kg_ref, *rest, rope):
    kt_ref = vt_ref = None
    if rope:
        cos_ref, sina_ref, sinb_ref, q_ref, k_ref, v_ref, u_ref = rest
    else:
        q_ref, k_ref, v_ref, u_ref, kt_ref, vt_ref = rest[-6:]
    mod = mod_ref[0]
    sh1 = mod[0:1]
    sc1 = mod[1:2]

    def head_norm(t, g):
        sq = (t * t).astype(BF16)
        msq = jnp.concatenate([_nn(sq[:, c:c + MXU_TILE], gmat_ref[...])
                               for c in range(0, ATT_W, MXU_TILE)], axis=1)
        return t * lax.rsqrt(msq + EPS) * g

    def rotate(t, rs):
        outs = []
        for j in range(ATT_W // LANES):
            s = t[:, j * LANES:(j + 1) * LANES]
            outs.append(s * cos_ref[rs, :] + pltpu.roll(s, LANES - 16, 1) * sina_ref[rs, :]
                        + pltpu.roll(s, 16, 1) * sinb_ref[rs, :])
        return jnp.concatenate(outs, axis=1)

    for sb in range(x_ref.shape[0] // MIX_SUB):
        rs = slice(sb * MIX_SUB, (sb + 1) * MIX_SUB)
        x = x_ref[rs, :]
        ms = jnp.mean(x * x, axis=-1, keepdims=True)
        h = x * lax.rsqrt(ms + EPS) * g1_ref[...]
        h = h * (1.0 + sc1) + sh1
        qkvu = _nn(h.astype(BF16), win_ref[0])
        q = head_norm(qkvu[:, 0:ATT_W], qg_ref[...])
        k = head_norm(qkvu[:, ATT_W:2 * ATT_W], kg_ref[...])
        if rope:
            q = rotate(q, rs)
            k = rotate(k, rs)
        q_ref[rs, :] = (q * (D_SUB ** -0.5)).astype(q_ref.dtype)
        k_ref[rs, :] = k.astype(k_ref.dtype)
        v = qkvu[:, 2 * ATT_W:3 * ATT_W]
        if kt_ref is not None:
            kt = k.T
            for slot in range(kt_ref.shape[1]):
                kt_ref[sb, slot] = kt
                for hd in range(N_HEADS):
                    vt_ref[sb, slot, pl.ds(hd, MIX_SUB, stride=N_HEADS), :] = v[:, hd * D_V:(hd + 1) * D_V]
        v_ref[rs, :] = v.astype(v_ref.dtype)
        u_ref[rs, :] = qkvu[:, 3 * ATT_W:].astype(u_ref.dtype)


def _premix(x, mods_l, g1, win, layer, gmat, qg, kg, rope_tabs, seq_len, mod_base, carry=None):
    n = x.shape[0]
    tm = MIX_TILE
    per_seq = seq_len // tm
    rope = rope_tabs is not None
    assert rope or seq_len == MIX_SUB
    if mod_base == 0:
        mod_map = lambda i: (0, 0, 0)
    else:
        mod_map = lambda i: (mod_base + i // per_seq, 0, 0)
    row = lambda i: (i, 0)
    const = lambda i: (0, 0)
    in_specs = [
        pl.BlockSpec((tm, D_MODEL), row),
        pl.BlockSpec((1, N_MOD, D_MODEL), mod_map),
        pl.BlockSpec((1, D_MODEL), const),
        pl.BlockSpec((1, D_MODEL, 4 * ATT_W), lambda i: (layer, 0, 0)),
        pl.BlockSpec((MXU_TILE, MXU_TILE), const),
        pl.BlockSpec((1, ATT_W), const),
        pl.BlockSpec((1, ATT_W), const),
    ]
    args = [x, mods_l, g1, win, gmat, qg, kg]
    if rope:
        tab = lambda i: (i % per_seq, 0)
        in_specs += [pl.BlockSpec((tm, LANES), tab)] * 3
        args += list(rope_tabs)
    out = pl.BlockSpec((tm, ATT_W), row)
    out_specs = [out, out, out, out]
    out_shape = [jax.ShapeDtypeStruct((n, ATT_W), BF16),
                 jax.ShapeDtypeStruct((n, ATT_W), BF16),
                 jax.ShapeDtypeStruct((n, ATT_W), BF16),
                 jax.ShapeDtypeStruct((n, SSM_W), F32)]
    aliases = {}
    if not rope:
        nseq = tm // seq_len
        slots, first = (DEPTH, 0) if carry is None else (1, layer)
        out_specs.append(pl.BlockSpec((nseq, slots, ATT_W, seq_len), lambda i: (i, first, 0, 0)))
        out_shape.append(jax.ShapeDtypeStruct((n // seq_len, DEPTH, ATT_W, seq_len), F32))
        out_specs.append(pl.BlockSpec((nseq, slots, seq_len * N_HEADS, D_V), lambda i: (i, first, 0, 0)))
        out_shape.append(jax.ShapeDtypeStruct((n // seq_len, DEPTH, seq_len * N_HEADS, D_V), F32))
        if carry is not None:
            aliases = {len(args): 4, len(args) + 1: 5}
            in_specs += [pl.BlockSpec(memory_space=pl.ANY)] * 2
            args += list(carry)
    return pl.pallas_call(
        functools.partial(_premix_kernel, rope=rope),
        grid=(n // tm,),
        in_specs=in_specs,
        out_specs=out_specs,
        out_shape=out_shape,
        input_output_aliases=aliases,
        compiler_params=pltpu.CompilerParams(dimension_semantics=("parallel",),
                                             vmem_limit_bytes=VMEM_LIMIT),
        name="premix_rope" if rope else "premix",
    )(*args)


def _attn_kernel(q_ref, k_ref, v_ref, lamp_ref, sg_ref, *rest, lam_init, has_cache):
    if has_cache:
        kc_ref, vc_ref, o_ref = rest
    else:
        (o_ref,) = rest
    tq = q_ref.shape[0]
    lp = lamp_ref[...]
    l1 = jnp.sum(lp[0:1] * lp[1:2], axis=-1, keepdims=True)
    l2 = jnp.sum(lp[2:3] * lp[3:4], axis=-1, keepdims=True)
    lam = jnp.exp(l1) - jnp.exp(l2) + lam_init
    first = lax.broadcasted_iota(jnp.int32, (1, LANES), 1) < D_SUB
    tsub = min(ATTN_SUB, tq)
    for h in range(N_HEADS):
        sl = slice(h * D_V, (h + 1) * D_V)
        kn = k_ref[:, sl]
        vn = v_ref[:, sl]
        if has_cache:
            ncache = kc_ref.shape[3]
            kct = kc_ref[0, 0, sl, :].astype(BF16)
            vct = vc_ref[0, 0, pl.ds(h, ncache, stride=N_HEADS), :].astype(BF16)
        for r0 in range(0, tq, tsub):
            qh = q_ref[r0:r0 + tsub, sl]
            zero = jnp.zeros_like(qh)
            qq = jnp.concatenate([jnp.where(first, qh, zero), jnp.where(first, zero, qh)], axis=0)
            if has_cache:
                sc = jnp.concatenate([_nn(qq, kct), _nt(qq, kn)], axis=1)
            else:
                sc = _nt(qq, kn)
            m = jnp.max(sc, axis=-1, keepdims=True)
            e = jnp.exp(sc - m)
            ssum = jnp.sum(e, axis=-1, keepdims=True)
            eb = e.astype(BF16)
            if has_cache:
                pv = _nn(eb[:, :ncache], vct) + _nn(eb[:, ncache:], vn)
            else:
                pv = _nn(eb, vn)
            on = pv * (1.0 / ssum)
            o = on[:tsub] - lam * on[tsub:]
            ms = jnp.mean(o * o, axis=-1, keepdims=True)
            o_ref[r0:r0 + tsub, sl] = (o * lax.rsqrt(ms + EPS) * sg_ref[:, sl]
                                       * (1.0 - lam_init)).astype(o_ref.dtype)


def _attention(q, k, v, lamp, sg, cache, layer, batch, seq_len, lam_init):
    n = q.shape[0]
    tq = min(ATTN_TILE, seq_len)
    nq = seq_len // tq
    has_cache = cache is not None
    in_specs = [
        pl.BlockSpec((tq, ATT_W), lambda b, i: (b * nq + i, 0)),
        pl.BlockSpec((seq_len, ATT_W), lambda b, i: (b, 0)),
        pl.BlockSpec((seq_len, ATT_W), lambda b, i: (b, 0)),
        pl.BlockSpec((4, D_SUB), lambda b, i: (0, 0)),
        pl.BlockSpec((1, ATT_W), lambda b, i: (0, 0)),
    ]
    args = [q, k, v, lamp, sg]
    if has_cache:
        past = cache[0].shape[3]
        in_specs += [pl.BlockSpec((1, 1, ATT_W, past), lambda b, i: (b, layer, 0, 0)),
                     pl.BlockSpec((1, 1, past * N_HEADS, D_V), lambda b, i: (b, layer, 0, 0))]
        args += list(cache)
    return pl.pallas_call(
        functools.partial(_attn_kernel, lam_init=lam_init, has_cache=has_cache),
        grid=(batch, nq),
        in_specs=in_specs,
        out_specs=pl.BlockSpec((tq, ATT_W), lambda b, i: (b * nq + i, 0)),
        out_shape=jax.ShapeDtypeStruct((n, ATT_W), BF16),
        compiler_params=pltpu.CompilerParams(dimension_semantics=("parallel", "parallel"),
                                             vmem_limit_bytes=VMEM_LIMIT),
        name="attn_cache" if has_cache else "attn",
    )(*args)


def _ssm_kernel(ut_ref, h0_ref, p1_ref, pb_ref, yt_ref, hf_ref, r_scr, hin_scr, **sizes):
    for gi in range(ut_ref.shape[0]):
        one = lambda ref, gi=gi: ref.at[pl.ds(gi, 1)]
        _ssm_group(one(ut_ref), one(h0_ref), one(p1_ref), one(pb_ref), one(yt_ref), one(hf_ref),
                   r_scr.at[gi], hin_scr.at[gi], **sizes)


def _ssm_group(ut_ref, h0_ref, p1_ref, pb_ref, yt_ref, hf_ref, r_scr, hin_scr, *,
               ctx_batch, ctx_chunks, lat_batch, lat_chunks):
    T = CHUNK
    assert T & (T - 1) == 0
    W = SSM_CH * T
    lane = lax.broadcasted_iota(jnp.int32, (1, LANES), 1)
    lo = lane < SSM_P
    sgn_lr = jnp.where(lo, 1.0, -1.0).astype(F32)
    sgn_rl = -sgn_lr
    sidx = lax.broadcasted_iota(jnp.int32, (T, LANES), 0)
    srow = sidx.astype(F32)

    def swap(a):
        return pltpu.roll(a, SSM_P, 1)

    def tile_time(a):
        return jnp.concatenate([jnp.broadcast_to(a[i:i + 1, :], (SSM_CH, LANES)) for i in range(T)], axis=0)

    def rep_chan(a):
        return jnp.concatenate([a] * T, axis=0)

    def cmul_const(x, yr2, yi2s):
        return x * yr2 + swap(x) * yi2s

    p1 = p1_ref[0]
    pb = pb_ref[0]

    def direction(d):
        lre2 = p1[3 * d + 0:3 * d + 1]
        lim2 = p1[3 * d + 1:3 * d + 2]
        ls2 = p1[3 * d + 2:3 * d + 3]
        bre2, bim2, cre2, cim2 = pb[4 * d + 0], pb[4 * d + 1], pb[4 * d + 2], pb[4 * d + 3]
        delta = jnp.exp(ls2)
        xr = lre2 * delta
        th = lim2 * delta

        mag1 = jnp.exp(xr)
        lbr2 = mag1 * jnp.cos(th)
        lbi2 = mag1 * jnp.sin(th)
        den = lre2 * lre2 + lim2 * lim2
        nr = lbr2 - 1.0
        cr2 = (nr * lre2 + lbi2 * lim2) / den
        ci2 = (lbi2 * lre2 - nr * lim2) / den
        bbr16 = cr2 * bre2 - ci2 * bim2
        bbi16s = (cr2 * bim2 + ci2 * bre2) * sgn_rl
        bbr2 = rep_chan(bbr16)
        bbi2s = rep_chan(bbi16s)
        ccr2 = rep_chan(cre2)
        cci2s = rep_chan(cim2 * sgn_rl)

        pr, pi = [lbr2], [lbi2]
        for _ in range(T.bit_length() - 1):
            r, i = pr[-1], pi[-1]
            pr.append(r * r - i * i)
            pi.append(2.0 * r * i)
        wpr = wrr = jnp.ones((T, LANES), F32)
        wpi = wri = jnp.zeros((T, LANES), F32)
        for j in range(T.bit_length() - 1):
            bit = ((sidx >> j) & 1) == 1
            r, i = pr[j], pi[j]
            wpr, wpi = jnp.where(bit, wpr * r - wpi * i, wpr), jnp.where(bit, wpr * i + wpi * r, wpi)
            wrr, wri = jnp.where(bit, wrr, wrr * r - wri * i), jnp.where(bit, wri, wrr * i + wri * r)
        wp = jnp.where(lo, wpr, wpi)
        wrev = jnp.where(lo, wrr, wri)

        def times(x, c1, c2s):
            return tile_time(x) * c1 + tile_time(swap(x)) * c2s

        lam_1 = (pr[0], pi[0] * sgn_rl)
        lam_t = (pr[-1], pi[-1] * sgn_rl)
        if d == 0:
            clag = times(wp, ccr2, cci2s)
            bs = times(wrev, bbr2, bbi2s)
            cc = times(cmul_const(wp, *lam_1), ccr2, cci2s)
        else:
            clag = times(wrev, ccr2, cci2s)
            bs = times(wp, bbr2, bbi2s)
            cc = times(cmul_const(wrev, *lam_1), ccr2, cci2s)
        kt = _nt(jnp.where(lo, bbr16, -bbi16s).astype(BF16), clag.astype(BF16))
        bbar = jnp.where(lo, bbr2, bbi2s)
        return kt, bs, cc * sgn_lr, lam_t, bbar

    kt_f, bsf, ccf, lamt_f, bbar_f = direction(0)
    kt_b, bsb, ccb, lamt_b, bbar_b = direction(1)

    sub = LANES // SSM_CH
    right_f = [kt_f if b == 0 else pltpu.roll(kt_f, b * SSM_CH, 1) for b in range(sub)]
    left_b = [kt_b if b == 0 else pltpu.roll(kt_b, W - b * SSM_CH, 1) for b in range(sub)]

    def ring(x, lanes):
        return x if lanes % W == 0 else jnp.concatenate([x[:, W - lanes % W:], x[:, :W - lanes % W]], axis=1)

    blocks = []
    for s in range(T):
        a, b = divmod(s, sub)
        a2, b2 = divmod(T - 1 - s, sub)
        fwd = ring(right_f[b], a * LANES)
        bwd = ring(left_b[b2], W - a2 * LANES)
        first, end = s * SSM_CH, (s + 1) * SSM_CH
        cols = []
        for v in range(W // LANES):
            lo_l, hi_l = v * LANES, (v + 1) * LANES
            fv, bv = fwd[:, lo_l:hi_l], bwd[:, lo_l:hi_l]
            parts = []
            if lo_l >= first:
                parts.append(fv)
            elif hi_l > first:
                parts.append(jnp.where(lane >= first - lo_l, fv, 0.0))
            if hi_l <= end:
                parts.append(bv)
            elif lo_l < end:
                parts.append(jnp.where(lane < end - lo_l, bv, 0.0))
            cols.append(parts[0] if len(parts) == 1 else parts[0] + parts[1])
        blocks.append(jnp.concatenate(cols, axis=1))
    a_t = jnp.concatenate(blocks, axis=0)
    w1 = jnp.concatenate([a_t, bsf, swap(bsf), bsb, swap(bsb)], axis=1).astype(BF16)
    r_scr[...] = _nn(ut_ref[0], w1)

    def chain(row0, nb, nchunks, col, lam_t, h, reverse):
        a1, a2s = lam_t
        hs = swap(h)
        order = range(nchunks - 1, -1, -1) if reverse else range(nchunks)
        hcol = slice(LANES, 2 * LANES) if reverse else slice(0, LANES)
        for c in order:
            rows = slice(row0 + c * nb, row0 + (c + 1) * nb)
            hin_scr[rows, hcol] = h
            s = r_scr[rows, col:col + LANES]
            ss = r_scr[rows, col + LANES:col + 2 * LANES]
            h, hs = h * a1 + hs * a2s + s, hs * a1 - h * a2s + ss
        return h

    zero = jnp.zeros((ctx_batch, LANES), F32)
    chain(0, ctx_batch, ctx_chunks, W, lamt_f, zero, False)
    chain(0, ctx_batch, ctx_chunks, W + 2 * LANES, lamt_b, zero, True)
    s_tiled = tile_time(srow)
    first_tok = jnp.where(s_tiled == 0.0, bbar_f, 0.0).astype(BF16)
    last_tok = jnp.where(s_tiled == float(T - 1), bbar_b, 0.0).astype(BF16)
    last_rows = (ctx_chunks - 1) * ctx_batch
    hf_ref[0, :, 0:LANES] = _nn(ut_ref[0, 0:ctx_batch, :], first_tok)
    hf_ref[0, :, LANES:2 * LANES] = _nn(ut_ref[0, last_rows:last_rows + ctx_batch, :], last_tok)
    lat0 = ctx_batch * ctx_chunks
    h0 = h0_ref[0]
    chain(lat0, lat_batch, lat_chunks, W, lamt_f, h0[:, 0:LANES], False)
    chain(lat0, lat_batch, lat_chunks, W + 2 * LANES, lamt_b, h0[:, LANES:2 * LANES], True)

    cc_cat = jnp.concatenate([ccf, ccb], axis=1).astype(BF16)
    yt_ref[0] = (r_scr[:, 0:W] + _nt(hin_scr[...].astype(BF16), cc_cat)).astype(yt_ref.dtype)


def _ssm(ut, h0, p1, pb, ctx_batch, ctx_chunks, lat_batch, lat_chunks):
    g, n, w = ut.shape
    gs = S5_GROUPS_PER_STEP
    return pl.pallas_call(
        functools.partial(_ssm_kernel, ctx_batch=ctx_batch, ctx_chunks=ctx_chunks,
                          lat_batch=lat_batch, lat_chunks=lat_chunks),
        grid=(g // gs,),
        in_specs=[
            pl.BlockSpec((gs, n, w), lambda i: (i, 0, 0)),
            pl.BlockSpec((gs, lat_batch, 2 * LANES), lambda i: (i, 0, 0)),
            pl.BlockSpec((gs, 8, LANES), lambda i: (i, 0, 0)),
            pl.BlockSpec((gs, 8, SSM_CH, LANES), lambda i: (i, 0, 0, 0)),
        ],
        out_specs=[
            pl.BlockSpec((gs, n, w), lambda i: (i, 0, 0)),
            pl.BlockSpec((gs, ctx_batch, 2 * LANES), lambda i: (i, 0, 0)),
        ],
        out_shape=[jax.ShapeDtypeStruct((g, n, w), BF16),
                   jax.ShapeDtypeStruct((g, ctx_batch, 2 * LANES), F32)],
        scratch_shapes=[pltpu.VMEM((gs, n, w + 4 * LANES), F32), pltpu.VMEM((gs, n, 2 * LANES), F32)],
        compiler_params=pltpu.CompilerParams(dimension_semantics=("parallel",),
                                             vmem_limit_bytes=VMEM_LIMIT),
        name="s5_scan",
    )(ut, h0, p1, pb)


def _tail_kernel(x_ref, oatt_ref, y_ref, u_ref, mod_ref, d_ref, wglu_ref, bglu_ref, wout_ref, g2_ref,
                 wup_ref, cw_ref, cb_ref, wd_ref, o_ref, x1_scr, h2_scr, *, seq_len):
    rows = x_ref.shape[0]
    tf = FFN_SLAB
    mod = mod_ref[0]
    gt1 = mod[2:3]
    sh2 = mod[3:4]
    sc2 = mod[4:5]
    gt2 = mod[5:6]
    for sb in range(rows // MIX_SUB):
        rs = slice(sb * MIX_SUB, (sb + 1) * MIX_SUB)
        y = y_ref[rs, :].astype(F32) + d_ref[...] * u_ref[rs, :]
        z = jax.nn.gelu(y)
        gate = _sigmoid(_nn(z.astype(BF16), wglu_ref[0]) + bglu_ref[...])
        o_ssm = (z * gate).astype(BF16)
        mix = _nn(oatt_ref[rs, :], wout_ref[0, 0:ATT_W, :]) + _nn(o_ssm, wout_ref[0, ATT_W:, :])
        x1 = x_ref[rs, :] + gt1 * mix
        x1_scr[rs, :] = x1
        ms = jnp.mean(x1 * x1, axis=-1, keepdims=True)
        h2 = x1 * lax.rsqrt(ms + EPS) * g2_ref[...]
        h2_scr[rs, :] = (h2 * (1.0 + sc2) + sh2).astype(BF16)

    row8 = lax.broadcasted_iota(jnp.int32, (8, 1), 0)
    h2 = h2_scr[...]

    def conv(up, cw, cb):
        prev = pltpu.roll(up, 1, 0)
        nxt = pltpu.roll(up, rows - 1, 0)
        pp, nn = [], []
        for r in range(0, rows, seq_len):
            e = r + seq_len
            pp += [jnp.where(row8 == 0, 0.0, prev[r:r + 8]), prev[r + 8:e]]
            nn += [nxt[r:e - 8], jnp.where(row8 == 7, 0.0, nxt[e - 8:e])]
        prev = jnp.concatenate(pp, axis=0)
        nxt = jnp.concatenate(nn, axis=0)
        return prev * cw[0:1] + up * cw[1:2] + nxt * cw[2:3] + cb

    acts = []
    for j in range(D_FF // tf):
        cv = slice(j * tf, (j + 1) * tf)
        cg = slice(D_FF + j * tf, D_FF + (j + 1) * tf)
        val = conv(_nn(h2, wup_ref[0, :, cv]), cw_ref[:, cv], cb_ref[:, cv])
        gate = conv(_nn(h2, wup_ref[0, :, cg]), cw_ref[:, cg], cb_ref[:, cg])
        acts.append((gate * _sigmoid(gate) * val).astype(BF16))
    act = jnp.concatenate(acts, axis=1)
    o_ref[...] = x1_scr[...] + gt2 * _nn(act, wd_ref[0])


def _tail(x, oatt, y, u, mods_l, d, wglu, bglu, wout, g2, wup, cw, cb, wdown, layer, seq_len, mod_base):
    n = x.shape[0]
    rows = FFN_ROWS
    per_seq = seq_len // rows if seq_len >= rows else 1
    if mod_base == 0:
        mod_map = lambda i: (0, 0, 0)
    else:
        mod_map = lambda i: (mod_base + i // per_seq, 0, 0)
    row = lambda i: (i, 0)
    const = lambda i: (0, 0)
    lay = lambda i: (layer, 0, 0)
    resident = dict(pipeline_mode=pl.Buffered(1))
    return pl.pallas_call(
        functools.partial(_tail_kernel, seq_len=seq_len),
        grid=(n // rows,),
        in_specs=[
            pl.BlockSpec((rows, D_MODEL), row),
            pl.BlockSpec((rows, ATT_W), row),
            pl.BlockSpec((rows, SSM_W), row),
            pl.BlockSpec((rows, SSM_W), row),
            pl.BlockSpec((1, N_MOD, D_MODEL), mod_map),
            pl.BlockSpec((1, SSM_W), const),
            pl.BlockSpec((1, SSM_W, SSM_W), lay, **resident),
            pl.BlockSpec((1, SSM_W), const),
            pl.BlockSpec((1, D_MODEL, D_MODEL), lay, **resident),
            pl.BlockSpec((1, D_MODEL), const),
            pl.BlockSpec((1, D_MODEL, 2 * D_FF), lay, **resident),
            pl.BlockSpec((3, 2 * D_FF), const),
            pl.BlockSpec((1, 2 * D_FF), const),
            pl.BlockSpec((1, D_FF, D_MODEL), lay, **resident),
        ],
        out_specs=pl.BlockSpec((rows, D_MODEL), row),
        out_shape=jax.ShapeDtypeStruct((n, D_MODEL), F32),
        scratch_shapes=[pltpu.VMEM((rows, D_MODEL), F32), pltpu.VMEM((rows, D_MODEL), BF16)],
        compiler_params=pltpu.CompilerParams(dimension_semantics=("parallel",),
                                             vmem_limit_bytes=VMEM_LIMIT),
        name="layer_tail",
    )(x, oatt, y, u, mods_l, d, wglu, bglu, wout, g2, wup, cw, cb, wdown)


def _rope_tables(n_tokens):
    pos = np.arange(n_tokens)
    row = (pos // GRID_W).astype(np.float64)
    col = (pos % GRID_W).astype(np.float64)
    nf = ROPE_AXIS // 2
    inv_freq = 1.0 / (ROPE_THETA ** (np.arange(nf, dtype=np.float64) / nf))
    lane = np.arange(LANES)
    freq = inv_freq[lane % nf]
    is_col = (lane % D_SUB) >= ROPE_AXIS
    ang = np.where(is_col[None, :], col[:, None], row[:, None]) * freq[None, :]
    second = (lane % ROPE_AXIS) >= nf
    cos = np.cos(ang)
    sin = np.sin(ang)
    sina = np.where(second[None, :], 0.0, -sin)
    sinb = np.where(second[None, :], sin, 0.0)
    return tuple(jnp.asarray(t, dtype=F32) for t in (cos, sina, sinb))


def _chunk_layout_passes(ctx_batch, ctx_len, lat_batch, lat_len):
    return ((0, ctx_batch, ctx_len, 0), (1, lat_batch, lat_len, ctx_batch * (ctx_len // CHUNK)))


def _row_permutation(batch):
    n = batch * CHUNK
    p = np.zeros((n, n), np.float32)
    b, s = np.meshgrid(np.arange(batch), np.arange(CHUNK), indexing="ij")
    p[(s * batch + b).ravel(), (b * CHUNK + s).ravel()] = 1.0
    return p


def _lane_permutation():
    n = GROUPS_PER_BLOCK * LANES
    q = np.zeros((n, n), np.float32)
    t, g, ch = np.meshgrid(np.arange(GROUPS_PER_BLOCK), np.arange(GROUPS_PER_BLOCK), np.arange(SSM_CH),
                           indexing="ij")
    q[(t * LANES + g * SSM_CH + ch).ravel(), (g * LANES + t * SSM_CH + ch).ravel()] = 1.0
    return q


def _to_chunks_kernel(uc_ref, ul_ref, pc_ref, pl_ref, q_ref, o_ref, v_scr, *, passes):
    for idx, batch, seq_len, row0 in passes:
        u_ref = (uc_ref, ul_ref)[idx]
        p_ref = (pc_ref, pl_ref)[idx]
        for c0 in range(0, seq_len // CHUNK, CHUNK_PAIR):
            x = jnp.concatenate(
                [jnp.concatenate([u_ref[b * seq_len + c * CHUNK:b * seq_len + (c + 1) * CHUNK, :]
                                  for b in range(batch)], axis=0) for c in range(c0, c0 + CHUNK_PAIR)], axis=1)
            r2 = _nn(p_ref[...], x.astype(BF16))
            for k in range(CHUNK_PAIR):
                c = c0 + k
                r = r2[:, k * LANES:(k + 1) * LANES]
                rows = slice(row0 + c * batch, row0 + (c + 1) * batch)
                for s in range(CHUNK):
                    j, t = divmod(s, GROUPS_PER_BLOCK)
                    v_scr[j, rows, t * LANES:(t + 1) * LANES] = r[s * batch:(s + 1) * batch]
    for j in range(TIME_BLOCKS):
        w = _nn(v_scr[j].astype(BF16), q_ref[...])
        for g in range(GROUPS_PER_BLOCK):
            o_ref[g, :, j * LANES:(j + 1) * LANES] = w[:, g * LANES:(g + 1) * LANES].astype(o_ref.dtype)


def _from_chunks_kernel(y_ref, pc_ref, pl_ref, q_ref, oc_ref, ol_ref, v_scr, *, passes):
    for j in range(TIME_BLOCKS):
        yj = jnp.concatenate([y_ref[g, :, j * LANES:(j + 1) * LANES] for g in range(GROUPS_PER_BLOCK)], axis=1)
        v_scr[j] = _nn(yj, q_ref[...])
    for idx, batch, seq_len, row0 in passes:
        o_ref = (oc_ref, ol_ref)[idx]
        p_ref = (pc_ref, pl_ref)[idx]
        for c0 in range(0, seq_len // CHUNK, CHUNK_PAIR):
            xs = []
            for c in range(c0, c0 + CHUNK_PAIR):
                rows = slice(row0 + c * batch, row0 + (c + 1) * batch)
                xs.append(jnp.concatenate(
                    [v_scr[s // GROUPS_PER_BLOCK, rows,
                           (s % GROUPS_PER_BLOCK) * LANES:(s % GROUPS_PER_BLOCK + 1) * LANES]
                     for s in range(CHUNK)], axis=0))
            r2 = _nn(p_ref[...], jnp.concatenate(xs, axis=1).astype(BF16))
            for k in range(CHUNK_PAIR):
                c = c0 + k
                for b in range(batch):
                    o_ref[b * seq_len + c * CHUNK:b * seq_len + (c + 1) * CHUNK, :] = (
                        r2[b * CHUNK:(b + 1) * CHUNK, k * LANES:(k + 1) * LANES].astype(o_ref.dtype))


def _to_chunks(uc, ul, ctx_batch, ctx_len, lat_batch, lat_len):
    passes = _chunk_layout_passes(ctx_batch, ctx_len, lat_batch, lat_len)
    rows = ctx_batch * (ctx_len // CHUNK) + lat_batch * (lat_len // CHUNK)
    pc = jnp.asarray(_row_permutation(ctx_batch), dtype=BF16)
    pn = jnp.asarray(_row_permutation(lat_batch), dtype=BF16)
    q = jnp.asarray(_lane_permutation(), dtype=BF16)
    const = lambda j: (0, 0)
    return pl.pallas_call(
        functools.partial(_to_chunks_kernel, passes=passes),
        grid=(SSM_GROUPS // GROUPS_PER_BLOCK,),
        in_specs=[pl.BlockSpec((uc.shape[0], LANES), lambda j: (0, j)),
                  pl.BlockSpec((ul.shape[0], LANES), lambda j: (0, j)),
                  pl.BlockSpec(pc.shape, const), pl.BlockSpec(pn.shape, const), pl.BlockSpec(q.shape, const)],
        out_specs=pl.BlockSpec((GROUPS_PER_BLOCK, rows, SSM_CH * CHUNK), lambda j: (j, 0, 0)),
        out_shape=jax.ShapeDtypeStruct((SSM_GROUPS, rows, SSM_CH * CHUNK), BF16),
        scratch_shapes=[pltpu.VMEM((TIME_BLOCKS, rows, GROUPS_PER_BLOCK * LANES), F32)],
        compiler_params=pltpu.CompilerParams(dimension_semantics=("parallel",),
                                             vmem_limit_bytes=VMEM_LIMIT),
        name="to_chunks",
    )(uc, ul, pc, pn, q)


def _from_chunks(yt, ctx_batch, ctx_len, lat_batch, lat_len):
    passes = _chunk_layout_passes(ctx_batch, ctx_len, lat_batch, lat_len)
    rows = yt.shape[1]
    n_ctx = ctx_batch * ctx_len
    n_lat = lat_batch * lat_len
    pc = jnp.asarray(_row_permutation(ctx_batch).T, dtype=BF16)
    pn = jnp.asarray(_row_permutation(lat_batch).T, dtype=BF16)
    q = jnp.asarray(_lane_permutation().T, dtype=BF16)
    const = lambda j: (0, 0)
    return pl.pallas_call(
        functools.partial(_from_chunks_kernel, passes=passes),
        grid=(SSM_GROUPS // GROUPS_PER_BLOCK,),
        in_specs=[pl.BlockSpec((GROUPS_PER_BLOCK, rows, SSM_CH * CHUNK), lambda j: (j, 0, 0)),
                  pl.BlockSpec(pc.shape, const), pl.BlockSpec(pn.shape, const), pl.BlockSpec(q.shape, const)],
        out_specs=[pl.BlockSpec((n_ctx, LANES), lambda j: (0, j)),
                   pl.BlockSpec((n_lat, LANES), lambda j: (0, j))],
        out_shape=[jax.ShapeDtypeStruct((n_ctx, SSM_W), BF16), jax.ShapeDtypeStruct((n_lat, SSM_W), BF16)],
        scratch_shapes=[pltpu.VMEM((TIME_BLOCKS, rows, GROUPS_PER_BLOCK * LANES), F32)],
        compiler_params=pltpu.CompilerParams(dimension_semantics=("parallel",),
                                             vmem_limit_bytes=VMEM_LIMIT),
        name="from_chunks",
    )(yt, pc, pn, q)


def _dup_lanes(a):
    return jnp.concatenate([a, a], axis=-1)


def kernel(x_prompt, x_sample, cache_k, cache_v, state_ssm, c, c_ctx, w_mod, b_mod, g_norm1, w_in, q_norm, k_norm, lambda_q1, lambda_k1, lambda_q2, lambda_k2, subln_g, ssm_lambda_re, ssm_lambda_im, ssm_log_step, ssm_b_re, ssm_b_im, ssm_c_re, ssm_c_im, ssm_d, w_glu, b_glu, w_out, g_norm2, w_up, conv_w, conv_b, w_down):
    nb_ctx, len_ctx, _ = x_prompt.shape
    nb_lat, len_lat, _ = x_sample.shape
    past = cache_k.shape[2]

    n_cond = 16
    cond = jnp.zeros((n_cond, D_MODEL), F32).at[0].set(c_ctx).at[1:1 + nb_lat].set(c)
    mods = _modulation(cond, w_mod, b_mod).reshape(DEPTH, n_cond, N_MOD, D_MODEL)

    gidx = np.arange(MXU_TILE) // D_SUB
    gmat = jnp.asarray(np.where(gidx[:, None] == gidx[None, :], 1.0 / D_SUB, 0.0), dtype=BF16)
    cache_kt = cache_k.transpose(0, 1, 3, 4, 5, 2).reshape(nb_lat, DEPTH, ATT_W, past)
    cache_vr = cache_v.reshape(nb_lat, DEPTH, past * N_HEADS, D_V)
    rope_tabs = _rope_tables(len_lat)
    win_b = w_in.astype(BF16)
    wout_b = w_out.astype(BF16)
    wglu_b = w_glu.astype(BF16)
    wup_b = w_up.astype(BF16)
    wdown_b = w_down.astype(BF16)

    xp = x_prompt.reshape(nb_ctx * len_ctx, D_MODEL)
    xs = x_sample.reshape(nb_lat * len_lat, D_MODEL)
    ss = []
    kv_out = None
    for l in range(DEPTH):
        lam_init = 0.8 - 0.6 * math.exp(-0.3 * l)
        mods_l = mods[l]
        g1 = g_norm1[l].reshape(1, D_MODEL)
        g2 = g_norm2[l].reshape(1, D_MODEL)
        qg = jnp.tile(q_norm[l], ATT_W // D_SUB).reshape(1, ATT_W)
        kg = jnp.tile(k_norm[l], ATT_W // D_SUB).reshape(1, ATT_W)
        sg = jnp.tile(subln_g[l], N_HEADS).reshape(1, ATT_W)
        lamp = jnp.stack([lambda_q1[l], lambda_k1[l], lambda_q2[l], lambda_k2[l]])
        d = ssm_d[l].reshape(1, SSM_W)
        bglu = b_glu[l].reshape(1, SSM_W)
        cb = conv_b[l].reshape(1, 2 * D_FF)
        cw = conv_w[l]

        zeros = jnp.zeros((SSM_GROUPS, SSM_P), F32)
        step = jnp.broadcast_to(ssm_log_step[l][:, :, None], (2, SSM_GROUPS, SSM_P))
        p1 = _dup_lanes(jnp.stack([ssm_lambda_re[l, 0], ssm_lambda_im[l, 0], step[0],
                                   ssm_lambda_re[l, 1], ssm_lambda_im[l, 1], step[1], zeros, zeros], axis=1))
        bt = lambda a: a.transpose(0, 2, 1)
        pb = _dup_lanes(jnp.stack([bt(ssm_b_re[l, 0]), bt(ssm_b_im[l, 0]), ssm_c_re[l, 0], ssm_c_im[l, 0],
                                   bt(ssm_b_re[l, 1]), bt(ssm_b_im[l, 1]), ssm_c_re[l, 1], ssm_c_im[l, 1]], axis=1))
        h0 = state_ssm[:, l].transpose(2, 0, 1, 4, 3).reshape(SSM_GROUPS, nb_lat, 4 * SSM_P)

        qc, kc_new, vc_new, uc, kt, vt = _premix(xp, mods_l, g1, win_b, l, gmat, qg, kg, None, len_ctx, 0,
                                                 carry=kv_out)
        ql, kl, vl, ul = _premix(xs, mods_l, g1, win_b, l, gmat, qg, kg, rope_tabs, len_lat, 1)

        oc = _attention(qc, kc_new, vc_new, lamp, sg, None, l, nb_ctx, len_ctx, lam_init)
        ol = _attention(ql, kl, vl, lamp, sg, (cache_kt, cache_vr), l, nb_lat, len_lat, lam_init)

        ut = _to_chunks(uc, ul, nb_ctx, len_ctx, nb_lat, len_lat)
        yt, hfin = _ssm(ut, h0, p1, pb, nb_ctx, len_ctx // CHUNK, nb_lat, len_lat // CHUNK)
        yc, yl = _from_chunks(yt, nb_ctx, len_ctx, nb_lat, len_lat)

        xp = _tail(xp, oc, yc, uc, mods_l, d, wglu_b, bglu, wout_b, g2, wup_b, cw, cb, wdown_b, l, len_ctx, 0)
        xs = _tail(xs, ol, yl, ul, mods_l, d, wglu_b, bglu, wout_b, g2, wup_b, cw, cb, wdown_b, l, len_lat, 1)

        kv_out = (kt, vt)
        ss.append(hfin.reshape(SSM_GROUPS, nb_ctx, 2, 2, SSM_P).transpose(1, 2, 0, 4, 3))

    new_k = kv_out[0].reshape(nb_ctx, DEPTH, N_HEADS, 2, D_SUB, len_ctx).transpose(0, 1, 5, 2, 3, 4)
    new_v = kv_out[1].reshape(nb_ctx, DEPTH, len_ctx, N_HEADS, D_V)
    return (xp.reshape(nb_ctx, len_ctx, D_MODEL), xs.reshape(nb_lat, len_lat, D_MODEL),
            new_k, new_v, jnp.stack(ss, axis=1))
```

```python
import functools
import math

import numpy as np
import jax
import jax.numpy as jnp
from jax import lax
from jax.experimental import pallas as pl
from jax.experimental.pallas import tpu as pltpu

F32 = jnp.float32
BF16 = jnp.bfloat16

D_MODEL = 1024
DEPTH = 2
GRID_W = 64
ATT_W = 512
SSM_W = 512
N_HEADS = 4
D_V = 128
D_SUB = 64
ROPE_AXIS = 32
ROPE_THETA = 10000.0
SSM_CH = 16
SSM_GROUPS = 32
SSM_P = 64
D_FF = 2048
N_MOD = 6
EPS = 1e-6

LANES = 128
MXU_TILE = 256
GROUPS_PER_BLOCK = LANES // SSM_CH
CHUNK = 32
S5_GROUPS_PER_STEP = 4
TIME_BLOCKS = CHUNK // GROUPS_PER_BLOCK
CHUNK_PAIR = MXU_TILE // LANES
ATTN_TILE = 1024
ATTN_SUB = 256
MIX_TILE = 1024
MIX_SUB = 256
FFN_ROWS = 1024
FFN_SLAB = 512
MOD_COLS = 1536
VMEM_LIMIT = 56 * 1024 * 1024


def _nt(a, b):
    return lax.dot_general(a, b, (((1,), (1,)), ((), ())), preferred_element_type=F32)


def _nn(a, b):
    return jnp.dot(a, b, preferred_element_type=F32)


def _sigmoid(x):
    return 1.0 / (1.0 + jnp.exp(-x))


def _mod_kernel(c_ref, w_ref, b_ref, o_ref):
    c = c_ref[...]
    a = (c * _sigmoid(c)).astype(BF16)
    o_ref[0] = _nn(a, w_ref[0].astype(BF16)) + b_ref[0]


def _modulation(cond, w_mod, b_mod):
    nb = cond.shape[0]
    ncol = N_MOD * D_MODEL
    return pl.pallas_call(
        _mod_kernel,
        grid=(DEPTH, ncol // MOD_COLS),
        in_specs=[
            pl.BlockSpec((nb, D_MODEL), lambda l, j: (0, 0)),
            pl.BlockSpec((1, D_MODEL, MOD_COLS), lambda l, j: (l, 0, j)),
            pl.BlockSpec((1, 1, MOD_COLS), lambda l, j: (l, 0, j)),
        ],
        out_specs=pl.BlockSpec((1, nb, MOD_COLS), lambda l, j: (l, 0, j)),
        out_shape=jax.ShapeDtypeStruct((DEPTH, nb, ncol), F32),
        compiler_params=pltpu.CompilerParams(vmem_limit_bytes=VMEM_LIMIT),
        name="adaln_mod",
    )(cond, w_mod, b_mod.reshape(DEPTH, 1, ncol))


def _premix_kernel(x_ref, mod_ref, g1_ref, win_ref, gmat_ref, qg_ref, kg_ref, *rest, rope):
    kt_ref = vt_ref = None
    if rope:
        cos_ref, sina_ref, sinb_ref, q_ref, k_ref, v_ref, u_ref = rest
    else:
        q_ref, k_ref, v_ref, u_ref, kt_ref, vt_ref = rest[-6:]
    mod = mod_ref[0]
    sh1 = mod[0:1]
    sc1 = mod[1:2]

    def head_norm(t, g):
        sq = (t * t).astype(BF16)
        msq = jnp.concatenate([_nn(sq[:, c:c + MXU_TILE], gmat_ref[...])
                               for c in range(0, ATT_W, MXU_TILE)], axis=1)
        return t * lax.rsqrt(msq + EPS) * g

    def rotate(t, rs):
        outs = []
        for j in range(ATT_W // LANES):
            s = t[:, j * LANES:(j + 1) * LANES]
            outs.append(s * cos_ref[rs, :] + pltpu.roll(s, LANES - 16, 1) * sina_ref[rs, :]
                        + pltpu.roll(s, 16, 1) * sinb_ref[rs, :])
        return jnp.concatenate(outs, axis=1)

    for sb in range(x_ref.shape[0] // MIX_SUB):
        rs = slice(sb * MIX_SUB, (sb + 1) * MIX_SUB)
        x = x_ref[rs, :]
        ms = jnp.mean(x * x, axis=-1, keepdims=True)
        h = x * lax.rsqrt(ms + EPS) * g1_ref[...]
        h = h * (1.0 + sc1) + sh1
        qkvu = _nn(h.astype(BF16), win_ref[0])
        q = head_norm(qkvu[:, 0:ATT_W], qg_ref[...])
        k = head_norm(qkvu[:, ATT_W:2 * ATT_W], kg_ref[...])
        if rope:
            q = rotate(q, rs)
            k = rotate(k, rs)
        q_ref[rs, :] = (q * (D_SUB ** -0.5)).astype(q_ref.dtype)
        k_ref[rs, :] = k.astype(k_ref.dtype)
        v = qkvu[:, 2 * ATT_W:3 * ATT_W]
        if kt_ref is not None:
            kt = k.T
            for slot in range(kt_ref.shape[1]):
                kt_ref[sb, slot] = kt
                for hd in range(N_HEADS):
                    vt_ref[sb, slot, pl.ds(hd, MIX_SUB, stride=N_HEADS), :] = v[:, hd * D_V:(hd + 1) * D_V]
        v_ref[rs, :] = v.astype(v_ref.dtype)
        u_ref[rs, :] = qkvu[:, 3 * ATT_W:].astype(u_ref.dtype)


def _premix(x, mods_l, g1, win, layer, gmat, qg, kg, rope_tabs, seq_len, mod_base, carry=None):
    n = x.shape[0]
    tm = MIX_TILE
    per_seq = seq_len // tm
    rope = rope_tabs is not None
    assert rope or seq_len == MIX_SUB
    if mod_base == 0:
        mod_map = lambda i: (0, 0, 0)
    else:
        mod_map = lambda i: (mod_base + i // per_seq, 0, 0)
    row = lambda i: (i, 0)
    const = lambda i: (0, 0)
    in_specs = [
        pl.BlockSpec((tm, D_MODEL), row),
        pl.BlockSpec((1, N_MOD, D_MODEL), mod_map),
        pl.BlockSpec((1, D_MODEL), const),
        pl.BlockSpec((1, D_MODEL, 4 * ATT_W), lambda i: (layer, 0, 0)),
        pl.BlockSpec((MXU_TILE, MXU_TILE), const),
        pl.BlockSpec((1, ATT_W), const),
        pl.BlockSpec((1, ATT_W), const),
    ]
    args = [x, mods_l, g1, win, gmat, qg, kg]
    if rope:
        tab = lambda i: (i % per_seq, 0)
        in_specs += [pl.BlockSpec((tm, LANES), tab)] * 3
        args += list(rope_tabs)
    out = pl.BlockSpec((tm, ATT_W), row)
    out_specs = [out, out, out, out]
    out_shape = [jax.ShapeDtypeStruct((n, ATT_W), BF16),
                 jax.ShapeDtypeStruct((n, ATT_W), BF16),
                 jax.ShapeDtypeStruct((n, ATT_W), BF16),
                 jax.ShapeDtypeStruct((n, SSM_W), F32)]
    aliases = {}
    if not rope:
        nseq = tm // seq_len
        slots, first = (DEPTH, 0) if carry is None else (1, layer)
        out_specs.append(pl.BlockSpec((nseq, slots, ATT_W, seq_len), lambda i: (i, first, 0, 0)))
        out_shape.append(jax.ShapeDtypeStruct((n // seq_len, DEPTH, ATT_W, seq_len), F32))
        out_specs.append(pl.BlockSpec((nseq, slots, seq_len * N_HEADS, D_V), lambda i: (i, first, 0, 0)))
        out_shape.append(jax.ShapeDtypeStruct((n // seq_len, DEPTH, seq_len * N_HEADS, D_V), F32))
        if carry is not None:
            aliases = {len(args): 4, len(args) + 1: 5}
            in_specs += [pl.BlockSpec(memory_space=pl.ANY)] * 2
            args += list(carry)
    return pl.pallas_call(
        functools.partial(_premix_kernel, rope=rope),
        grid=(n // tm,),
        in_specs=in_specs,
        out_specs=out_specs,
        out_shape=out_shape,
        input_output_aliases=aliases,
        compiler_params=pltpu.CompilerParams(dimension_semantics=("parallel",),
                                             vmem_limit_bytes=VMEM_LIMIT),
        name="premix_rope" if rope else "premix",
    )(*args)


def _attn_kernel(q_ref, k_ref, v_ref, lamp_ref, sg_ref, *rest, lam_init, has_cache):
    if has_cache:
        kc_ref, vc_ref, o_ref = rest
    else:
        (o_ref,) = rest
    tq = q_ref.shape[0]
    lp = lamp_ref[...]
    l1 = jnp.sum(lp[0:1] * lp[1:2], axis=-1, keepdims=True)
    l2 = jnp.sum(lp[2:3] * lp[3:4], axis=-1, keepdims=True)
    lam = jnp.exp(l1) - jnp.exp(l2) + lam_init
    first = lax.broadcasted_iota(jnp.int32, (1, LANES), 1) < D_SUB
    tsub = min(ATTN_SUB, tq)
    for h in range(N_HEADS):
        sl = slice(h * D_V, (h + 1) * D_V)
        kn = k_ref[:, sl]
        vn = v_ref[:, sl]
        if has_cache:
            ncache = kc_ref.shape[3]
            kct = kc_ref[0, 0, sl, :].astype(BF16)
            vct = vc_ref[0, 0, pl.ds(h, ncache, stride=N_HEADS), :].astype(BF16)
        for r0 in range(0, tq, tsub):
            qh = q_ref[r0:r0 + tsub, sl]
            zero = jnp.zeros_like(qh)
            qq = jnp.concatenate([jnp.where(first, qh, zero), jnp.where(first, zero, qh)], axis=0)
            if has_cache:
                sc = jnp.concatenate([_nn(qq, kct), _nt(qq, kn)], axis=1)
            else:
                sc = _nt(qq, kn)
            m = jnp.max(sc, axis=-1, keepdims=True)
            e = jnp.exp(sc - m)
            ssum = jnp.sum(e, axis=-1, keepdims=True)
            eb = e.astype(BF16)
            if has_cache:
                pv = _nn(eb[:, :ncache], vct) + _nn(eb[:, ncache:], vn)
            else:
                pv = _nn(eb, vn)
            on = pv * (1.0 / ssum)
            o = on[:tsub] - lam * on[tsub:]
            ms = jnp.mean(o * o, axis=-1, keepdims=True)
            o_ref[r0:r0 + tsub, sl] = (o * lax.rsqrt(ms + EPS) * sg_ref[:, sl]
                                       * (1.0 - lam_init)).astype(o_ref.dtype)


def _attention(q, k, v, lamp, sg, cache, layer, batch, seq_len, lam_init):
    n = q.shape[0]
    tq = min(ATTN_TILE, seq_len)
    nq = seq_len // tq
    has_cache = cache is not None
    in_specs = [
        pl.BlockSpec((tq, ATT_W), lambda b, i: (b * nq + i, 0)),
        pl.BlockSpec((seq_len, ATT_W), lambda b, i: (b, 0)),
        pl.BlockSpec((seq_len, ATT_W), lambda b, i: (b, 0)),
        pl.BlockSpec((4, D_SUB), lambda b, i: (0, 0)),
        pl.BlockSpec((1, ATT_W), lambda b, i: (0, 0)),
    ]
    args = [q, k, v, lamp, sg]
    if has_cache:
        past = cache[0].shape[3]
        in_specs += [pl.BlockSpec((1, 1, ATT_W, past), lambda b, i: (b, layer, 0, 0)),
                     pl.BlockSpec((1, 1, past * N_HEADS, D_V), lambda b, i: (b, layer, 0, 0))]
        args += list(cache)
    return pl.pallas_call(
        functools.partial(_attn_kernel, lam_init=lam_init, has_cache=has_cache),
        grid=(batch, nq),
        in_specs=in_specs,
        out_specs=pl.BlockSpec((tq, ATT_W), lambda b, i: (b * nq + i, 0)),
        out_shape=jax.ShapeDtypeStruct((n, ATT_W), BF16),
        compiler_params=pltpu.CompilerParams(dimension_semantics=("parallel", "parallel"),
                                             vmem_limit_bytes=VMEM_LIMIT),
        name="attn_cache" if has_cache else "attn",
    )(*args)


def _ssm_kernel(ut_ref, h0_ref, p1_ref, pb_ref, yt_ref, hf_ref, r_scr, hin_scr, **sizes):
    for gi in range(ut_ref.shape[0]):
        one = lambda ref, gi=gi: ref.at[pl.ds(gi, 1)]
        _ssm_group(one(ut_ref), one(h0_ref), one(p1_ref), one(pb_ref), one(yt_ref), one(hf_ref),
                   r_scr.at[gi], hin_scr.at[gi], **sizes)


def _ssm_group(ut_ref, h0_ref, p1_ref, pb_ref, yt_ref, hf_ref, r_scr, hin_scr, *,
               ctx_batch, ctx_chunks, lat_batch, lat_chunks):
    T = CHUNK
    assert T & (T - 1) == 0
    W = SSM_CH * T
    lane = lax.broadcasted_iota(jnp.int32, (1, LANES), 1)
    lo = lane < SSM_P
    sgn_lr = jnp.where(lo, 1.0, -1.0).astype(F32)
    sgn_rl = -sgn_lr
    sidx = lax.broadcasted_iota(jnp.int32, (T, LANES), 0)
    srow = sidx.astype(F32)

    def swap(a):
        return pltpu.roll(a, SSM_P, 1)

    def tile_time(a):
        return jnp.concatenate([jnp.broadcast_to(a[i:i + 1, :], (SSM_CH, LANES)) for i in range(T)], axis=0)

    def rep_chan(a):
        return jnp.concatenate([a] * T, axis=0)

    def cmul_const(x, yr2, yi2s):
        return x * yr2 + swap(x) * yi2s

    p1 = p1_ref[0]
    pb = pb_ref[0]

    def direction(d):
        lre2 = p1[3 * d + 0:3 * d + 1]
        lim2 = p1[3 * d + 1:3 * d + 2]
        ls2 = p1[3 * d + 2:3 * d + 3]
        bre2, bim2, cre2, cim2 = pb[4 * d + 0], pb[4 * d + 1], pb[4 * d + 2], pb[4 * d + 3]
        delta = jnp.exp(ls2)
        xr = lre2 * delta
        th = lim2 * delta

        mag1 = jnp.exp(xr)
        lbr2 = mag1 * jnp.cos(th)
        lbi2 = mag1 * jnp.sin(th)
        den = lre2 * lre2 + lim2 * lim2
        nr = lbr2 - 1.0
        cr2 = (nr * lre2 + lbi2 * lim2) / den
        ci2 = (lbi2 * lre2 - nr * lim2) / den
        bbr16 = cr2 * bre2 - ci2 * bim2
        bbi16s = (cr2 * bim2 + ci2 * bre2) * sgn_rl
        bbr2 = rep_chan(bbr16)
        bbi2s = rep_chan(bbi16s)
        ccr2 = rep_chan(cre2)
        cci2s = rep_chan(cim2 * sgn_rl)

        pr, pi = [lbr2], [lbi2]
        for _ in range(T.bit_length() - 1):
            r, i = pr[-1], pi[-1]
            pr.append(r * r - i * i)
            pi.append(2.0 * r * i)
        wpr = wrr = jnp.ones((T, LANES), F32)
        wpi = wri = jnp.zeros((T, LANES), F32)
        for j in range(T.bit_length() - 1):
            bit = ((sidx >> j) & 1) == 1
            r, i = pr[j], pi[j]
            wpr, wpi = jnp.where(bit, wpr * r - wpi * i, wpr), jnp.where(bit, wpr * i + wpi * r, wpi)
            wrr, wri = jnp.where(bit, wrr, wrr * r - wri * i), jnp.where(bit, wri, wrr * i + wri * r)
        wp = jnp.where(lo, wpr, wpi)
        wrev = jnp.where(lo, wrr, wri)

        def times(x, c1, c2s):
            return tile_time(x) * c1 + tile_time(swap(x)) * c2s

        lam_1 = (pr[0], pi[0] * sgn_rl)
        lam_t = (pr[-1], pi[-1] * sgn_rl)
        if d == 0:
            clag = times(wp, ccr2, cci2s)
            bs = times(wrev, bbr2, bbi2s)
            cc = times(cmul_const(wp, *lam_1), ccr2, cci2s)
        else:
            clag = times(wrev, ccr2, cci2s)
            bs = times(wp, bbr2, bbi2s)
            cc = times(cmul_const(wrev, *lam_1), ccr2, cci2s)
        kt = _nt(jnp.where(lo, bbr16, -bbi16s).astype(BF16), clag.astype(BF16))
        bbar = jnp.where(lo, bbr2, bbi2s)
        return kt, bs, cc * sgn_lr, lam_t, bbar

    kt_f, bsf, ccf, lamt_f, bbar_f = direction(0)
    kt_b, bsb, ccb, lamt_b, bbar_b = direction(1)

    sub = LANES // SSM_CH
    right_f = [kt_f if b == 0 else pltpu.roll(kt_f, b * SSM_CH, 1) for b in range(sub)]
    left_b = [kt_b if b == 0 else pltpu.roll(kt_b, W - b * SSM_CH, 1) for b in range(sub)]

    def ring(x, lanes):
        return x if lanes % W == 0 else jnp.concatenate([x[:, W - lanes % W:], x[:, :W - lanes % W]], axis=1)

    blocks = []
    for s in range(T):
        a, b = divmod(s, sub)
        a2, b2 = divmod(T - 1 - s, sub)
        fwd = ring(right_f[b], a * LANES)
        bwd = ring(left_b[b2], W - a2 * LANES)
        first, end = s * SSM_CH, (s + 1) * SSM_CH
        cols = []
        for v in range(W // LANES):
            lo_l, hi_l = v * LANES, (v + 1) * LANES
            fv, bv = fwd[:, lo_l:hi_l], bwd[:, lo_l:hi_l]
            parts = []
            if lo_l >= first:
                parts.append(fv)
            elif hi_l > first:
                parts.append(jnp.where(lane >= first - lo_l, fv, 0.0))
            if hi_l <= end:
                parts.append(bv)
            elif lo_l < end:
                parts.append(jnp.where(lane < end - lo_l, bv, 0.0))
            cols.append(parts[0] if len(parts) == 1 else parts[0] + parts[1])
        blocks.append(jnp.concatenate(cols, axis=1))
    a_t = jnp.concatenate(blocks, axis=0)
    w1 = jnp.concatenate([a_t, bsf, swap(bsf), bsb, swap(bsb)], axis=1).astype(BF16)
    r_scr[...] = _nn(ut_ref[0], w1)

    def chain(row0, nb, nchunks, col, lam_t, h, reverse):
        a1, a2s = lam_t
        hs = swap(h)
        order = range(nchunks - 1, -1, -1) if reverse else range(nchunks)
        hcol = slice(LANES, 2 * LANES) if reverse else slice(0, LANES)
        for c in order:
            rows = slice(row0 + c * nb, row0 + (c + 1) * nb)
            hin_scr[rows, hcol] = h
            s = r_scr[rows, col:col + LANES]
            ss = r_scr[rows, col + LANES:col + 2 * LANES]
            h, hs = h * a1 + hs * a2s + s, hs * a1 - h * a2s + ss
        return h

    zero = jnp.zeros((ctx_batch, LANES), F32)
    chain(0, ctx_batch, ctx_chunks, W, lamt_f, zero, False)
    chain(0, ctx_batch, ctx_chunks, W + 2 * LANES, lamt_b, zero, True)
    s_tiled = tile_time(srow)
    first_tok = jnp.where(s_tiled == 0.0, bbar_f, 0.0).astype(BF16)
    last_tok = jnp.where(s_tiled == float(T - 1), bbar_b, 0.0).astype(BF16)
    last_rows = (ctx_chunks - 1) * ctx_batch
    hf_ref[0, :, 0:LANES] = _nn(ut_ref[0, 0:ctx_batch, :], first_tok)
    hf_ref[0, :, LANES:2 * LANES] = _nn(ut_ref[0, last_rows:last_rows + ctx_batch, :], last_tok)
    lat0 = ctx_batch * ctx_chunks
    h0 = h0_ref[0]
    chain(lat0, lat_batch, lat_chunks, W, lamt_f, h0[:, 0:LANES], False)
    chain(lat0, lat_batch, lat_chunks, W + 2 * LANES, lamt_b, h0[:, LANES:2 * LANES], True)

    cc_cat = jnp.concatenate([ccf, ccb], axis=1).astype(BF16)
    yt_ref[0] = (r_scr[:, 0:W] + _nt(hin_scr[...].astype(BF16), cc_cat)).astype(yt_ref.dtype)


def _ssm(ut, h0, p1, pb, ctx_batch, ctx_chunks, lat_batch, lat_chunks):
    g, n, w = ut.shape
    gs = S5_GROUPS_PER_STEP
    return pl.pallas_call(
        functools.partial(_ssm_kernel, ctx_batch=ctx_batch, ctx_chunks=ctx_chunks,
                          lat_batch=lat_batch, lat_chunks=lat_chunks),
        grid=(g // gs,),
        in_specs=[
            pl.BlockSpec((gs, n, w), lambda i: (i, 0, 0)),
            pl.BlockSpec((gs, lat_batch, 2 * LANES), lambda i: (i, 0, 0)),
            pl.BlockSpec((gs, 8, LANES), lambda i: (i, 0, 0)),
            pl.BlockSpec((gs, 8, SSM_CH, LANES), lambda i: (i, 0, 0, 0)),
        ],
        out_specs=[
            pl.BlockSpec((gs, n, w), lambda i: (i, 0, 0)),
            pl.BlockSpec((gs, ctx_batch, 2 * LANES), lambda i: (i, 0, 0)),
        ],
        out_shape=[jax.ShapeDtypeStruct((g, n, w), BF16),
                   jax.ShapeDtypeStruct((g, ctx_batch, 2 * LANES), F32)],
        scratch_shapes=[pltpu.VMEM((gs, n, w + 4 * LANES), F32), pltpu.VMEM((gs, n, 2 * LANES), F32)],
        compiler_params=pltpu.CompilerParams(dimension_semantics=("parallel",),
                                             vmem_limit_bytes=VMEM_LIMIT),
        name="s5_scan",
    )(ut, h0, p1, pb)


def _tail_kernel(x_ref, oatt_ref, y_ref, u_ref, mod_ref, d_ref, wglu_ref, bglu_ref, wout_ref, g2_ref,
                 wup_ref, cw_ref, cb_ref, wd_ref, o_ref, x1_scr, h2_scr, *, seq_len):
    rows = x_ref.shape[0]
    tf = FFN_SLAB
    mod = mod_ref[0]
    gt1 = mod[2:3]
    sh2 = mod[3:4]
    sc2 = mod[4:5]
    gt2 = mod[5:6]
    for sb in range(rows // MIX_SUB):
        rs = slice(sb * MIX_SUB, (sb + 1) * MIX_SUB)
        y = y_ref[rs, :].astype(F32) + d_ref[...] * u_ref[rs, :]
        z = jax.nn.gelu(y)
        gate = _sigmoid(_nn(z.astype(BF16), wglu_ref[0]) + bglu_ref[...])
        o_ssm = (z * gate).astype(BF16)
        mix = _nn(oatt_ref[rs, :], wout_ref[0, 0:ATT_W, :]) + _nn(o_ssm, wout_ref[0, ATT_W:, :])
        x1 = x_ref[rs, :] + gt1 * mix
        x1_scr[rs, :] = x1
        ms = jnp.mean(x1 * x1, axis=-1, keepdims=True)
        h2 = x1 * lax.rsqrt(ms + EPS) * g2_ref[...]
        h2_scr[rs, :] = (h2 * (1.0 + sc2) + sh2).astype(BF16)

    row8 = lax.broadcasted_iota(jnp.int32, (8, 1), 0)
    h2 = h2_scr[...]

    def conv(up, cw, cb):
        prev = pltpu.roll(up, 1, 0)
        nxt = pltpu.roll(up, rows - 1, 0)
        pp, nn = [], []
        for r in range(0, rows, seq_len):
            e = r + seq_len
            pp += [jnp.where(row8 == 0, 0.0, prev[r:r + 8]), prev[r + 8:e]]
            nn += [nxt[r:e - 8], jnp.where(row8 == 7, 0.0, nxt[e - 8:e])]
        prev = jnp.concatenate(pp, axis=0)
        nxt = jnp.concatenate(nn, axis=0)
        return prev * cw[0:1] + up * cw[1:2] + nxt * cw[2:3] + cb

    acts = []
    for j in range(D_FF // tf):
        cv = slice(j * tf, (j + 1) * tf)
        cg = slice(D_FF + j * tf, D_FF + (j + 1) * tf)
        val = conv(_nn(h2, wup_ref[0, :, cv]), cw_ref[:, cv], cb_ref[:, cv])
        gate = conv(_nn(h2, wup_ref[0, :, cg]), cw_ref[:, cg], cb_ref[:, cg])
        acts.append((gate * _sigmoid(gate) * val).astype(BF16))
    act = jnp.concatenate(acts, axis=1)
    o_ref[...] = x1_scr[...] + gt2 * _nn(act, wd_ref[0])


def _tail(x, oatt, y, u, mods_l, d, wglu, bglu, wout, g2, wup, cw, cb, wdown, layer, seq_len, mod_base):
    n = x.shape[0]
    rows = FFN_ROWS
    per_seq = seq_len // rows if seq_len >= rows else 1
    if mod_base == 0:
        mod_map = lambda i: (0, 0, 0)
    else:
        mod_map = lambda i: (mod_base + i // per_seq, 0, 0)
    row = lambda i: (i, 0)
    const = lambda i: (0, 0)
    lay = lambda i: (layer, 0, 0)
    resident = dict(pipeline_mode=pl.Buffered(1))
    return pl.pallas_call(
        functools.partial(_tail_kernel, seq_len=seq_len),
        grid=(n // rows,),
        in_specs=[
            pl.BlockSpec((rows, D_MODEL), row),
            pl.BlockSpec((rows, ATT_W), row),
            pl.BlockSpec((rows, SSM_W), row),
            pl.BlockSpec((rows, SSM_W), row),
            pl.BlockSpec((1, N_MOD, D_MODEL), mod_map),
            pl.BlockSpec((1, SSM_W), const),
            pl.BlockSpec((1, SSM_W, SSM_W), lay, **resident),
            pl.BlockSpec((1, SSM_W), const),
            pl.BlockSpec((1, D_MODEL, D_MODEL), lay, **resident),
            pl.BlockSpec((1, D_MODEL), const),
            pl.BlockSpec((1, D_MODEL, 2 * D_FF), lay, **resident),
            pl.BlockSpec((3, 2 * D_FF), const),
            pl.BlockSpec((1, 2 * D_FF), const),
            pl.BlockSpec((1, D_FF, D_MODEL), lay, **resident),
        ],
        out_specs=pl.BlockSpec((rows, D_MODEL), row),
        out_shape=jax.ShapeDtypeStruct((n, D_MODEL), F32),
        scratch_shapes=[pltpu.VMEM((rows, D_MODEL), F32), pltpu.VMEM((rows, D_MODEL), BF16)],
        compiler_params=pltpu.CompilerParams(dimension_semantics=("parallel",),
                                             vmem_limit_bytes=VMEM_LIMIT),
        name="layer_tail",
    )(x, oatt, y, u, mods_l, d, wglu, bglu, wout, g2, wup, cw, cb, wdown)


def _rope_tables(n_tokens):
    pos = np.arange(n_tokens)
    row = (pos // GRID_W).astype(np.float64)
    col = (pos % GRID_W).astype(np.float64)
    nf = ROPE_AXIS // 2
    inv_freq = 1.0 / (ROPE_THETA ** (np.arange(nf, dtype=np.float64) / nf))
    lane = np.arange(LANES)
    freq = inv_freq[lane % nf]
    is_col = (lane % D_SUB) >= ROPE_AXIS
    ang = np.where(is_col[None, :], col[:, None], row[:, None]) * freq[None, :]
    second = (lane % ROPE_AXIS) >= nf
    cos = np.cos(ang)
    sin = np.sin(ang)
    sina = np.where(second[None, :], 0.0, -sin)
    sinb = np.where(second[None, :], sin, 0.0)
    return tuple(jnp.asarray(t, dtype=F32) for t in (cos, sina, sinb))


def _chunk_layout_passes(ctx_batch, ctx_len, lat_batch, lat_len):
    return ((0, ctx_batch, ctx_len, 0), (1, lat_batch, lat_len, ctx_batch * (ctx_len // CHUNK)))


def _row_permutation(batch):
    n = batch * CHUNK
    p = np.zeros((n, n), np.float32)
    b, s = np.meshgrid(np.arange(batch), np.arange(CHUNK), indexing="ij")
    p[(s * batch + b).ravel(), (b * CHUNK + s).ravel()] = 1.0
    return p


def _lane_permutation():
    n = GROUPS_PER_BLOCK * LANES
    q = np.zeros((n, n), np.float32)
    t, g, ch = np.meshgrid(np.arange(GROUPS_PER_BLOCK), np.arange(GROUPS_PER_BLOCK), np.arange(SSM_CH),
                           indexing="ij")
    q[(t * LANES + g * SSM_CH + ch).ravel(), (g * LANES + t * SSM_CH + ch).ravel()] = 1.0
    return q


def _to_chunks_kernel(uc_ref, ul_ref, pc_ref, pl_ref, q_ref, o_ref, v_scr, *, passes):
    for idx, batch, seq_len, row0 in passes:
        u_ref = (uc_ref, ul_ref)[idx]
        p_ref = (pc_ref, pl_ref)[idx]
        for c0 in range(0, seq_len // CHUNK, CHUNK_PAIR):
            x = jnp.concatenate(
                [jnp.concatenate([u_ref[b * seq_len + c * CHUNK:b * seq_len + (c + 1) * CHUNK, :]
                                  for b in range(batch)], axis=0) for c in range(c0, c0 + CHUNK_PAIR)], axis=1)
            r2 = _nn(p_ref[...], x.astype(BF16))
            for k in range(CHUNK_PAIR):
                c = c0 + k
                r = r2[:, k * LANES:(k + 1) * LANES]
                rows = slice(row0 + c * batch, row0 + (c + 1) * batch)
                for s in range(CHUNK):
                    j, t = divmod(s, GROUPS_PER_BLOCK)
                    v_scr[j, rows, t * LANES:(t + 1) * LANES] = r[s * batch:(s + 1) * batch]
    for j in range(TIME_BLOCKS):
        w = _nn(v_scr[j].astype(BF16), q_ref[...])
        for g in range(GROUPS_PER_BLOCK):
            o_ref[g, :, j * LANES:(j + 1) * LANES] = w[:, g * LANES:(g + 1) * LANES].astype(o_ref.dtype)


def _from_chunks_kernel(y_ref, pc_ref, pl_ref, q_ref, oc_ref, ol_ref, v_scr, *, passes):
    for j in range(TIME_BLOCKS):
        yj = jnp.concatenate([y_ref[g, :, j * LANES:(j + 1) * LANES] for g in range(GROUPS_PER_BLOCK)], axis=1)
        v_scr[j] = _nn(yj, q_ref[...])
    for idx, batch, seq_len, row0 in passes:
        o_ref = (oc_ref, ol_ref)[idx]
        p_ref = (pc_ref, pl_ref)[idx]
        for c0 in range(0, seq_len // CHUNK, CHUNK_PAIR):
            xs = []
            for c in range(c0, c0 + CHUNK_PAIR):
                rows = slice(row0 + c * batch, row0 + (c + 1) * batch)
                xs.append(jnp.concatenate(
                    [v_scr[s // GROUPS_PER_BLOCK, rows,
                           (s % GROUPS_PER_BLOCK) * LANES:(s % GROUPS_PER_BLOCK + 1) * LANES]
                     for s in range(CHUNK)], axis=0))
            r2 = _nn(p_ref[...], jnp.concatenate(xs, axis=1).astype(BF16))
            for k in range(CHUNK_PAIR):
                c = c0 + k
                for b in range(batch):
                    o_ref[b * seq_len + c * CHUNK:b * seq_len + (c + 1) * CHUNK, :] = (
                        r2[b * CHUNK:(b + 1) * CHUNK, k * LANES:(k + 1) * LANES].astype(o_ref.dtype))


def _to_chunks(uc, ul, ctx_batch, ctx_len, lat_batch, lat_len):
    passes = _chunk_layout_passes(ctx_batch, ctx_len, lat_batch, lat_len)
    rows = ctx_batch * (ctx_len // CHUNK) + lat_batch * (lat_len // CHUNK)
    pc = jnp.asarray(_row_permutation(ctx_batch), dtype=BF16)
    pn = jnp.asarray(_row_permutation(lat_batch), dtype=BF16)
    q = jnp.asarray(_lane_permutation(), dtype=BF16)
    const = lambda j: (0, 0)
    return pl.pallas_call(
        functools.partial(_to_chunks_kernel, passes=passes),
        grid=(SSM_GROUPS // GROUPS_PER_BLOCK,),
        in_specs=[pl.BlockSpec((uc.shape[0], LANES), lambda j: (0, j)),
                  pl.BlockSpec((ul.shape[0], LANES), lambda j: (0, j)),
                  pl.BlockSpec(pc.shape, const), pl.BlockSpec(pn.shape, const), pl.BlockSpec(q.shape, const)],
        out_specs=pl.BlockSpec((GROUPS_PER_BLOCK, rows, SSM_CH * CHUNK), lambda j: (j, 0, 0)),
        out_shape=jax.ShapeDtypeStruct((SSM_GROUPS, rows, SSM_CH * CHUNK), BF16),
        scratch_shapes=[pltpu.VMEM((TIME_BLOCKS, rows, GROUPS_PER_BLOCK * LANES), F32)],
        compiler_params=pltpu.CompilerParams(dimension_semantics=("parallel",),
                                             vmem_limit_bytes=VMEM_LIMIT),
        name="to_chunks",
    )(uc, ul, pc, pn, q)


def _from_chunks(yt, ctx_batch, ctx_len, lat_batch, lat_len):
    passes = _chunk_layout_passes(ctx_batch, ctx_len, lat_batch, lat_len)
    rows = yt.shape[1]
    n_ctx = ctx_batch * ctx_len
    n_lat = lat_batch * lat_len
    pc = jnp.asarray(_row_permutation(ctx_batch).T, dtype=BF16)
    pn = jnp.asarray(_row_permutation(lat_batch).T, dtype=BF16)
    q = jnp.asarray(_lane_permutation().T, dtype=BF16)
    const = lambda j: (0, 0)
    return pl.pallas_call(
        functools.partial(_from_chunks_kernel, passes=passes),
        grid=(SSM_GROUPS // GROUPS_PER_BLOCK,),
        in_specs=[pl.BlockSpec((GROUPS_PER_BLOCK, rows, SSM_CH * CHUNK), lambda j: (j, 0, 0)),
                  pl.BlockSpec(pc.shape, const), pl.BlockSpec(pn.shape, const), pl.BlockSpec(q.shape, const)],
        out_specs=[pl.BlockSpec((n_ctx, LANES), lambda j: (0, j)),
                   pl.BlockSpec((n_lat, LANES), lambda j: (0, j))],
        out_shape=[jax.ShapeDtypeStruct((n_ctx, SSM_W), BF16), jax.ShapeDtypeStruct((n_lat, SSM_W), BF16)],
        scratch_shapes=[pltpu.VMEM((TIME_BLOCKS, rows, GROUPS_PER_BLOCK * LANES), F32)],
        compiler_params=pltpu.CompilerParams(dimension_semantics=("parallel",),
                                             vmem_limit_bytes=VMEM_LIMIT),
        name="from_chunks",
    )(yt, pc, pn, q)


def _dup_lanes(a):
    return jnp.concatenate([a, a], axis=-1)


def kernel(x_prompt, x_sample, cache_k, cache_v, state_ssm, c, c_ctx, w_mod, b_mod, g_norm1, w_in, q_norm, k_norm, lambda_q1, lambda_k1, lambda_q2, lambda_k2, subln_g, ssm_lambda_re, ssm_lambda_im, ssm_log_step, ssm_b_re, ssm_b_im, ssm_c_re, ssm_c_im, ssm_d, w_glu, b_glu, w_out, g_norm2, w_up, conv_w, conv_b, w_down):
    nb_ctx, len_ctx, _ = x_prompt.shape
    nb_lat, len_lat, _ = x_sample.shape
    past = cache_k.shape[2]

    n_cond = 16
    cond = jnp.zeros((n_cond, D_MODEL), F32).at[0].set(c_ctx).at[1:1 + nb_lat].set(c)
    mods = _modulation(cond, w_mod, b_mod).reshape(DEPTH, n_cond, N_MOD, D_MODEL)

    gidx = np.arange(MXU_TILE) // D_SUB
    gmat = jnp.asarray(np.where(gidx[:, None] == gidx[None, :], 1.0 / D_SUB, 0.0), dtype=BF16)
    cache_kt = cache_k.transpose(0, 1, 3, 4, 5, 2).reshape(nb_lat, DEPTH, ATT_W, past)
    cache_vr = cache_v.reshape(nb_lat, DEPTH, past * N_HEADS, D_V)
    rope_tabs = _rope_tables(len_lat)
    win_b = w_in.astype(BF16)
    wout_b = w_out.astype(BF16)
    wglu_b = w_glu.astype(BF16)
    wup_b = w_up.astype(BF16)
    wdown_b = w_down.astype(BF16)

    xp = x_prompt.reshape(nb_ctx * len_ctx, D_MODEL)
    xs = x_sample.reshape(nb_lat * len_lat, D_MODEL)
    ss = []
    kv_out = None
    for l in range(DEPTH):
        lam_init = 0.8 - 0.6 * math.exp(-0.3 * l)
        mods_l = mods[l]
        g1 = g_norm1[l].reshape(1, D_MODEL)
        g2 = g_norm2[l].reshape(1, D_MODEL)
        qg = jnp.tile(q_norm[l], ATT_W // D_SUB).reshape(1, ATT_W)
        kg = jnp.tile(k_norm[l], ATT_W // D_SUB).reshape(1, ATT_W)
        sg = jnp.tile(subln_g[l], N_HEADS).reshape(1, ATT_W)
        lamp = jnp.stack([lambda_q1[l], lambda_k1[l], lambda_q2[l], lambda_k2[l]])
        d = ssm_d[l].reshape(1, SSM_W)
        bglu = b_glu[l].reshape(1, SSM_W)
        cb = conv_b[l].reshape(1, 2 * D_FF)
        cw = conv_w[l]

        zeros = jnp.zeros((SSM_GROUPS, SSM_P), F32)
        step = jnp.broadcast_to(ssm_log_step[l][:, :, None], (2, SSM_GROUPS, SSM_P))
        p1 = _dup_lanes(jnp.stack([ssm_lambda_re[l, 0], ssm_lambda_im[l, 0], step[0],
                                   ssm_lambda_re[l, 1], ssm_lambda_im[l, 1], step[1], zeros, zeros], axis=1))
        bt = lambda a: a.transpose(0, 2, 1)
        pb = _dup_lanes(jnp.stack([bt(ssm_b_re[l, 0]), bt(ssm_b_im[l, 0]), ssm_c_re[l, 0], ssm_c_im[l, 0],
                                   bt(ssm_b_re[l, 1]), bt(ssm_b_im[l, 1]), ssm_c_re[l, 1], ssm_c_im[l, 1]], axis=1))
        h0 = state_ssm[:, l].transpose(2, 0, 1, 4, 3).reshape(SSM_GROUPS, nb_lat, 4 * SSM_P)

        qc, kc_new, vc_new, uc, kt, vt = _premix(xp, mods_l, g1, win_b, l, gmat, qg, kg, None, len_ctx, 0,
                                                 carry=kv_out)
        ql, kl, vl, ul = _premix(xs, mods_l, g1, win_b, l, gmat, qg, kg, rope_tabs, len_lat, 1)

        oc = _attention(qc, kc_new, vc_new, lamp, sg, None, l, nb_ctx, len_ctx, lam_init)
        ol = _attention(ql, kl, vl, lamp, sg, (cache_kt, cache_vr), l, nb_lat, len_lat, lam_init)

        ut = _to_chunks(uc, ul, nb_ctx, len_ctx, nb_lat, len_lat)
        yt, hfin = _ssm(ut, h0, p1, pb, nb_ctx, len_ctx // CHUNK, nb_lat, len_lat // CHUNK)
        yc, yl = _from_chunks(yt, nb_ctx, len_ctx, nb_lat, len_lat)

        xp = _tail(xp, oc, yc, uc, mods_l, d, wglu_b, bglu, wout_b, g2, wup_b, cw, cb, wdown_b, l, len_ctx, 0)
        xs = _tail(xs, ol, yl, ul, mods_l, d, wglu_b, bglu, wout_b, g2, wup_b, cw, cb, wdown_b, l, len_lat, 1)

        kv_out = (kt, vt)
        ss.append(hfin.reshape(SSM_GROUPS, nb_ctx, 2, 2, SSM_P).transpose(1, 2, 0, 4, 3))

    new_k = kv_out[0].reshape(nb_ctx, DEPTH, N_HEADS, 2, D_SUB, len_ctx).transpose(0, 1, 5, 2, 3, 4)
    new_v = kv_out[1].reshape(nb_ctx, DEPTH, len_ctx, N_HEADS, D_V)
    return (xp.reshape(nb_ctx, len_ctx, D_MODEL), xs.reshape(nb_lat, len_lat, D_MODEL),
            new_k, new_v, jnp.stack(ss, axis=1))
```
